```python
import jax, jax.numpy as jnp
from jax import lax
import numpy as np

D_MODEL = 1024
BATCH = 8
SEQ = 2048
DEPTH = 2
DEC_BATCH = 32
DEC_SEQ = 1
PAST_LEN = 8192
PAGE_SIZE = 128

N_A_LAYERS = DEPTH // 2
N_B_LAYERS = DEPTH - N_A_LAYERS
CHUNK = 128
GATE_WIDTH = 2 * D_MODEL
N_GATE_GROUPS = 8
GROUP_WIDTH = GATE_WIDTH // N_GATE_GROUPS
HEAD_DIM = 128
N_HEADS = D_MODEL // HEAD_DIM
ROT_DIM = HEAD_DIM // 4
ROPE_THETA = 500000.0
MOBA_BLOCK = 256
MOBA_TOPK = 3
Q_BLOCK = 64
N_EXPERT_GROUPS = 4
EXPERTS_PER_GROUP = 4
N_EXPERTS = N_EXPERT_GROUPS * EXPERTS_PER_GROUP
EXPERT_TOPK = 2
EXPERT_FF = D_MODEL // 2
EPS = 1e-6

kernel_name = 'hybrid_gmlp_moba_hmoe_step'


def rms_norm(x, g):
    xf = x.astype(jnp.float32)
    y = xf * lax.rsqrt(jnp.mean(xf * xf, axis=-1, keepdims=True) + EPS)
    return (y * g.astype(jnp.float32)).astype(x.dtype)


def layer_norm(x, g, b):
    xf = x.astype(jnp.float32)
    xc = xf - jnp.mean(xf, axis=-1, keepdims=True)
    y = xc * lax.rsqrt(jnp.mean(xc * xc, axis=-1, keepdims=True) + EPS)
    return (y * g.astype(jnp.float32) + b.astype(jnp.float32)).astype(x.dtype)


def partial_rope(x, pos):
    half = ROT_DIM // 2
    inv_freq = ROPE_THETA ** (-jnp.arange(half, dtype=jnp.float32) / half)
    ang = pos.astype(jnp.float32)[:, None] * inv_freq[None, :]
    cos = jnp.cos(ang)[None, :, None, :]
    sin = jnp.sin(ang)[None, :, None, :]
    xr = x[..., :ROT_DIM].astype(jnp.float32)
    x1, x2 = xr[..., :half], xr[..., half:]
    rot = jnp.concatenate([x1 * cos - x2 * sin, x2 * cos + x1 * sin], axis=-1)
    return jnp.concatenate([rot.astype(x.dtype), x[..., ROT_DIM:]], axis=-1)


def chunk_gmlp(x, norm_g, w_in, ln_g, ln_b, w_s, b_s, w_out):
    bsz, t_len, _ = x.shape
    z = jax.nn.gelu(rms_norm(x, norm_g) @ w_in)
    u, v = z[..., :GATE_WIDTH], z[..., GATE_WIDTH:]
    v = layer_norm(v, ln_g, ln_b)
    n_chunks = -(-t_len // CHUNK)
    vp = jnp.pad(v, ((0, 0), (0, n_chunks * CHUNK - t_len), (0, 0)))
    vp = vp.reshape(bsz, n_chunks, CHUNK, N_GATE_GROUPS, GROUP_WIDTH)
    causal = jnp.tril(jnp.ones((CHUNK, CHUNK), dtype=bool))
    ws = jnp.where(causal[None], w_s, jnp.zeros((), w_s.dtype))
    mixed = jnp.einsum('gts,bnsgc->bntgc', ws, vp) + b_s.T[None, None, :, :, None]
    mixed = mixed.reshape(bsz, n_chunks * CHUNK, GATE_WIDTH)[:, :t_len]
    return (u * mixed) @ w_out, v


def hier_moe(x, norm_g, w_rg, b_rg, w_re, b_re, w_gate, w_up, w_down):
    h = rms_norm(x, norm_g)
    lg = (h @ w_rg + b_rg).astype(jnp.float32)
    g_sel = jnp.argmax(lg, axis=-1)
    p_grp = jnp.max(jax.nn.softmax(lg, axis=-1), axis=-1)
    le = (h @ w_re + b_re).astype(jnp.float32)
    le = le.reshape(le.shape[:-1] + (N_EXPERT_GROUPS, EXPERTS_PER_GROUP))
    idx = jnp.broadcast_to(g_sel[..., None, None], le.shape[:-2] + (1, EXPERTS_PER_GROUP))
    le_g = jnp.take_along_axis(le, idx, axis=-2)[..., 0, :]
    top_v, top_i = lax.top_k(le_g, EXPERT_TOPK)
    w_top = jax.nn.softmax(top_v, axis=-1) * p_grp[..., None]
    e_ids = g_sel[..., None] * EXPERTS_PER_GROUP + top_i
    gates = jnp.sum(jax.nn.one_hot(e_ids, N_EXPERTS, dtype=jnp.float32) * w_top[..., None], axis=-2).astype(x.dtype)
    y = jnp.zeros_like(x)
    for e in range(N_EXPERTS):
        he = jax.nn.silu(h @ w_gate[e]) * (h @ w_up[e])
        y = y + gates[..., e:e + 1] * (he @ w_down[e])
    return y


def shared_kv(x, pos, norm_g, w_kv, k_g):
    bsz, t_len, _ = x.shape
    kv = rms_norm(x, norm_g) @ w_kv
    hd = N_HEADS * HEAD_DIM
    k = kv[..., :hd].reshape(bsz, t_len, N_HEADS, HEAD_DIM)
    v = kv[..., hd:].reshape(bsz, t_len, N_HEADS, HEAD_DIM)
    return partial_rope(rms_norm(k, k_g), pos), v


def moba_attend(q, q_start, k_all, v_all):
    bsz, t_q = q.shape[:2]
    l_kv = k_all.shape[1]
    n_blk = -(-l_kv // MOBA_BLOCK)
    k_sel_n = min(MOBA_TOPK, n_blk)
    kpad = n_blk * MOBA_BLOCK - l_kv
    kb = jnp.pad(k_all, ((0, 0), (0, kpad), (0, 0), (0, 0))).reshape(bsz, n_blk, MOBA_BLOCK, N_HEADS, HEAD_DIM)
    vb = jnp.pad(v_all, ((0, 0), (0, kpad), (0, 0), (0, 0))).reshape(bsz, n_blk, MOBA_BLOCK, N_HEADS, HEAD_DIM)
    k_mean = jnp.mean(kb.astype(jnp.float32), axis=2)
    qb = min(Q_BLOCK, t_q)
    n_qb = -(-t_q // qb)
    qq = jnp.pad(q, ((0, 0), (0, n_qb * qb - t_q), (0, 0), (0, 0))).reshape(bsz, n_qb, qb, N_HEADS, HEAD_DIM)
    scale = HEAD_DIM ** -0.5
    n_sel = k_sel_n * MOBA_BLOCK
    blk_ids = jnp.arange(n_blk)
    slot_ids = jnp.arange(k_sel_n)
    head_ids = jnp.arange(N_HEADS)[:, None, None]
    own_offsets = jnp.arange(MOBA_BLOCK)

    def one_sequence(seq_args):
        q_s, kb_s, vb_s, km_s = seq_args

        def one_query_block(blk_args):
            qi, q_blk = blk_args
            start = q_start + qi * qb
            q_pos = start + jnp.arange(qb)
            cur = start // MOBA_BLOCK
            qh = q_blk.transpose(1, 0, 2)
            gate = jnp.einsum('hqd,nhd->hqn', qh.astype(jnp.float32), km_s)
            gate = jnp.where(blk_ids < cur, gate, -jnp.inf)
            _, sel = lax.top_k(gate, k_sel_n)
            k_g = kb_s[sel, :, head_ids]
            v_g = vb_s[sel, :, head_ids]
            s_sel = jnp.einsum('hqd,hqjkd->hqjk', qh, k_g, preferred_element_type=jnp.float32) * scale
            s_sel = jnp.where((slot_ids < cur)[None, None, :, None], s_sel, -jnp.inf)
            k_own = lax.dynamic_index_in_dim(kb_s, cur, axis=0, keepdims=False)
            v_own = lax.dynamic_index_in_dim(vb_s, cur, axis=0, keepdims=False)
            s_own = jnp.einsum('hqd,khd->hqk', qh, k_own, preferred_element_type=jnp.float32) * scale
            k_pos = cur * MOBA_BLOCK + own_offsets
            s_own = jnp.where(k_pos[None, None, :] <= q_pos[None, :, None], s_own, -jnp.inf)
            logits = jnp.concatenate([s_sel.reshape(N_HEADS, qb, n_sel), s_own], axis=-1)
            p = jax.nn.softmax(logits, axis=-1).astype(v_g.dtype)
            o = (jnp.einsum('hqjk,hqjkd->hqd', p[..., :n_sel].reshape(N_HEADS, qb, k_sel_n, MOBA_BLOCK), v_g)
                 + jnp.einsum('hqk,khd->hqd', p[..., n_sel:], v_own))
            return o.transpose(1, 0, 2)

        return lax.map(one_query_block, (jnp.arange(n_qb, dtype=jnp.int32), q_s))

    out = lax.map(one_sequence, (qq, kb, vb, k_mean))
    return out.reshape(bsz, n_qb * qb, N_HEADS, HEAD_DIM)[:, :t_q]


def moba_mixer(x, pos0, k_all, v_all, norm_g, w_q, q_g, w_o):
    bsz, t_len, _ = x.shape
    q = (rms_norm(x, norm_g) @ w_q).reshape(bsz, t_len, N_HEADS, HEAD_DIM)
    q = partial_rope(rms_norm(q, q_g), pos0 + jnp.arange(t_len, dtype=jnp.int32))
    o = moba_attend(q, pos0, k_all, v_all)
    return o.reshape(bsz, t_len, N_HEADS * HEAD_DIM) @ w_o


def gather_pages(pool, page_table):
    rows = pool[page_table]
    return rows.reshape((page_table.shape[0], -1) + pool.shape[2:])


def setup_inputs(seed: int = 0) -> dict:
    key = jax.random.key(seed)
    ks = jax.random.split(key, 32)
    n_pages = PAST_LEN // PAGE_SIZE
    n_used = DEC_BATCH * n_pages
    n_pool = n_used + -(-n_used // 4)
    hd = N_HEADS * HEAD_DIM

    def nrm(k, shape, scale):
        return jax.random.normal(k, shape, jnp.float32) * scale

    def gain(k, shape):
        return 1.0 + 0.02 * jax.random.normal(k, shape, jnp.float32)

    page_table = jax.random.permutation(ks[0], n_pool)[:n_used].reshape(DEC_BATCH, n_pages).astype(jnp.int32)
    return {
        'x_prompt': nrm(ks[1], (BATCH, SEQ, D_MODEL), 1.0),
        'x_sample': nrm(ks[2], (DEC_BATCH, DEC_SEQ, D_MODEL), 1.0),
        'cache_k': nrm(ks[3], (n_pool, PAGE_SIZE, N_HEADS, HEAD_DIM), 1.0),
        'cache_v': nrm(ks[4], (n_pool, PAGE_SIZE, N_HEADS, HEAD_DIM), 1.0),
        'page_table': page_table,
        'a_norm_g': gain(ks[5], (N_A_LAYERS, D_MODEL)),
        'a_w_in': nrm(ks[6], (N_A_LAYERS, D_MODEL, 2 * GATE_WIDTH), D_MODEL ** -0.5),
        'a_ln_g': gain(ks[7], (N_A_LAYERS, GATE_WIDTH)),
        'a_ln_b': nrm(ks[8], (N_A_LAYERS, GATE_WIDTH), 0.02),
        'a_w_s': nrm(ks[9], (N_A_LAYERS, N_GATE_GROUPS, CHUNK, CHUNK), CHUNK ** -0.5),
        'a_b_s': gain(ks[10], (N_A_LAYERS, N_GATE_GROUPS, CHUNK)),
        'a_w_out': nrm(ks[11], (N_A_LAYERS, GATE_WIDTH, D_MODEL), GATE_WIDTH ** -0.5),
        'kv_norm_g': gain(ks[12], (D_MODEL,)),
        'w_kv': nrm(ks[13], (D_MODEL, 2 * hd), D_MODEL ** -0.5),
        'k_norm_g': gain(ks[14], (HEAD_DIM,)),
        'b_norm_g': gain(ks[15], (N_B_LAYERS, D_MODEL)),
        'b_w_q': nrm(ks[16], (N_B_LAYERS, D_MODEL, hd), D_MODEL ** -0.5),
        'q_norm_g': gain(ks[17], (N_B_LAYERS, HEAD_DIM)),
        'b_w_o': nrm(ks[18], (N_B_LAYERS, hd, D_MODEL), hd ** -0.5),
        'moe_norm_g': gain(ks[19], (DEPTH, D_MODEL)),
        'moe_w_rg': nrm(ks[20], (DEPTH, D_MODEL, N_EXPERT_GROUPS), D_MODEL ** -0.5),
        'moe_b_rg': nrm(ks[21], (DEPTH, N_EXPERT_GROUPS), 0.01),
        'moe_w_re': nrm(ks[22], (DEPTH, D_MODEL, N_EXPERTS), D_MODEL ** -0.5),
        'moe_b_re': nrm(ks[23], (DEPTH, N_EXPERTS), 0.01),
        'moe_w_gate': nrm(ks[24], (DEPTH, N_EXPERTS, D_MODEL, EXPERT_FF), D_MODEL ** -0.5),
        'moe_w_up': nrm(ks[25], (DEPTH, N_EXPERTS, D_MODEL, EXPERT_FF), D_MODEL ** -0.5),
        'moe_w_down': nrm(ks[26], (DEPTH, N_EXPERTS, EXPERT_FF, D_MODEL), EXPERT_FF ** -0.5),
    }


def reference(x_prompt, x_sample, cache_k, cache_v, page_table,
              a_norm_g, a_w_in, a_ln_g, a_ln_b, a_w_s, a_b_s, a_w_out,
              kv_norm_g, w_kv, k_norm_g,
              b_norm_g, b_w_q, q_norm_g, b_w_o,
              moe_norm_g, moe_w_rg, moe_b_rg, moe_w_re, moe_b_re, moe_w_gate, moe_w_up, moe_w_down):
    seq = x_prompt.shape[1]
    dec_seq = x_sample.shape[1]
    past = page_table.shape[1] * cache_k.shape[1]
    pos_p = jnp.arange(seq, dtype=jnp.int32)
    pos_s = past + jnp.arange(dec_seq, dtype=jnp.int32)
    last_chunk_start = ((seq - 1) // CHUNK) * CHUNK
    xp, xs = x_prompt, x_sample
    v_rows_p, v_rows_s = [], []
    for i in range(DEPTH):
        if i < N_A_LAYERS:
            a_args = (a_norm_g[i], a_w_in[i], a_ln_g[i], a_ln_b[i], a_w_s[i], a_b_s[i], a_w_out[i])
            mp, rows_p = chunk_gmlp(xp, *a_args)
            ms, rows_s = chunk_gmlp(xs, *a_args)
            xp, xs = xp + mp, xs + ms
            v_rows_p.append(rows_p[:, last_chunk_start:])
            v_rows_s.append(rows_s)
        else:
            j = i - N_A_LAYERS
            b_args = (b_norm_g[j], b_w_q[j], q_norm_g[j], b_w_o[j])
            xp = xp + moba_mixer(xp, 0, k_p, v_p, *b_args)
            xs = xs + moba_mixer(xs, past, k_all_s, v_all_s, *b_args)
        m_args = (moe_norm_g[i], moe_w_rg[i], moe_b_rg[i], moe_w_re[i], moe_b_re[i],
                  moe_w_gate[i], moe_w_up[i], moe_w_down[i])
        xp = xp + hier_moe(xp, *m_args)
        xs = xs + hier_moe(xs, *m_args)
        if i == N_A_LAYERS - 1:
            k_p, v_p = shared_kv(xp, pos_p, kv_norm_g, w_kv, k_norm_g)
            k_s, v_s = shared_kv(xs, pos_s, kv_norm_g, w_kv, k_norm_g)
            k_all_s = jnp.concatenate([gather_pages(cache_k, page_table), k_s], axis=1)
            v_all_s = jnp.concatenate([gather_pages(cache_v, page_table), v_s], axis=1)
    return (xp, xs, jnp.stack(v_rows_p), jnp.stack(v_rows_s), k_p, v_p, k_s, v_s)
```

```python
import functools

import jax
import jax.numpy as jnp
from jax import lax
from jax.experimental import pallas as pl
from jax.experimental.pallas import tpu as pltpu

D_MODEL = 1024
CHUNK = 128
GATE_WIDTH = 2 * D_MODEL
N_GATE_GROUPS = 8
GROUP_WIDTH = GATE_WIDTH // N_GATE_GROUPS
HEAD_DIM = 128
N_HEADS = D_MODEL // HEAD_DIM
ROT_DIM = HEAD_DIM // 4
ROT_HALF = ROT_DIM // 2
ROPE_THETA = 500000.0
MOBA_BLOCK = 256
MOBA_TOPK = 3
N_EXPERT_GROUPS = 4
EXPERTS_PER_GROUP = 4
N_EXPERTS = N_EXPERT_GROUPS * EXPERTS_PER_GROUP
EXPERT_FF = D_MODEL // 2
EPS = 1e-6

LANES = 128
ROW_TILE = 256
EXPERT_TILE = 256
EXPERT_TILE_SMALL = 16
IN_PROJ_COLS = 512
PAGES_PER_STEP = 16
VMEM_LIMIT = 60 * 1024 * 1024

F32 = jnp.float32
BF16 = jnp.bfloat16
NEG_INF = float("-inf")


def _params(n_axes):
    return pltpu.CompilerParams(dimension_semantics=("arbitrary",) * n_axes, vmem_limit_bytes=VMEM_LIMIT)


def _rms(x, g):
    return x * lax.rsqrt(jnp.mean(x * x, axis=-1, keepdims=True) + EPS) * g


def _mm(a, b, exact):
    if exact:
        return jnp.dot(a, b, precision=lax.Precision.HIGHEST, preferred_element_type=F32)
    return jnp.dot(a.astype(BF16), b, preferred_element_type=F32)


def _full(shape):
    nd = len(shape)
    return pl.BlockSpec(shape, lambda *_: (0,) * nd, pipeline_mode=pl.Buffered(1))


def _route(hm, wr_ref, br_ref, exact):
    logits = _mm(hm, wr_ref[...], exact) + br_ref[...]
    lane = lax.broadcasted_iota(jnp.int32, logits.shape, 1).astype(F32)
    lg = jnp.where(lane < N_EXPERT_GROUPS, logits, NEG_INF)
    mg = jnp.max(lg, axis=-1, keepdims=True)
    g_sel = jnp.min(jnp.where(lg == mg, lane, float(LANES)), axis=-1, keepdims=True)
    p_grp = 1.0 / jnp.sum(jnp.exp(lg - mg), axis=-1, keepdims=True)
    lo = N_EXPERT_GROUPS + g_sel * EXPERTS_PER_GROUP
    le = jnp.where((lane >= lo) & (lane < lo + EXPERTS_PER_GROUP), logits, NEG_INF)
    v1 = jnp.max(le, axis=-1, keepdims=True)
    i1 = jnp.min(jnp.where(le == v1, lane, float(LANES)), axis=-1, keepdims=True)
    le2 = jnp.where(lane == i1, NEG_INF, le)
    v2 = jnp.max(le2, axis=-1, keepdims=True)
    i2 = jnp.min(jnp.where(le2 == v2, lane, float(LANES)), axis=-1, keepdims=True)
    t = jnp.exp(v2 - v1)
    w1 = p_grp / (1.0 + t)
    w2 = p_grp * t / (1.0 + t)
    out = jnp.where(lane == 0.0, i1 - N_EXPERT_GROUPS, 0.0)
    out = jnp.where(lane == 1.0, i2 - N_EXPERT_GROUPS, out)
    out = jnp.where(lane == 2.0, w1, out)
    out = jnp.where(lane == 3.0, w2, out)
    return out


def _moe_prenorm_and_route(x, mg_ref, wr_ref, br_ref, hm_ref, route_ref, exact):
    hm = _rms(x, mg_ref[...])
    hm_ref[...] = hm
    route_ref[...] = _route(hm, wr_ref, br_ref, exact)


def _gmlp_kernel(sample, tm, x_ref, ng_ref, win_ref, lng_ref, lnb_ref, ws_ref, bs_ref, wout_ref,
                 mg_ref, wr_ref, br_ref, x1_ref, vrows_ref, hm_ref, route_ref, u_scr, v_scr, gated_scr):
    x = x_ref[...]
    h = _rms(x, ng_ref[...])
    if not sample:
        h = h.astype(BF16)
    n_col = 2 * GATE_WIDTH // IN_PROJ_COLS
    n_ucol = GATE_WIDTH // IN_PROJ_COLS
    vsum = jnp.zeros((tm, 1), F32)
    for c in range(n_col):
        z = _mm(h, win_ref[:, c * IN_PROJ_COLS:(c + 1) * IN_PROJ_COLS], sample)
        z = jax.nn.gelu(z, approximate=True)
        if c < n_ucol:
            u_scr[:, c * IN_PROJ_COLS:(c + 1) * IN_PROJ_COLS] = z
        else:
            v_scr[:, (c - n_ucol) * IN_PROJ_COLS:(c - n_ucol + 1) * IN_PROJ_COLS] = z
            vsum = vsum + jnp.sum(z, axis=-1, keepdims=True)
    mean = vsum / GATE_WIDTH
    vss = jnp.zeros((tm, 1), F32)
    for c in range(n_ucol):
        xc = v_scr[:, c * IN_PROJ_COLS:(c + 1) * IN_PROJ_COLS] - mean
        vss = vss + jnp.sum(xc * xc, axis=-1, keepdims=True)
    rstd = lax.rsqrt(vss / GATE_WIDTH + EPS)
    for c in range(n_ucol):
        cs = slice(c * IN_PROJ_COLS, (c + 1) * IN_PROJ_COLS)
        v_scr[:, cs] = (v_scr[:, cs] - mean) * rstd * lng_ref[:, cs] + lnb_ref[:, cs]

    if sample:
        vrows_ref[...] = v_scr[...]
        gated_scr[...] = u_scr[...] * (v_scr[...] * ws_ref[...] + bs_ref[...])
    else:
        vrows_ref[0] = v_scr[tm - CHUNK:tm, :]
        t_out = lax.broadcasted_iota(jnp.int32, (CHUNK, CHUNK), 0)
        s_in = lax.broadcasted_iota(jnp.int32, (CHUNK, CHUNK), 1)
        for g in range(N_GATE_GROUPS):
            gs = slice(g * GROUP_WIDTH, (g + 1) * GROUP_WIDTH)
            wsg = jnp.where(s_in <= t_out, ws_ref[g], 0.0).astype(BF16)
            bias = bs_ref[:, g:g + 1]
            for c in range(tm // CHUNK):
                rs = slice(c * CHUNK, (c + 1) * CHUNK)
                mixed = jnp.dot(wsg, v_scr[rs, gs].astype(BF16), preferred_element_type=F32) + bias
                gated_scr[rs, gs] = (u_scr[rs, gs] * mixed).astype(BF16)

    x1 = x + _mm(gated_scr[...], wout_ref[...], sample)
    x1_ref[...] = x1
    _moe_prenorm_and_route(x1, mg_ref, wr_ref, br_ref, hm_ref, route_ref, sample)


def _gmlp(x, w, sample, seq):
    n = x.shape[0]
    mw = w["f32"] if sample else w["bf16"]
    tm = n if sample else ROW_TILE
    row = lambda i: (i, 0)
    if sample:
        ws_spec, bs_spec = _full((1, GATE_WIDTH)), _full((1, GATE_WIDTH))
        vrows_shape = jax.ShapeDtypeStruct((n, GATE_WIDTH), F32)
        vrows_spec = pl.BlockSpec((tm, GATE_WIDTH), row)
    else:
        ws_spec, bs_spec = _full((N_GATE_GROUPS, CHUNK, CHUNK)), _full((CHUNK, N_GATE_GROUPS))
        tiles_per_seq = seq // tm
        vrows_shape = jax.ShapeDtypeStruct((n // seq, CHUNK, GATE_WIDTH), F32)
        vrows_spec = pl.BlockSpec((1, CHUNK, GATE_WIDTH), lambda i: (i // tiles_per_seq, 0, 0))
    return pl.pallas_call(
        functools.partial(_gmlp_kernel, sample, tm),
        grid=(n // tm,),
        in_specs=[pl.BlockSpec((tm, D_MODEL), row), _full((1, D_MODEL)), _full((D_MODEL, 2 * GATE_WIDTH)),
                  _full((1, GATE_WIDTH)), _full((1, GATE_WIDTH)), ws_spec, bs_spec, _full((GATE_WIDTH, D_MODEL)),
                  _full((1, D_MODEL)), _full((D_MODEL, LANES)), _full((1, LANES))],
        out_specs=[pl.BlockSpec((tm, D_MODEL), row), vrows_spec, pl.BlockSpec((tm, D_MODEL), row),
                   pl.BlockSpec((tm, LANES), row)],
        out_shape=[jax.ShapeDtypeStruct((n, D_MODEL), F32), vrows_shape,
                   jax.ShapeDtypeStruct((n, D_MODEL), F32), jax.ShapeDtypeStruct((n, LANES), F32)],
        scratch_shapes=[pltpu.VMEM((tm, GATE_WIDTH), F32), pltpu.VMEM((tm, GATE_WIDTH), F32),
                        pltpu.VMEM((tm, GATE_WIDTH), F32 if sample else BF16)],
        compiler_params=_params(1),
        name="gmlp_sample" if sample else "gmlp_prompt",
    )(x, w["a_norm_g"], mw["a_w_in"], w["a_ln_g"], w["a_ln_b"],
      w["a_ws0"] if sample else w["a_w_s"], w["a_bs0"] if sample else w["a_b_s_t"], mw["a_w_out"],
      w["moe_norm_g"][0], mw["moe_w_route"][0], w["moe_b_route"][0])


def _expert_kernel(tmg, exact, te_ref, nv_ref, sp_ref, h_hbm, sw_ref, wg_ref, wu_ref, wd_ref, out_hbm,
                   xbuf, ybuf, sem_in, sem_out):
    del te_ref
    i = pl.program_id(0)
    nv = nv_ref[i]
    base = i * tmg

    def gather_copy(r, tok):
        return pltpu.make_async_copy(h_hbm.at[pl.ds(tok, 1)], xbuf.at[pl.ds(r, 1)], sem_in)

    def scatter_copy(r, pair):
        return pltpu.make_async_copy(ybuf.at[pl.ds(r, 1)], out_hbm.at[pl.ds(pair, 1)], sem_out)

    @pl.when(nv > 0)
    def _():
        def start_gather(r, c):
            gather_copy(r, sp_ref[base + r] >> 1).start()
            return c

        def wait_gather(r, c):
            gather_copy(r, 0).wait()
            return c

        lax.fori_loop(0, tmg, start_gather, 0, unroll=8)
        lax.fori_loop(0, tmg, wait_gather, 0, unroll=8)

        xb = xbuf[...] if exact else xbuf[...].astype(BF16)
        a = jax.nn.silu(_mm(xb, wg_ref[0], exact)) * _mm(xb, wu_ref[0], exact)
        ybuf[...] = _mm(a, wd_ref[0], exact) * sw_ref[...]

        def start_scatter(r, c):
            scatter_copy(r, sp_ref[base + r]).start()
            return c

        def wait_scatter(r, c):
            scatter_copy(r, 0).wait()
            return c

        lax.fori_loop(0, nv, start_scatter, 0)
        lax.fori_loop(0, nv, wait_scatter, 0)


def _dispatch(route, tmg):
    n = route.shape[0]
    n_pairs = 2 * n
    e_flat = route[:, 0:2].astype(jnp.int32).reshape(n_pairs)
    w_flat = route[:, 2:4].reshape(n_pairs)
    n_slots = -(-(n_pairs + N_EXPERTS * (tmg - 1)) // tmg) * tmg
    n_tiles = n_slots // tmg
    order = jnp.argsort(e_flat, stable=True).astype(jnp.int32)
    counts = jnp.sum((e_flat[:, None] == jnp.arange(N_EXPERTS, dtype=jnp.int32)[None, :]).astype(jnp.int32), axis=0)
    padded = -(-counts // tmg) * tmg
    g_end = jnp.cumsum(padded)
    g_start = g_end - padded
    c_start = jnp.cumsum(counts) - counts
    tile_start = jnp.arange(n_tiles, dtype=jnp.int32) * tmg
    tile_expert = jnp.minimum(jnp.searchsorted(g_end, tile_start, side="right"), N_EXPERTS - 1).astype(jnp.int32)
    tile_valid = jnp.clip(counts[tile_expert] - (tile_start - g_start[tile_expert]), 0, tmg).astype(jnp.int32)
    slot = jnp.arange(n_slots, dtype=jnp.int32)
    s_exp = tile_expert[slot // tmg]
    s_rank = slot - g_start[s_exp]
    s_ok = s_rank < counts[s_exp]
    s_pair = jnp.where(s_ok, order[jnp.clip(c_start[s_exp] + s_rank, 0, n_pairs - 1)], 0).astype(jnp.int32)
    s_w = jnp.where(s_ok, w_flat[s_pair], 0.0).reshape(n_slots, 1)
    return tile_expert, tile_valid, s_pair, s_w


def _experts(hm, route, w, layer, tmg, exact):
    n = hm.shape[0]
    mw = w["f32"] if exact else w["bf16"]
    w_gate, w_up, w_down = mw["moe_w_gate"][layer], mw["moe_w_up"][layer], mw["moe_w_down"][layer]
    tile_expert, tile_valid, s_pair, s_w = _dispatch(route, tmg)
    n_tiles = tile_expert.shape[0]
    by_expert = lambda i, te, nv, sp: (te[i], 0, 0)
    grid_spec = pltpu.PrefetchScalarGridSpec(
        num_scalar_prefetch=3,
        grid=(n_tiles,),
        in_specs=[pl.BlockSpec(memory_space=pl.ANY),
                  pl.BlockSpec((tmg, 1), lambda i, te, nv, sp: (i, 0)),
                  pl.BlockSpec((1, D_MODEL, EXPERT_FF), by_expert),
                  pl.BlockSpec((1, D_MODEL, EXPERT_FF), by_expert),
                  pl.BlockSpec((1, EXPERT_FF, D_MODEL), by_expert)],
        out_specs=pl.BlockSpec(memory_space=pl.ANY),
        scratch_shapes=[pltpu.VMEM((tmg, D_MODEL), F32), pltpu.VMEM((tmg, D_MODEL), F32),
                        pltpu.SemaphoreType.DMA, pltpu.SemaphoreType.DMA],
    )
    out = pl.pallas_call(
        functools.partial(_expert_kernel, tmg, exact),
        grid_spec=grid_spec,
        out_shape=jax.ShapeDtypeStruct((2 * n, D_MODEL), F32),
        compiler_params=_params(1),
        name="experts",
    )(tile_expert, tile_valid, s_pair, hm, s_w, w_gate, w_up, w_down)
    return out.reshape(n, 2 * D_MODEL)


def _rope(xh, cos, sin, lane):
    swapped = jnp.where(lane < ROT_HALF, pltpu.roll(xh, LANES - ROT_HALF, 1), pltpu.roll(xh, ROT_HALF, 1))
    return xh * cos + swapped * sin


def _kvq_kernel(with_kmean, exact, x1_ref, yp_ref, kvg_ref, wkv_ref, kg_ref, cos_ref, sin_ref, bng_ref, wq_ref, qg_ref,
                x2_ref, k_ref, v_ref, kb_ref, vb_ref, q_ref, *maybe_km_ref):
    x2 = x1_ref[...] + (yp_ref[:, :D_MODEL] + yp_ref[:, D_MODEL:])
    x2_ref[...] = x2
    cos = cos_ref[...]
    sin = sin_ref[...]
    lane = lax.broadcasted_iota(jnp.int32, cos.shape, 1)
    kv = _mm(_rms(x2, kvg_ref[...]), wkv_ref[...], exact)
    for h in range(N_HEADS):
        hs = slice(h * HEAD_DIM, (h + 1) * HEAD_DIM)
        kh = _rope(_rms(kv[:, hs], kg_ref[...]), cos, sin, lane)
        k_ref[:, hs] = kh
        kb_ref[:, hs] = kh.astype(BF16)
        if with_kmean:
            maybe_km_ref[0][0, :, hs] = jnp.mean(kh, axis=0, keepdims=True)
    v = kv[:, D_MODEL:]
    v_ref[...] = v
    vb_ref[...] = v.astype(BF16)
    q = _mm(_rms(x2, bng_ref[...]), wq_ref[...], exact)
    for h in range(N_HEADS):
        hs = slice(h * HEAD_DIM, (h + 1) * HEAD_DIM)
        q_ref[:, hs] = _rope(_rms(q[:, hs], qg_ref[...]), cos, sin, lane)


def _kvq(x1, ypairs, cos_tab, sin_tab, w, with_kmean, exact):
    n = x1.shape[0]
    mw = w["f32"] if exact else w["bf16"]
    tm = ROW_TILE if with_kmean else n
    row = lambda i: (i, 0)
    tab_tiles = cos_tab.shape[0] // tm
    tab = lambda i: (i % tab_tiles, 0)
    out_specs = [pl.BlockSpec((tm, D_MODEL), row)] * 6
    out_shape = [jax.ShapeDtypeStruct((n, D_MODEL), F32)] * 3 + [jax.ShapeDtypeStruct((n, D_MODEL), BF16)] * 2 \
        + [jax.ShapeDtypeStruct((n, D_MODEL), F32)]
    if with_kmean:
        out_specs.append(pl.BlockSpec((1, 1, D_MODEL), lambda i: (i, 0, 0)))
        out_shape.append(jax.ShapeDtypeStruct((n // tm, 1, D_MODEL), F32))
    return pl.pallas_call(
        functools.partial(_kvq_kernel, with_kmean, exact),
        grid=(n // tm,),
        in_specs=[pl.BlockSpec((tm, D_MODEL), row), pl.BlockSpec((tm, 2 * D_MODEL), row),
                  _full((1, D_MODEL)), _full((D_MODEL, 2 * D_MODEL)), _full((1, HEAD_DIM)),
                  pl.BlockSpec((tm, HEAD_DIM), tab), pl.BlockSpec((tm, HEAD_DIM), tab),
                  _full((1, D_MODEL)), _full((D_MODEL, D_MODEL)), _full((1, HEAD_DIM))],
        out_specs=out_specs,
        out_shape=out_shape,
        compiler_params=_params(1),
        name="kvq",
    )(x1, ypairs, w["kv_norm_g"], mw["w_kv"], w["k_norm_g"], cos_tab, sin_tab,
      w["b_norm_g"], mw["b_w_q"], w["q_norm_g"])


def _attn_kernel(q_ref, kb_ref, vb_ref, km_ref, x2_ref, wo_ref, mg_ref, wr_ref, br_ref,
                 x3_ref, hm_ref, route_ref, o_scr):
    j = pl.program_id(1)
    tq = q_ref.shape[0]
    n_blk = km_ref.shape[1]
    scale = HEAD_DIM ** -0.5
    q_pos = lax.broadcasted_iota(jnp.int32, (tq, MOBA_BLOCK), 0)
    k_pos = lax.broadcasted_iota(jnp.int32, (tq, MOBA_BLOCK), 1)
    blk = lax.broadcasted_iota(jnp.int32, (tq, n_blk), 1).astype(F32)
    cur = j.astype(F32)
    contract_last = (((1,), (1,)), ((), ()))
    own = pl.multiple_of(j * MOBA_BLOCK, MOBA_BLOCK)
    for h in range(N_HEADS):
        hs = slice(h * HEAD_DIM, (h + 1) * HEAD_DIM)
        qb = q_ref[:, hs].astype(BF16)
        gate = lax.dot_general(qb, km_ref[0, :, hs].astype(BF16), contract_last, preferred_element_type=F32)
        gate = jnp.where(blk < cur, gate, NEG_INF)
        sel = jnp.zeros((tq, n_blk), F32)
        for _ in range(MOBA_TOPK):
            m = jnp.max(gate, axis=-1, keepdims=True)
            first = jnp.min(jnp.where(gate == m, blk, float(n_blk)), axis=-1, keepdims=True)
            pick = (blk == first) & (m > NEG_INF)
            sel = jnp.where(pick, 1.0, sel)
            gate = jnp.where(pick, NEG_INF, gate)

        s = lax.dot_general(qb, kb_ref[pl.ds(own, MOBA_BLOCK), hs], contract_last, preferred_element_type=F32) * scale
        s = jnp.where(k_pos <= q_pos, s, NEG_INF)
        m0 = jnp.max(s, axis=-1, keepdims=True)
        p = jnp.exp(s - m0)
        l0 = jnp.sum(p, axis=-1, keepdims=True)
        acc0 = jnp.dot(p.astype(BF16), vb_ref[pl.ds(own, MOBA_BLOCK), hs], preferred_element_type=F32)

        def past_block(jj, carry, qb=qb, sel=sel, hs=hs):
            m, l, acc = carry
            off = pl.multiple_of(jj * MOBA_BLOCK, MOBA_BLOCK)
            s = lax.dot_general(qb, kb_ref[pl.ds(off, MOBA_BLOCK), hs], contract_last,
                                preferred_element_type=F32) * scale
            visible = jnp.max(jnp.where(blk == jj.astype(F32), sel, 0.0), axis=-1, keepdims=True)
            s = jnp.where(visible > 0.0, s, NEG_INF)
            m_new = jnp.maximum(m, jnp.max(s, axis=-1, keepdims=True))
            alpha = jnp.exp(m - m_new)
            p = jnp.exp(s - m_new)
            l = alpha * l + jnp.sum(p, axis=-1, keepdims=True)
            acc = alpha * acc + jnp.dot(p.astype(BF16), vb_ref[pl.ds(off, MOBA_BLOCK), hs],
                                        preferred_element_type=F32)
            return m_new, l, acc

        _, l, acc = lax.fori_loop(0, j, past_block, (m0, l0, acc0))
        o_scr[:, hs] = (acc / l).astype(BF16)

    x3 = x2_ref[...] + jnp.dot(o_scr[...], wo_ref[...], preferred_element_type=F32)
    x3_ref[...] = x3
    _moe_prenorm_and_route(x3, mg_ref, wr_ref, br_ref, hm_ref, route_ref, False)


def _attention_prompt(q, kb, vb, kmean, x2, w, seq):
    n = q.shape[0]
    bsz = n // seq
    n_qb = seq // MOBA_BLOCK
    row = lambda b, j: (b * n_qb + j, 0)
    return pl.pallas_call(
        _attn_kernel,
        grid=(bsz, n_qb),
        in_specs=[pl.BlockSpec((MOBA_BLOCK, D_MODEL), row),
                  pl.BlockSpec((seq, D_MODEL), lambda b, j: (b, 0)),
                  pl.BlockSpec((seq, D_MODEL), lambda b, j: (b, 0)),
                  pl.BlockSpec((1, n_qb, D_MODEL), lambda b, j: (b, 0, 0)),
                  pl.BlockSpec((MOBA_BLOCK, D_MODEL), row),
                  _full((D_MODEL, D_MODEL)), _full((1, D_MODEL)), _full((D_MODEL, LANES)), _full((1, LANES))],
        out_specs=[pl.BlockSpec((MOBA_BLOCK, D_MODEL), row), pl.BlockSpec((MOBA_BLOCK, D_MODEL), row),
                   pl.BlockSpec((MOBA_BLOCK, LANES), row)],
        out_shape=[jax.ShapeDtypeStruct((n, D_MODEL), F32), jax.ShapeDtypeStruct((n, D_MODEL), F32),
                   jax.ShapeDtypeStruct((n, LANES), F32)],
        scratch_shapes=[pltpu.VMEM((MOBA_BLOCK, D_MODEL), BF16)],
        compiler_params=_params(2),
        name="moba_prompt",
    )(q, kb, vb, kmean.reshape(bsz, n_qb, D_MODEL), x2, w["bf16"]["b_w_o"],
      w["moe_norm_g"][1], w["bf16"]["moe_w_route"][1], w["moe_b_route"][1])


def _paged_kmean_kernel(pt_ref, *refs):
    del pt_ref
    page_refs, out_ref = refs[:-1], refs[-1]
    pages_per_block = MOBA_BLOCK // page_refs[0].shape[1]
    for b in range(len(page_refs) // pages_per_block):
        tot = jnp.zeros((1, D_MODEL), F32)
        for p in range(pages_per_block):
            tot = tot + jnp.sum(page_refs[b * pages_per_block + p][0], axis=0, keepdims=True)
        out_ref[0, b:b + 1, :] = tot / MOBA_BLOCK


def _paged_kmean(cache_k3, page_table):
    n_seq, n_pages = page_table.shape
    page = cache_k3.shape[1]
    blocks_per_step = PAGES_PER_STEP * page // MOBA_BLOCK
    n_blk = n_pages * page // MOBA_BLOCK
    in_specs = [pl.BlockSpec((1, page, D_MODEL), lambda s, j, pt, i=i: (pt[s, j * PAGES_PER_STEP + i], 0, 0))
                for i in range(PAGES_PER_STEP)]
    grid_spec = pltpu.PrefetchScalarGridSpec(
        num_scalar_prefetch=1,
        grid=(n_seq, n_pages // PAGES_PER_STEP),
        in_specs=in_specs,
        out_specs=pl.BlockSpec((1, blocks_per_step, D_MODEL), lambda s, j, pt: (s, j, 0)),
    )
    return pl.pallas_call(
        _paged_kmean_kernel,
        grid_spec=grid_spec,
        out_shape=jax.ShapeDtypeStruct((n_seq, n_blk, D_MODEL), F32),
        compiler_params=_params(2),
        name="paged_kmean",
    )(page_table, *([cache_k3] * PAGES_PER_STEP))


def _block_select_kernel(q_ref, km_ref, sel_ref):
    prod = km_ref[0] * q_ref[0]
    n_blk = prod.shape[0]
    lane = lax.broadcasted_iota(jnp.int32, (n_blk, LANES), 1)
    gate = jnp.full((n_blk, LANES), NEG_INF, F32)
    for h in range(N_HEADS):
        gh = jnp.sum(prod[:, h * HEAD_DIM:(h + 1) * HEAD_DIM], axis=-1, keepdims=True)
        gate = jnp.where(lane == h, gh, gate)
    blk = lax.broadcasted_iota(jnp.int32, (n_blk, LANES), 0).astype(F32)
    out_row = lax.broadcasted_iota(jnp.int32, sel_ref.shape[1:], 0)
    out = jnp.zeros(sel_ref.shape[1:], F32)
    for r in range(MOBA_TOPK):
        m = jnp.max(gate, axis=0, keepdims=True)
        first = jnp.min(jnp.where(gate == m, blk, float(n_blk)), axis=0, keepdims=True)
        gate = jnp.where(blk == first, NEG_INF, gate)
        out = jnp.where(out_row == r, first, out)
    sel_ref[0] = out.astype(jnp.int32)


def _block_select(q3, kmean_s):
    n_seq, n_blk, _ = kmean_s.shape
    return pl.pallas_call(
        _block_select_kernel,
        grid=(n_seq,),
        in_specs=[pl.BlockSpec((1, 1, D_MODEL), lambda s: (s, 0, 0)),
                  pl.BlockSpec((1, n_blk, D_MODEL), lambda s: (s, 0, 0))],
        out_specs=pl.BlockSpec((1, 8, LANES), lambda s: (s, 0, 0)),
        out_shape=jax.ShapeDtypeStruct((n_seq, 8, LANES), jnp.int32),
        compiler_params=_params(1),
        name="block_select",
    )(q3, kmean_s)


def _attn_sample_kernel(n_sel_pages, pg_ref, q_ref, *refs):
    del pg_ref
    k_pages = refs[:n_sel_pages]
    v_pages = refs[n_sel_pages:2 * n_sel_pages]
    k_own_ref, v_own_ref, o_ref = refs[2 * n_sel_pages:]
    scale = HEAD_DIM ** -0.5
    q = q_ref[0]
    s_own = jnp.sum(k_own_ref[0] * q, axis=-1, keepdims=True) * scale
    scores = [jnp.sum(kp[0] * q, axis=-1, keepdims=True) * scale for kp in k_pages]
    m = s_own
    for s in scores:
        m = jnp.maximum(m, jnp.max(s, axis=0, keepdims=True))
    p_own = jnp.exp(s_own - m)
    l = p_own
    acc = p_own * v_own_ref[0]
    for s, vp in zip(scores, v_pages):
        p = jnp.exp(s - m)
        l = l + jnp.sum(p, axis=0, keepdims=True)
        acc = acc + jnp.sum(p * vp[0], axis=0, keepdims=True)
    o_ref[0] = acc / l


def _attention_sample(q3, k3, v3, cache_k3, cache_v3, sel_pages):
    n_seq = q3.shape[0]
    page = cache_k3.shape[1]
    n_sel_pages = sel_pages.shape[-1]
    head = lambda s, h, pg: (s, 0, h)
    paged = [pl.BlockSpec((1, page, HEAD_DIM),
                          lambda s, h, pg, i=i: (pg[(s * N_HEADS + h) * n_sel_pages + i], 0, h))
             for i in range(n_sel_pages)]
    grid_spec = pltpu.PrefetchScalarGridSpec(
        num_scalar_prefetch=1,
        grid=(n_seq, N_HEADS),
        in_specs=[pl.BlockSpec((1, 1, HEAD_DIM), head)] + paged + paged
                 + [pl.BlockSpec((1, 1, HEAD_DIM), head), pl.BlockSpec((1, 1, HEAD_DIM), head)],
        out_specs=pl.BlockSpec((1, 1, HEAD_DIM), head),
    )
    return pl.pallas_call(
        functools.partial(_attn_sample_kernel, n_sel_pages),
        grid_spec=grid_spec,
        out_shape=jax.ShapeDtypeStruct((n_seq, 1, D_MODEL), F32),
        compiler_params=_params(2),
        name="moba_sample",
    )(sel_pages.reshape(-1), q3, *([cache_k3] * n_sel_pages), *([cache_v3] * n_sel_pages), k3, v3)


def _oproj_kernel(o_ref, x2_ref, wo_ref, mg_ref, wr_ref, br_ref, x3_ref, hm_ref, route_ref):
    x3 = x2_ref[...] + _mm(o_ref[...], wo_ref[...], True)
    x3_ref[...] = x3
    _moe_prenorm_and_route(x3, mg_ref, wr_ref, br_ref, hm_ref, route_ref, True)


def _oproj(o, x2, w):
    n = o.shape[0]
    return pl.pallas_call(
        _oproj_kernel,
        grid=(1,),
        in_specs=[_full((n, D_MODEL)), _full((n, D_MODEL)), _full((D_MODEL, D_MODEL)), _full((1, D_MODEL)),
                  _full((D_MODEL, LANES)), _full((1, LANES))],
        out_specs=[_full((n, D_MODEL)), _full((n, D_MODEL)), _full((n, LANES))],
        out_shape=[jax.ShapeDtypeStruct((n, D_MODEL), F32), jax.ShapeDtypeStruct((n, D_MODEL), F32),
                   jax.ShapeDtypeStruct((n, LANES), F32)],
        compiler_params=_params(1),
        name="oproj_sample",
    )(o, x2, w["f32"]["b_w_o"], w["moe_norm_g"][1], w["f32"]["moe_w_route"][1], w["moe_b_route"][1])


def _combine_kernel(x_ref, yp_ref, o_ref):
    o_ref[...] = x_ref[...] + (yp_ref[:, :D_MODEL] + yp_ref[:, D_MODEL:])


def _combine(x, ypairs):
    n = x.shape[0]
    tm = min(n, 2 * ROW_TILE)
    row = lambda i: (i, 0)
    return pl.pallas_call(
        _combine_kernel,
        grid=(n // tm,),
        in_specs=[pl.BlockSpec((tm, D_MODEL), row), pl.BlockSpec((tm, 2 * D_MODEL), row)],
        out_specs=pl.BlockSpec((tm, D_MODEL), row),
        out_shape=jax.ShapeDtypeStruct((n, D_MODEL), F32),
        compiler_params=_params(1),
        name="combine",
    )(x, ypairs)


def _rope_tables(pos):
    inv_freq = ROPE_THETA ** (-jnp.arange(ROT_HALF, dtype=F32) / ROT_HALF)
    ang = pos.astype(F32)[:, None] * inv_freq[None, :]
    cos, sin = jnp.cos(ang), jnp.sin(ang)
    rest = HEAD_DIM - ROT_DIM
    cos_tab = jnp.concatenate([cos, cos, jnp.ones((pos.shape[0], rest), F32)], axis=-1)
    sin_tab = jnp.concatenate([-sin, sin, jnp.zeros((pos.shape[0], rest), F32)], axis=-1)
    return cos_tab, sin_tab


def kernel(x_prompt, x_sample, cache_k, cache_v, page_table, a_norm_g, a_w_in, a_ln_g, a_ln_b, a_w_s, a_b_s, a_w_out, kv_norm_g, w_kv, k_norm_g, b_norm_g, b_w_q, q_norm_g, b_w_o, moe_norm_g, moe_w_rg, moe_b_rg, moe_w_re, moe_b_re, moe_w_gate, moe_w_up, moe_w_down):
    bsz, seq, _ = x_prompt.shape
    n_dec = x_sample.shape[0]
    assert x_sample.shape[1] == 1 and a_w_in.shape[0] == 1 and b_w_q.shape[0] == 1
    assert seq % ROW_TILE == 0 and ROW_TILE == MOBA_BLOCK
    n_pool, page = cache_k.shape[:2]
    past = page_table.shape[1] * page
    assert past % MOBA_BLOCK == 0 and past // MOBA_BLOCK >= MOBA_TOPK and MOBA_BLOCK % page == 0

    depth = moe_norm_g.shape[0]
    pad = LANES - N_EXPERT_GROUPS - N_EXPERTS
    mats = {
        "a_w_in": a_w_in[0], "a_w_out": a_w_out[0], "w_kv": w_kv, "b_w_q": b_w_q[0], "b_w_o": b_w_o[0],
        "moe_w_route": jnp.concatenate([moe_w_rg, moe_w_re, jnp.zeros((depth, D_MODEL, pad), F32)], axis=-1),
        "moe_w_gate": moe_w_gate, "moe_w_up": moe_w_up, "moe_w_down": moe_w_down,
    }
    w = {
        "f32": mats, "bf16": {name: m.astype(BF16) for name, m in mats.items()},
        "a_norm_g": a_norm_g[0][None], "a_ln_g": a_ln_g[0][None], "a_ln_b": a_ln_b[0][None],
        "a_w_s": a_w_s[0], "a_b_s_t": a_b_s[0].T,
        "a_ws0": jnp.repeat(a_w_s[0, :, 0, 0], GROUP_WIDTH)[None], "a_bs0": jnp.repeat(a_b_s[0, :, 0], GROUP_WIDTH)[None],
        "kv_norm_g": kv_norm_g[None], "k_norm_g": k_norm_g[None],
        "b_norm_g": b_norm_g[0][None], "q_norm_g": q_norm_g[0][None],
        "moe_norm_g": moe_norm_g[:, None, :],
        "moe_b_route": jnp.concatenate([moe_b_rg, moe_b_re, jnp.zeros((depth, pad), F32)], axis=-1)[:, None, :],
    }

    xp = x_prompt.reshape(bsz * seq, D_MODEL)
    xs = x_sample.reshape(n_dec, D_MODEL)
    cos_p, sin_p = _rope_tables(jnp.arange(seq, dtype=jnp.int32))
    cos_s, sin_s = _rope_tables(jnp.full((n_dec,), past, dtype=jnp.int32))

    x1p, vrows_p, hmp, routep = _gmlp(xp, w, False, seq)
    x1s, vrows_s, hms, routes = _gmlp(xs, w, True, seq)
    yp = _experts(hmp, routep, w, 0, EXPERT_TILE, False)
    ys = _experts(hms, routes, w, 0, EXPERT_TILE_SMALL, True)

    x2p, k_p, v_p, kb_p, vb_p, q_p, kmean_p = _kvq(x1p, yp, cos_p, sin_p, w, True, False)
    x2s, k_s, v_s, _, _, q_s = _kvq(x1s, ys, cos_s, sin_s, w, False, True)

    x3p, hmp, routep = _attention_prompt(q_p, kb_p, vb_p, kmean_p, x2p, w, seq)
    cache_k3 = cache_k.reshape(n_pool, page, D_MODEL)
    cache_v3 = cache_v.reshape(n_pool, page, D_MODEL)
    q3 = q_s.reshape(n_dec, 1, D_MODEL)
    kmean_s = _paged_kmean(cache_k3, page_table)
    sel = _block_select(q3, kmean_s)[:, :MOBA_TOPK, :N_HEADS]
    pages_per_block = MOBA_BLOCK // page
    sel_blocks = jnp.transpose(sel, (0, 2, 1))
    page_slots = sel_blocks[..., None] * pages_per_block + jnp.arange(pages_per_block, dtype=jnp.int32)
    page_slots = page_slots.reshape(n_dec, N_HEADS * MOBA_TOPK * pages_per_block)
    sel_pages = jnp.take_along_axis(page_table, page_slots, axis=1).reshape(n_dec, N_HEADS, MOBA_TOPK * pages_per_block)
    o_s = _attention_sample(q3, k_s.reshape(n_dec, 1, D_MODEL), v_s.reshape(n_dec, 1, D_MODEL),
                            cache_k3, cache_v3, sel_pages)
    x3s, hms, routes = _oproj(o_s.reshape(n_dec, D_MODEL), x2s, w)
    yp = _experts(hmp, routep, w, 1, EXPERT_TILE, False)
    ys = _experts(hms, routes, w, 1, EXPERT_TILE_SMALL, True)
    y_prompt = _combine(x3p, yp)
    y_sample = _combine(x3s, ys)

    return (y_prompt.reshape(bsz, seq, D_MODEL), y_sample.reshape(n_dec, 1, D_MODEL),
            vrows_p.reshape(1, bsz, CHUNK, GATE_WIDTH), vrows_s.reshape(1, n_dec, 1, GATE_WIDTH),
            k_p.reshape(bsz, seq, N_HEADS, HEAD_DIM), v_p.reshape(bsz, seq, N_HEADS, HEAD_DIM),
            k_s.reshape(n_dec, 1, N_HEADS, HEAD_DIM), v_s.reshape(n_dec, 1, N_HEADS, HEAD_DIM))
```

```python
import functools

import jax
import jax.numpy as jnp
from jax import lax
from jax.experimental import pallas as pl
from jax.experimental.pallas import tpu as pltpu

D_MODEL = 1024
CHUNK = 128
GATE_WIDTH = 2 * D_MODEL
N_GATE_GROUPS = 8
GROUP_WIDTH = GATE_WIDTH // N_GATE_GROUPS
HEAD_DIM = 128
N_HEADS = D_MODEL // HEAD_DIM
ROT_DIM = HEAD_DIM // 4
ROT_HALF = ROT_DIM // 2
ROPE_THETA = 500000.0
MOBA_BLOCK = 256
MOBA_TOPK = 3
N_EXPERT_GROUPS = 4
EXPERTS_PER_GROUP = 4
N_EXPERTS = N_EXPERT_GROUPS * EXPERTS_PER_GROUP
EXPERT_TOPK = 2
EXPERT_FF = D_MODEL // 2
EPS = 1e-6

LANES = 128
SUBLANES = 8
ROW_PIECES = D_MODEL // LANES
ROW_TILE = 256
EXPERT_TILE = 256
EXPERT_TILE_SMALL = 16
IN_PROJ_COLS = 512
PAGES_PER_STEP = 16
VMEM_LIMIT = 60 * 1024 * 1024

F32 = jnp.float32
BF16 = jnp.bfloat16
NEG_INF = float("-inf")

assert ROW_PIECES == SUBLANES


def _params(n_axes):
    return pltpu.CompilerParams(dimension_semantics=("arbitrary",) * n_axes, vmem_limit_bytes=VMEM_LIMIT)


def _rms(x, g):
    return x * lax.rsqrt(jnp.mean(x * x, axis=-1, keepdims=True) + EPS) * g


def _mm(a, b, exact):
    if exact:
        return jnp.dot(a, b, precision=lax.Precision.HIGHEST, preferred_element_type=F32)
    return jnp.dot(a.astype(BF16), b, preferred_element_type=F32)


def _full(shape):
    nd = len(shape)
    return pl.BlockSpec(shape, lambda *_: (0,) * nd, pipeline_mode=pl.Buffered(1))


def _store_tile_rows(ref, x):
    for s in range(ROW_PIECES):
        ref[pl.ds(s, x.shape[0], stride=ROW_PIECES), :] = x[:, s * LANES:(s + 1) * LANES]


def _load_tile_rows(ref, rows, first=0, step=ROW_PIECES):
    return jnp.concatenate([ref[pl.ds(first + s, rows, stride=step), :] for s in range(ROW_PIECES)], axis=-1)


def _route(hm, wr_ref, br_ref, exact):
    logits = _mm(hm, wr_ref[...], exact) + br_ref[...]
    lane = lax.broadcasted_iota(jnp.int32, logits.shape, 1).astype(F32)
    lg = jnp.where(lane < N_EXPERT_GROUPS, logits, NEG_INF)
    mg = jnp.max(lg, axis=-1, keepdims=True)
    g_sel = jnp.min(jnp.where(lg == mg, lane, float(LANES)), axis=-1, keepdims=True)
    p_grp = 1.0 / jnp.sum(jnp.exp(lg - mg), axis=-1, keepdims=True)
    lo = N_EXPERT_GROUPS + g_sel * EXPERTS_PER_GROUP
    le = jnp.where((lane >= lo) & (lane < lo + EXPERTS_PER_GROUP), logits, NEG_INF)
    v1 = jnp.max(le, axis=-1, keepdims=True)
    i1 = jnp.min(jnp.where(le == v1, lane, float(LANES)), axis=-1, keepdims=True)
    le2 = jnp.where(lane == i1, NEG_INF, le)
    v2 = jnp.max(le2, axis=-1, keepdims=True)
    i2 = jnp.min(jnp.where(le2 == v2, lane, float(LANES)), axis=-1, keepdims=True)
    t = jnp.exp(v2 - v1)
    w1 = p_grp / (1.0 + t)
    w2 = p_grp * t / (1.0 + t)
    out = jnp.where(lane == 0.0, i1 - N_EXPERT_GROUPS, 0.0)
    out = jnp.where(lane == 1.0, i2 - N_EXPERT_GROUPS, out)
    out = jnp.where(lane == 2.0, w1, out)
    out = jnp.where(lane == 3.0, w2, out)
    return out


def _moe_prenorm_and_route(x, mg_ref, wr_ref, br_ref, hm_ref, route_ref, exact):
    hm = _rms(x, mg_ref[...])
    _store_tile_rows(hm_ref, hm)
    route_ref[...] = _route(hm, wr_ref, br_ref, exact)


def _moe_residual(x, y_ref, route_ref):
    rows = x.shape[0]
    step = EXPERT_TOPK * ROW_PIECES
    y1 = _load_tile_rows(y_ref, rows, 0, step)
    y2 = _load_tile_rows(y_ref, rows, ROW_PIECES, step)
    return x + (route_ref[:, 2:3] * y1 + route_ref[:, 3:4] * y2)


def _gmlp_kernel(sample, tm, x_ref, ng_ref, win_ref, lng_ref, lnb_ref, ws_ref, bs_ref, wout_ref,
                 mg_ref, wr_ref, br_ref, x1_ref, vrows_ref, hm_ref, route_ref, u_scr, v_scr, gated_scr):
    x = x_ref[...]
    h = _rms(x, ng_ref[...])
    if not sample:
        h = h.astype(BF16)
    n_col = 2 * GATE_WIDTH // IN_PROJ_COLS
    n_ucol = GATE_WIDTH // IN_PROJ_COLS
    vsum = jnp.zeros((tm, 1), F32)
    for c in range(n_col):
        z = _mm(h, win_ref[:, c * IN_PROJ_COLS:(c + 1) * IN_PROJ_COLS], sample)
        z = jax.nn.gelu(z, approximate=True)
        if c < n_ucol:
            u_scr[:, c * IN_PROJ_COLS:(c + 1) * IN_PROJ_COLS] = z
        else:
            v_scr[:, (c - n_ucol) * IN_PROJ_COLS:(c - n_ucol + 1) * IN_PROJ_COLS] = z
            vsum = vsum + jnp.sum(z, axis=-1, keepdims=True)
    mean = vsum / GATE_WIDTH
    vss = jnp.zeros((tm, 1), F32)
    for c in range(n_ucol):
        xc = v_scr[:, c * IN_PROJ_COLS:(c + 1) * IN_PROJ_COLS] - mean
        vss = vss + jnp.sum(xc * xc, axis=-1, keepdims=True)
    rstd = lax.rsqrt(vss / GATE_WIDTH + EPS)
    for c in range(n_ucol):
        cs = slice(c * IN_PROJ_COLS, (c + 1) * IN_PROJ_COLS)
        v_scr[:, cs] = (v_scr[:, cs] - mean) * rstd * lng_ref[:, cs] + lnb_ref[:, cs]

    if sample:
        vrows_ref[...] = v_scr[...]
        gated_scr[...] = u_scr[...] * (v_scr[...] * ws_ref[...] + bs_ref[...])
    else:
        vrows_ref[0] = v_scr[tm - CHUNK:tm, :]
        t_out = lax.broadcasted_iota(jnp.int32, (CHUNK, CHUNK), 0)
        s_in = lax.broadcasted_iota(jnp.int32, (CHUNK, CHUNK), 1)
        for g in range(N_GATE_GROUPS):
            gs = slice(g * GROUP_WIDTH, (g + 1) * GROUP_WIDTH)
            wsg = jnp.where(s_in <= t_out, ws_ref[g], 0.0).astype(BF16)
            bias = bs_ref[:, g:g + 1]
            for c in range(tm // CHUNK):
                rs = slice(c * CHUNK, (c + 1) * CHUNK)
                mixed = jnp.dot(wsg, v_scr[rs, gs].astype(BF16), preferred_element_type=F32) + bias
                gated_scr[rs, gs] = (u_scr[rs, gs] * mixed).astype(BF16)

    x1 = x + _mm(gated_scr[...], wout_ref[...], sample)
    x1_ref[...] = x1
    _moe_prenorm_and_route(x1, mg_ref, wr_ref, br_ref, hm_ref, route_ref, sample)


def _gmlp(x, w, sample, seq):
    n = x.shape[0]
    mw = w["f32"] if sample else w["bf16"]
    tm = n if sample else ROW_TILE
    row = lambda i: (i, 0)
    if sample:
        ws_spec, bs_spec = _full((1, GATE_WIDTH)), _full((1, GATE_WIDTH))
        vrows_shape = jax.ShapeDtypeStruct((n, GATE_WIDTH), F32)
        vrows_spec = pl.BlockSpec((tm, GATE_WIDTH), row)
    else:
        ws_spec, bs_spec = _full((N_GATE_GROUPS, CHUNK, CHUNK)), _full((CHUNK, N_GATE_GROUPS))
        tiles_per_seq = seq // tm
        vrows_shape = jax.ShapeDtypeStruct((n // seq, CHUNK, GATE_WIDTH), F32)
        vrows_spec = pl.BlockSpec((1, CHUNK, GATE_WIDTH), lambda i: (i // tiles_per_seq, 0, 0))
    return pl.pallas_call(
        functools.partial(_gmlp_kernel, sample, tm),
        grid=(n // tm,),
        in_specs=[pl.BlockSpec((tm, D_MODEL), row), _full((1, D_MODEL)), _full((D_MODEL, 2 * GATE_WIDTH)),
                  _full((1, GATE_WIDTH)), _full((1, GATE_WIDTH)), ws_spec, bs_spec, _full((GATE_WIDTH, D_MODEL)),
                  _full((1, D_MODEL)), _full((D_MODEL, LANES)), _full((1, LANES))],
        out_specs=[pl.BlockSpec((tm, D_MODEL), row), vrows_spec, pl.BlockSpec((tm * ROW_PIECES, LANES), row),
                   pl.BlockSpec((tm, LANES), row)],
        out_shape=[jax.ShapeDtypeStruct((n, D_MODEL), F32), vrows_shape,
                   jax.ShapeDtypeStruct((n * ROW_PIECES, LANES), F32), jax.ShapeDtypeStruct((n, LANES), F32)],
        scratch_shapes=[pltpu.VMEM((tm, GATE_WIDTH), F32), pltpu.VMEM((tm, GATE_WIDTH), F32),
                        pltpu.VMEM((tm, GATE_WIDTH), F32 if sample else BF16)],
        compiler_params=_params(1),
        name="gmlp_sample" if sample else "gmlp_prompt",
    )(x, w["a_norm_g"], mw["a_w_in"], w["a_ln_g"], w["a_ln_b"],
      w["a_ws0"] if sample else w["a_w_s"], w["a_bs0"] if sample else w["a_b_s_t"], mw["a_w_out"],
      w["moe_norm_g"][0], mw["moe_w_route"][0], w["moe_b_route"][0])


def _expert_kernel(tmg, n_tiles, exact, te_ref, nv_ref, ts_ref, od_ref, h_hbm, wg_ref, wu_ref, wd_ref, out_hbm,
                   xbuf, ybuf, sem_in, sem_out):
    del te_ref
    i = pl.program_id(0)
    rows8 = SUBLANES

    def gather_row(r, slot, pair=0):
        tok8 = pl.multiple_of((pair >> 1) * rows8, rows8)
        return pltpu.make_async_copy(h_hbm.at[pl.ds(tok8, rows8)], xbuf.at[slot, pl.ds(r * rows8, rows8)],
                                     sem_in.at[slot])

    def scatter_row(r, slot, pair=0):
        pair8 = pl.multiple_of(pair * rows8, rows8)
        return pltpu.make_async_copy(ybuf.at[slot, pl.ds(pl.multiple_of(r * rows8, rows8), rows8)],
                                     out_hbm.at[pl.ds(pair8, rows8)], sem_out.at[slot])

    def for_scatter_rows(tile, slot, start):
        base, n_rows = ts_ref[tile], nv_ref[tile]

        def one(r):
            if start:
                scatter_row(r, slot, od_ref[base + r]).start()
            else:
                scatter_row(r, slot).wait()

        @pl.when(n_rows == tmg)
        def _():
            for r in range(tmg):
                one(r)

        @pl.when(n_rows < tmg)
        def _():
            def body(r, c):
                one(r)
                return c
            lax.fori_loop(0, n_rows, body, 0)

    nxt = jnp.minimum(i, n_tiles - 1)

    @pl.when((i < n_tiles) & (nv_ref[nxt] > 0))
    def _():
        base = ts_ref[nxt]
        for r in range(tmg):
            gather_row(r, nxt % 2, od_ref[base + r]).start()

    c = jnp.maximum(i - 1, 0)

    @pl.when((i >= 1) & (nv_ref[c] > 0))
    def _():
        slot = c % 2
        for r in range(tmg):
            gather_row(r, slot).wait()
        x = _load_tile_rows(xbuf.at[slot], tmg)
        a = jax.nn.silu(_mm(x, wg_ref[0], exact)) * _mm(x, wu_ref[0], exact)
        y = _mm(a, wd_ref[0], exact)

        @pl.when(c >= 2)
        def _():
            for_scatter_rows(c - 2, slot, False)

        _store_tile_rows(ybuf.at[slot], y)
        for_scatter_rows(c, slot, True)

        @pl.when((c == n_tiles - 1) | (nv_ref[jnp.minimum(c + 1, n_tiles - 1)] == 0))
        def _():
            @pl.when(c >= 1)
            def _():
                for_scatter_rows(c - 1, 1 - slot, False)
            for_scatter_rows(c, slot, False)


def _dispatch(route, tmg):
    n_pairs = EXPERT_TOPK * route.shape[0]
    e_flat = route[:, 0:EXPERT_TOPK].astype(jnp.int32).reshape(n_pairs)
    n_tiles = -(-(n_pairs + N_EXPERTS * (tmg - 1)) // tmg)
    order = jnp.argsort(e_flat, stable=True).astype(jnp.int32)
    counts = jnp.sum((e_flat[:, None] == jnp.arange(N_EXPERTS, dtype=jnp.int32)[None, :]).astype(jnp.int32), axis=0)
    tiles_per = -(-counts // tmg)
    t_end = jnp.cumsum(tiles_per)
    t_start = t_end - tiles_per
    c_start = jnp.cumsum(counts) - counts
    tile = jnp.arange(n_tiles, dtype=jnp.int32)
    tile_expert = jnp.minimum(jnp.sum((t_end[None, :] <= tile[:, None]).astype(jnp.int32), axis=1), N_EXPERTS - 1)
    first_row = (tile - t_start[tile_expert]) * tmg
    tile_valid = jnp.clip(counts[tile_expert] - first_row, 0, tmg).astype(jnp.int32)
    tile_src = jnp.where(tile_valid > 0, c_start[tile_expert] + first_row, 0).astype(jnp.int32)
    order = jnp.concatenate([order, jnp.zeros((tmg,), jnp.int32)])
    return tile_expert.astype(jnp.int32), tile_valid, tile_src, order


def _experts(hm, route, w, layer, tmg, exact):
    n = route.shape[0]
    mw = w["f32"] if exact else w["bf16"]
    w_gate, w_up, w_down = mw["moe_w_gate"][layer], mw["moe_w_up"][layer], mw["moe_w_down"][layer]
    tile_expert, tile_valid, tile_src, order = _dispatch(route, tmg)
    n_tiles = tile_expert.shape[0]
    by_expert = lambda i, te, nv, ts, od: (te[jnp.maximum(i - 1, 0)], 0, 0)
    grid_spec = pltpu.PrefetchScalarGridSpec(
        num_scalar_prefetch=4,
        grid=(n_tiles + 1,),
        in_specs=[pl.BlockSpec(memory_space=pl.ANY),
                  pl.BlockSpec((1, D_MODEL, EXPERT_FF), by_expert),
                  pl.BlockSpec((1, D_MODEL, EXPERT_FF), by_expert),
                  pl.BlockSpec((1, EXPERT_FF, D_MODEL), by_expert)],
        out_specs=pl.BlockSpec(memory_space=pl.ANY),
        scratch_shapes=[pltpu.VMEM((2, tmg * ROW_PIECES, LANES), F32), pltpu.VMEM((2, tmg * ROW_PIECES, LANES), F32),
                        pltpu.SemaphoreType.DMA((2,)), pltpu.SemaphoreType.DMA((2,))],
    )
    return pl.pallas_call(
        functools.partial(_expert_kernel, tmg, n_tiles, exact),
        grid_spec=grid_spec,
        out_shape=jax.ShapeDtypeStruct((EXPERT_TOPK * n * ROW_PIECES, LANES), F32),
        compiler_params=_params(1),
        name="experts",
    )(tile_expert, tile_valid, tile_src, order, hm, w_gate, w_up, w_down)


def _rope(xh, cos, sin, lane):
    swapped = jnp.where(lane < ROT_HALF, pltpu.roll(xh, LANES - ROT_HALF, 1), pltpu.roll(xh, ROT_HALF, 1))
    return xh * cos + swapped * sin


def _kvq_kernel(with_kmean, exact, x1_ref, yp_ref, route_ref, kvg_ref, wkv_ref, kg_ref, cos_ref, sin_ref,
                bng_ref, wq_ref, qg_ref, x2_ref, k_ref, v_ref, kb_ref, vb_ref, q_ref, *maybe_km_ref):
    x2 = _moe_residual(x1_ref[...], yp_ref, route_ref)
    x2_ref[...] = x2
    cos = cos_ref[...]
    sin = sin_ref[...]
    lane = lax.broadcasted_iota(jnp.int32, cos.shape, 1)
    kv = _mm(_rms(x2, kvg_ref[...]), wkv_ref[...], exact)
    for h in range(N_HEADS):
        hs = slice(h * HEAD_DIM, (h + 1) * HEAD_DIM)
        kh = _rope(_rms(kv[:, hs], kg_ref[...]), cos, sin, lane)
        k_ref[:, hs] = kh
        kb_ref[:, hs] = kh.astype(BF16)
        if with_kmean:
            maybe_km_ref[0][0, :, hs] = jnp.mean(kh, axis=0, keepdims=True)
    v = kv[:, D_MODEL:]
    v_ref[...] = v
    vb_ref[...] = v.astype(BF16)
    q = _mm(_rms(x2, bng_ref[...]), wq_ref[...], exact)
    for h in range(N_HEADS):
        hs = slice(h * HEAD_DIM, (h + 1) * HEAD_DIM)
        q_ref[:, hs] = _rope(_rms(q[:, hs], qg_ref[...]), cos, sin, lane)


def _kvq(x1, ypairs, route, cos_tab, sin_tab, w, with_kmean, exact):
    n = x1.shape[0]
    mw = w["f32"] if exact else w["bf16"]
    tm = ROW_TILE if with_kmean else n
    row = lambda i: (i, 0)
    tab_tiles = cos_tab.shape[0] // tm
    tab = lambda i: (i % tab_tiles, 0)
    out_specs = [pl.BlockSpec((tm, D_MODEL), row)] * 6
    out_shape = [jax.ShapeDtypeStruct((n, D_MODEL), F32)] * 3 + [jax.ShapeDtypeStruct((n, D_MODEL), BF16)] * 2 \
        + [jax.ShapeDtypeStruct((n, D_MODEL), F32)]
    if with_kmean:
        out_specs.append(pl.BlockSpec((1, 1, D_MODEL), lambda i: (i, 0, 0)))
        out_shape.append(jax.ShapeDtypeStruct((n // tm, 1, D_MODEL), F32))
    return pl.pallas_call(
        functools.partial(_kvq_kernel, with_kmean, exact),
        grid=(n // tm,),
        in_specs=[pl.BlockSpec((tm, D_MODEL), row), pl.BlockSpec((EXPERT_TOPK * tm * ROW_PIECES, LANES), row),
                  pl.BlockSpec((tm, LANES), row),
                  _full((1, D_MODEL)), _full((D_MODEL, 2 * D_MODEL)), _full((1, HEAD_DIM)),
                  pl.BlockSpec((tm, HEAD_DIM), tab), pl.BlockSpec((tm, HEAD_DIM), tab),
                  _full((1, D_MODEL)), _full((D_MODEL, D_MODEL)), _full((1, HEAD_DIM))],
        out_specs=out_specs,
        out_shape=out_shape,
        compiler_params=_params(1),
        name="kvq",
    )(x1, ypairs, route, w["kv_norm_g"], mw["w_kv"], w["k_norm_g"], cos_tab, sin_tab,
      w["b_norm_g"], mw["b_w_q"], w["q_norm_g"])


def _attn_kernel(q_ref, kb_ref, vb_ref, km_ref, x2_ref, wo_ref, mg_ref, wr_ref, br_ref,
                 x3_ref, hm_ref, route_ref, o_scr):
    j = pl.program_id(1)
    tq = q_ref.shape[0]
    n_blk = km_ref.shape[1]
    scale = HEAD_DIM ** -0.5
    q_pos = lax.broadcasted_iota(jnp.int32, (tq, MOBA_BLOCK), 0)
    k_pos = lax.broadcasted_iota(jnp.int32, (tq, MOBA_BLOCK), 1)
    blk = lax.broadcasted_iota(jnp.int32, (tq, n_blk), 1).astype(F32)
    cur = j.astype(F32)
    contract_last = (((1,), (1,)), ((), ()))
    own = pl.multiple_of(j * MOBA_BLOCK, MOBA_BLOCK)
    for h in range(N_HEADS):
        hs = slice(h * HEAD_DIM, (h + 1) * HEAD_DIM)
        qb = q_ref[:, hs].astype(BF16)
        gate = lax.dot_general(qb, km_ref[0, :, hs].astype(BF16), contract_last, preferred_element_type=F32)
        gate = jnp.where(blk < cur, gate, NEG_INF)
        sel = jnp.zeros((tq, n_blk), F32)
        for _ in range(MOBA_TOPK):
            m = jnp.max(gate, axis=-1, keepdims=True)
            first = jnp.min(jnp.where(gate == m, blk, float(n_blk)), axis=-1, keepdims=True)
            pick = (blk == first) & (m > NEG_INF)
            sel = jnp.where(pick, 1.0, sel)
            gate = jnp.where(pick, NEG_INF, gate)

        s = lax.dot_general(qb, kb_ref[pl.ds(own, MOBA_BLOCK), hs], contract_last, preferred_element_type=F32) * scale
        s = jnp.where(k_pos <= q_pos, s, NEG_INF)
        m0 = jnp.max(s, axis=-1, keepdims=True)
        p = jnp.exp(s - m0)
        l0 = jnp.sum(p, axis=-1, keepdims=True)
        acc0 = jnp.dot(p.astype(BF16), vb_ref[pl.ds(own, MOBA_BLOCK), hs], preferred_element_type=F32)

        def past_block(jj, carry, qb=qb, sel=sel, hs=hs):
            m, l, acc = carry
            off = pl.multiple_of(jj * MOBA_BLOCK, MOBA_BLOCK)
            s = lax.dot_general(qb, kb_ref[pl.ds(off, MOBA_BLOCK), hs], contract_last,
                                preferred_element_type=F32) * scale
            visible = jnp.max(jnp.where(blk == jj.astype(F32), sel, 0.0), axis=-1, keepdims=True)
            s = jnp.where(visible > 0.0, s, NEG_INF)
            m_new = jnp.maximum(m, jnp.max(s, axis=-1, keepdims=True))
            alpha = jnp.exp(m - m_new)
            p = jnp.exp(s - m_new)
            l = alpha * l + jnp.sum(p, axis=-1, keepdims=True)
            acc = alpha * acc + jnp.dot(p.astype(BF16), vb_ref[pl.ds(off, MOBA_BLOCK), hs],
                                        preferred_element_type=F32)
            return m_new, l, acc

        _, l, acc = lax.fori_loop(0, j, past_block, (m0, l0, acc0))
        o_scr[:, hs] = (acc / l).astype(BF16)

    x3 = x2_ref[...] + jnp.dot(o_scr[...], wo_ref[...], preferred_element_type=F32)
    x3_ref[...] = x3
    _moe_prenorm_and_route(x3, mg_ref, wr_ref, br_ref, hm_ref, route_ref, False)


def _attention_prompt(q, kb, vb, kmean, x2, w, seq):
    n = q.shape[0]
    bsz = n // seq
    n_qb = seq // MOBA_BLOCK
    row = lambda b, j: (b * n_qb + j, 0)
    return pl.pallas_call(
        _attn_kernel,
        grid=(bsz, n_qb),
        in_specs=[pl.BlockSpec((MOBA_BLOCK, D_MODEL), row),
                  pl.BlockSpec((seq, D_MODEL), lambda b, j: (b, 0)),
                  pl.BlockSpec((seq, D_MODEL), lambda b, j: (b, 0)),
                  pl.BlockSpec((1, n_qb, D_MODEL), lambda b, j: (b, 0, 0)),
                  pl.BlockSpec((MOBA_BLOCK, D_MODEL), row),
                  _full((D_MODEL, D_MODEL)), _full((1, D_MODEL)), _full((D_MODEL, LANES)), _full((1, LANES))],
        out_specs=[pl.BlockSpec((MOBA_BLOCK, D_MODEL), row), pl.BlockSpec((MOBA_BLOCK * ROW_PIECES, LANES), row),
                   pl.BlockSpec((MOBA_BLOCK, LANES), row)],
        out_shape=[jax.ShapeDtypeStruct((n, D_MODEL), F32), jax.ShapeDtypeStruct((n * ROW_PIECES, LANES), F32),
                   jax.ShapeDtypeStruct((n, LANES), F32)],
        scratch_shapes=[pltpu.VMEM((MOBA_BLOCK, D_MODEL), BF16)],
        compiler_params=_params(2),
        name="moba_prompt",
    )(q, kb, vb, kmean.reshape(bsz, n_qb, D_MODEL), x2, w["bf16"]["b_w_o"],
      w["moe_norm_g"][1], w["bf16"]["moe_w_route"][1], w["moe_b_route"][1])


def _paged_kmean_kernel(pt_ref, *refs):
    del pt_ref
    page_refs, out_ref = refs[:-1], refs[-1]
    pages_per_block = MOBA_BLOCK // page_refs[0].shape[1]
    for b in range(len(page_refs) // pages_per_block):
        tot = jnp.sum(page_refs[b * pages_per_block][0], axis=0)
        for p in range(1, pages_per_block):
            tot = tot + jnp.sum(page_refs[b * pages_per_block + p][0], axis=0)
        out_ref[0, b] = tot / MOBA_BLOCK


def _paged_kmean(cache_k, page_table):
    n_seq, n_pages = page_table.shape
    page = cache_k.shape[1]
    blocks_per_step = PAGES_PER_STEP * page // MOBA_BLOCK
    n_blk = n_pages * page // MOBA_BLOCK
    in_specs = [pl.BlockSpec((1, page, N_HEADS, HEAD_DIM),
                             lambda s, j, pt, i=i: (pt[s, j * PAGES_PER_STEP + i], 0, 0, 0))
                for i in range(PAGES_PER_STEP)]
    grid_spec = pltpu.PrefetchScalarGridSpec(
        num_scalar_prefetch=1,
        grid=(n_seq, n_pages // PAGES_PER_STEP),
        in_specs=in_specs,
        out_specs=pl.BlockSpec((1, blocks_per_step, N_HEADS, HEAD_DIM), lambda s, j, pt: (s, j, 0, 0)),
    )
    return pl.pallas_call(
        _paged_kmean_kernel,
        grid_spec=grid_spec,
        out_shape=jax.ShapeDtypeStruct((n_seq, n_blk, N_HEADS, HEAD_DIM), F32),
        compiler_params=_params(2),
        name="paged_kmean",
    )(page_table, *([cache_k] * PAGES_PER_STEP))


def _block_select_kernel(q_ref, km_ref, sel_ref):
    gate = jnp.sum(km_ref[0] * q_ref[...], axis=-1, keepdims=True)
    n_blk = gate.shape[0]
    blk = lax.broadcasted_iota(jnp.int32, gate.shape, 0).astype(F32)
    rank = lax.broadcasted_iota(jnp.int32, sel_ref.shape[1:], 1)
    out = jnp.zeros(sel_ref.shape[1:], F32)
    for r in range(MOBA_TOPK):
        m = jnp.max(gate, axis=0, keepdims=True)
        first = jnp.min(jnp.where(gate == m, blk, float(n_blk)), axis=0, keepdims=True)
        gate = jnp.where(blk == first, NEG_INF, gate)
        out = jnp.where(rank == r, first[0], out)
    sel_ref[0] = out.astype(jnp.int32)


def _block_select(q4, kmean_s):
    n_seq, n_blk = kmean_s.shape[:2]
    return pl.pallas_call(
        _block_select_kernel,
        grid=(n_seq,),
        in_specs=[pl.BlockSpec((1, N_HEADS, HEAD_DIM), lambda s: (s, 0, 0)),
                  pl.BlockSpec((1, n_blk, N_HEADS, HEAD_DIM), lambda s: (s, 0, 0, 0))],
        out_specs=pl.BlockSpec((1, N_HEADS, LANES), lambda s: (s, 0, 0)),
        out_shape=jax.ShapeDtypeStruct((n_seq, N_HEADS, LANES), jnp.int32),
        compiler_params=_params(1),
        name="block_select",
    )(q4, kmean_s)


def _attn_sample_kernel(n_sel, pg_ref, q_ref, k_own_ref, v_own_ref, ck_hbm, cv_hbm, o_ref, kbuf, vbuf, sem):
    s = pl.program_id(0)

    def page_copies(seq, slot):
        cps = []
        for h in range(N_HEADS):
            for i in range(n_sel):
                pg = pg_ref[(seq * N_HEADS + h) * n_sel + i]
                cps.append(pltpu.make_async_copy(ck_hbm.at[pg, :, h, :], kbuf.at[slot, h, i], sem.at[slot]))
                cps.append(pltpu.make_async_copy(cv_hbm.at[pg, :, h, :], vbuf.at[slot, h, i], sem.at[slot]))
        return cps

    @pl.when(s == 0)
    def _():
        for cp in page_copies(0, 0):
            cp.start()

    @pl.when(s + 1 < pl.num_programs(0))
    def _():
        for cp in page_copies(s + 1, (s + 1) % 2):
            cp.start()

    slot = s % 2
    for cp in page_copies(s, slot):
        cp.wait()

    scale = HEAD_DIM ** -0.5
    for h in range(N_HEADS):
        hs = slice(h * HEAD_DIM, (h + 1) * HEAD_DIM)
        q = q_ref[0, :, hs]
        s_own = jnp.sum(k_own_ref[0, :, hs] * q, axis=-1, keepdims=True) * scale
        scores = [jnp.sum(kbuf[slot, h, i] * q, axis=-1, keepdims=True) * scale for i in range(n_sel)]
        m = s_own
        for sc in scores:
            m = jnp.maximum(m, jnp.max(sc, axis=0, keepdims=True))
        p_own = jnp.exp(s_own - m)
        l = p_own
        acc = p_own * v_own_ref[0, :, hs]
        for i, sc in enumerate(scores):
            p = jnp.exp(sc - m)
            l = l + jnp.sum(p, axis=0, keepdims=True)
            acc = acc + jnp.sum(p * vbuf[slot, h, i], axis=0, keepdims=True)
        o_ref[0, :, hs] = acc / l


def _attention_sample(q3, k3, v3, cache_k, cache_v, sel_pages):
    n_seq = q3.shape[0]
    page = cache_k.shape[1]
    n_sel = sel_pages.shape[-1]
    seq_row = lambda s, pg: (s, 0, 0)
    grid_spec = pltpu.PrefetchScalarGridSpec(
        num_scalar_prefetch=1,
        grid=(n_seq,),
        in_specs=[pl.BlockSpec((1, 1, D_MODEL), seq_row)] * 3
                 + [pl.BlockSpec(memory_space=pl.ANY), pl.BlockSpec(memory_space=pl.ANY)],
        out_specs=pl.BlockSpec((1, 1, D_MODEL), seq_row),
        scratch_shapes=[pltpu.VMEM((2, N_HEADS, n_sel, page, HEAD_DIM), F32),
                        pltpu.VMEM((2, N_HEADS, n_sel, page, HEAD_DIM), F32),
                        pltpu.SemaphoreType.DMA((2,))],
    )
    return pl.pallas_call(
        functools.partial(_attn_sample_kernel, n_sel),
        grid_spec=grid_spec,
        out_shape=jax.ShapeDtypeStruct((n_seq, 1, D_MODEL), F32),
        compiler_params=_params(1),
        name="moba_sample",
    )(sel_pages.reshape(-1), q3, k3, v3, cache_k, cache_v)


def _oproj_kernel(o_ref, x2_ref, wo_ref, mg_ref, wr_ref, br_ref, x3_ref, hm_ref, route_ref):
    x3 = x2_ref[...] + _mm(o_ref[...], wo_ref[...], True)
    x3_ref[...] = x3
    _moe_prenorm_and_route(x3, mg_ref, wr_ref, br_ref, hm_ref, route_ref, True)


def _oproj(o, x2, w):
    n = o.shape[0]
    return pl.pallas_call(
        _oproj_kernel,
        grid=(1,),
        in_specs=[_full((n, D_MODEL)), _full((n, D_MODEL)), _full((D_MODEL, D_MODEL)), _full((1, D_MODEL)),
                  _full((D_MODEL, LANES)), _full((1, LANES))],
        out_specs=[_full((n, D_MODEL)), _full((n * ROW_PIECES, LANES)), _full((n, LANES))],
        out_shape=[jax.ShapeDtypeStruct((n, D_MODEL), F32), jax.ShapeDtypeStruct((n * ROW_PIECES, LANES), F32),
                   jax.ShapeDtypeStruct((n, LANES), F32)],
        compiler_params=_params(1),
        name="oproj_sample",
    )(o, x2, w["f32"]["b_w_o"], w["moe_norm_g"][1], w["f32"]["moe_w_route"][1], w["moe_b_route"][1])


def _combine_kernel(x_ref, yp_ref, route_ref, o_ref):
    o_ref[...] = _moe_residual(x_ref[...], yp_ref, route_ref)


def _combine(x, ypairs, route):
    n = x.shape[0]
    tm = min(n, ROW_TILE)
    row = lambda i: (i, 0)
    return pl.pallas_call(
        _combine_kernel,
        grid=(n // tm,),
        in_specs=[pl.BlockSpec((tm, D_MODEL), row), pl.BlockSpec((EXPERT_TOPK * tm * ROW_PIECES, LANES), row),
                  pl.BlockSpec((tm, LANES), row)],
        out_specs=pl.BlockSpec((tm, D_MODEL), row),
        out_shape=jax.ShapeDtypeStruct((n, D_MODEL), F32),
        compiler_params=_params(1),
        name="combine",
    )(x, ypairs, route)


def _rope_tables(pos):
    inv_freq = ROPE_THETA ** (-jnp.arange(ROT_HALF, dtype=F32) / ROT_HALF)
    ang = pos.astype(F32)[:, None] * inv_freq[None, :]
    cos, sin = jnp.cos(ang), jnp.sin(ang)
    rest = HEAD_DIM - ROT_DIM
    cos_tab = jnp.concatenate([cos, cos, jnp.ones((pos.shape[0], rest), F32)], axis=-1)
    sin_tab = jnp.concatenate([-sin, sin, jnp.zeros((pos.shape[0], rest), F32)], axis=-1)
    return cos_tab, sin_tab


def kernel(x_prompt, x_sample, cache_k, cache_v, page_table, a_norm_g, a_w_in, a_ln_g, a_ln_b, a_w_s, a_b_s, a_w_out, kv_norm_g, w_kv, k_norm_g, b_norm_g, b_w_q, q_norm_g, b_w_o, moe_norm_g, moe_w_rg, moe_b_rg, moe_w_re, moe_b_re, moe_w_gate, moe_w_up, moe_w_down):
    bsz, seq, _ = x_prompt.shape
    n_dec = x_sample.shape[0]
    assert x_sample.shape[1] == 1 and a_w_in.shape[0] == 1 and b_w_q.shape[0] == 1
    assert seq % ROW_TILE == 0 and ROW_TILE == MOBA_BLOCK
    page = cache_k.shape[1]
    past = page_table.shape[1] * page
    assert past % MOBA_BLOCK == 0 and past // MOBA_BLOCK >= MOBA_TOPK and MOBA_BLOCK % page == 0

    depth = moe_norm_g.shape[0]
    pad = LANES - N_EXPERT_GROUPS - N_EXPERTS
    mats = {
        "a_w_in": a_w_in[0], "a_w_out": a_w_out[0], "w_kv": w_kv, "b_w_q": b_w_q[0], "b_w_o": b_w_o[0],
        "moe_w_route": jnp.concatenate([moe_w_rg, moe_w_re, jnp.zeros((depth, D_MODEL, pad), F32)], axis=-1),
        "moe_w_gate": moe_w_gate, "moe_w_up": moe_w_up, "moe_w_down": moe_w_down,
    }
    w = {
        "f32": mats, "bf16": {name: m.astype(BF16) for name, m in mats.items()},
        "a_norm_g": a_norm_g[0][None], "a_ln_g": a_ln_g[0][None], "a_ln_b": a_ln_b[0][None],
        "a_w_s": a_w_s[0], "a_b_s_t": a_b_s[0].T,
        "a_ws0": jnp.repeat(a_w_s[0, :, 0, 0], GROUP_WIDTH)[None], "a_bs0": jnp.repeat(a_b_s[0, :, 0], GROUP_WIDTH)[None],
        "kv_norm_g": kv_norm_g[None], "k_norm_g": k_norm_g[None],
        "b_norm_g": b_norm_g[0][None], "q_norm_g": q_norm_g[0][None],
        "moe_norm_g": moe_norm_g[:, None, :],
        "moe_b_route": jnp.concatenate([moe_b_rg, moe_b_re, jnp.zeros((depth, pad), F32)], axis=-1)[:, None, :],
    }

    xp = x_prompt.reshape(bsz * seq, D_MODEL)
    xs = x_sample.reshape(n_dec, D_MODEL)
    cos_p, sin_p = _rope_tables(jnp.arange(seq, dtype=jnp.int32))
    cos_s, sin_s = _rope_tables(jnp.full((n_dec,), past, dtype=jnp.int32))

    x1p, vrows_p, hmp, route0p = _gmlp(xp, w, False, seq)
    x1s, vrows_s, hms, route0s = _gmlp(xs, w, True, seq)
    yp = _experts(hmp, route0p, w, 0, EXPERT_TILE, False)
    ys = _experts(hms, route0s, w, 0, EXPERT_TILE_SMALL, True)

    x2p, k_p, v_p, kb_p, vb_p, q_p, kmean_p = _kvq(x1p, yp, route0p, cos_p, sin_p, w, True, False)
    x2s, k_s, v_s, _, _, q_s = _kvq(x1s, ys, route0s, cos_s, sin_s, w, False, True)

    x3p, hmp, route1p = _attention_prompt(q_p, kb_p, vb_p, kmean_p, x2p, w, seq)
    kmean_s = _paged_kmean(cache_k, page_table)
    sel_blocks = _block_select(q_s.reshape(n_dec, N_HEADS, HEAD_DIM), kmean_s)[:, :, :MOBA_TOPK]
    pages_per_block = MOBA_BLOCK // page
    page_slots = sel_blocks[..., None] * pages_per_block + jnp.arange(pages_per_block, dtype=jnp.int32)
    page_slots = page_slots.reshape(n_dec, N_HEADS * MOBA_TOPK * pages_per_block)
    sel_pages = jnp.take_along_axis(page_table, page_slots, axis=1).reshape(n_dec, N_HEADS, MOBA_TOPK * pages_per_block)
    o_s = _attention_sample(q_s.reshape(n_dec, 1, D_MODEL), k_s.reshape(n_dec, 1, D_MODEL),
                            v_s.reshape(n_dec, 1, D_MODEL), cache_k, cache_v, sel_pages)
    x3s, hms, route1s = _oproj(o_s.reshape(n_dec, D_MODEL), x2s, w)
    yp = _experts(hmp, route1p, w, 1, EXPERT_TILE, False)
    ys = _experts(hms, route1s, w, 1, EXPERT_TILE_SMALL, True)
    y_prompt = _combine(x3p, yp, route1p)
    y_sample = _combine(x3s, ys, route1s)

    return (y_prompt.reshape(bsz, seq, D_MODEL), y_sample.reshape(n_dec, 1, D_MODEL),
            vrows_p.reshape(1, bsz, CHUNK, GATE_WIDTH), vrows_s.reshape(1, n_dec, 1, GATE_WIDTH),
            k_p.reshape(bsz, seq, N_HEADS, HEAD_DIM), v_p.reshape(bsz, seq, N_HEADS, HEAD_DIM),
            k_s.reshape(n_dec, 1, N_HEADS, HEAD_DIM), v_s.reshape(n_dec, 1, N_HEADS, HEAD_DIM))
```

```python
import functools

import jax
import jax.numpy as jnp
from jax import lax
from jax.experimental import pallas as pl
from jax.experimental.pallas import tpu as pltpu

D_MODEL = 1024
CHUNK = 128
GATE_WIDTH = 2 * D_MODEL
N_GATE_GROUPS = 8
GROUP_WIDTH = GATE_WIDTH // N_GATE_GROUPS
HEAD_DIM = 128
N_HEADS = D_MODEL // HEAD_DIM
ROT_DIM = HEAD_DIM // 4
ROT_HALF = ROT_DIM // 2
ROPE_THETA = 500000.0
MOBA_BLOCK = 256
MOBA_TOPK = 3
N_EXPERT_GROUPS = 4
EXPERTS_PER_GROUP = 4
N_EXPERTS = N_EXPERT_GROUPS * EXPERTS_PER_GROUP
EXPERT_TOPK = 2
EXPERT_FF = D_MODEL // 2
EPS = 1e-6

LANES = 128
SUBLANES = 8
ROW_PIECES = D_MODEL // LANES
ROW_TILE = 256
EXPERT_TILE = 256
EXPERT_TILE_SMALL = 16
IN_PROJ_COLS = 512
PAGES_PER_STEP = 16
VMEM_LIMIT = 60 * 1024 * 1024

F32 = jnp.float32
BF16 = jnp.bfloat16
NEG_INF = float("-inf")

assert ROW_PIECES == SUBLANES


def _params(n_axes):
    return pltpu.CompilerParams(dimension_semantics=("arbitrary",) * n_axes, vmem_limit_bytes=VMEM_LIMIT)


def _rms(x, g):
    return x * lax.rsqrt(jnp.mean(x * x, axis=-1, keepdims=True) + EPS) * g


def _mm(a, b, exact):
    if exact:
        return jnp.dot(a, b, precision=lax.Precision.HIGHEST, preferred_element_type=F32)
    return jnp.dot(a.astype(BF16), b, preferred_element_type=F32)


def _full(shape):
    nd = len(shape)
    return pl.BlockSpec(shape, lambda *_: (0,) * nd, pipeline_mode=pl.Buffered(1))


def _store_tile_rows(ref, x):
    for s in range(ROW_PIECES):
        ref[pl.ds(s, x.shape[0], stride=ROW_PIECES), :] = x[:, s * LANES:(s + 1) * LANES]


def _load_tile_rows(ref, rows, first=0, step=ROW_PIECES):
    return jnp.concatenate([ref[pl.ds(first + s, rows, stride=step), :] for s in range(ROW_PIECES)], axis=-1)


def _route(hm, wr_ref, br_ref, exact):
    logits = _mm(hm, wr_ref[...], exact) + br_ref[...]
    lane = lax.broadcasted_iota(jnp.int32, logits.shape, 1).astype(F32)
    lg = jnp.where(lane < N_EXPERT_GROUPS, logits, NEG_INF)
    mg = jnp.max(lg, axis=-1, keepdims=True)
    g_sel = jnp.min(jnp.where(lg == mg, lane, float(LANES)), axis=-1, keepdims=True)
    p_grp = 1.0 / jnp.sum(jnp.exp(lg - mg), axis=-1, keepdims=True)
    lo = N_EXPERT_GROUPS + g_sel * EXPERTS_PER_GROUP
    le = jnp.where((lane >= lo) & (lane < lo + EXPERTS_PER_GROUP), logits, NEG_INF)
    v1 = jnp.max(le, axis=-1, keepdims=True)
    i1 = jnp.min(jnp.where(le == v1, lane, float(LANES)), axis=-1, keepdims=True)
    le2 = jnp.where(lane == i1, NEG_INF, le)
    v2 = jnp.max(le2, axis=-1, keepdims=True)
    i2 = jnp.min(jnp.where(le2 == v2, lane, float(LANES)), axis=-1, keepdims=True)
    t = jnp.exp(v2 - v1)
    w1 = p_grp / (1.0 + t)
    w2 = p_grp * t / (1.0 + t)
    out = jnp.where(lane == 0.0, i1 - N_EXPERT_GROUPS, 0.0)
    out = jnp.where(lane == 1.0, i2 - N_EXPERT_GROUPS, out)
    out = jnp.where(lane == 2.0, w1, out)
    out = jnp.where(lane == 3.0, w2, out)
    return out


def _moe_prenorm_and_route(x, mg_ref, wr_ref, br_ref, hm_ref, route_ref, exact):
    hm = _rms(x, mg_ref[...])
    _store_tile_rows(hm_ref, hm)
    route_ref[...] = _route(hm, wr_ref, br_ref, exact)


def _moe_residual(x, y_ref, route_ref):
    rows = x.shape[0]
    step = EXPERT_TOPK * ROW_PIECES
    y1 = _load_tile_rows(y_ref, rows, 0, step)
    y2 = _load_tile_rows(y_ref, rows, ROW_PIECES, step)
    return x + (route_ref[:, 2:3] * y1 + route_ref[:, 3:4] * y2)


def _gmlp_kernel(sample, tm, x_ref, ng_ref, win_ref, lng_ref, lnb_ref, ws_ref, bs_ref, wout_ref,
                 mg_ref, wr_ref, br_ref, x1_ref, vrows_ref, hm_ref, route_ref, u_scr, v_scr, gated_scr):
    x = x_ref[...]
    h = _rms(x, ng_ref[...])
    if not sample:
        h = h.astype(BF16)
    n_col = 2 * GATE_WIDTH // IN_PROJ_COLS
    n_ucol = GATE_WIDTH // IN_PROJ_COLS
    vsum = jnp.zeros((tm, 1), F32)
    for c in range(n_col):
        z = _mm(h, win_ref[:, c * IN_PROJ_COLS:(c + 1) * IN_PROJ_COLS], sample)
        z = jax.nn.gelu(z, approximate=True)
        if c < n_ucol:
            u_scr[:, c * IN_PROJ_COLS:(c + 1) * IN_PROJ_COLS] = z
        else:
            v_scr[:, (c - n_ucol) * IN_PROJ_COLS:(c - n_ucol + 1) * IN_PROJ_COLS] = z
            vsum = vsum + jnp.sum(z, axis=-1, keepdims=True)
    mean = vsum / GATE_WIDTH
    vss = jnp.zeros((tm, 1), F32)
    for c in range(n_ucol):
        xc = v_scr[:, c * IN_PROJ_COLS:(c + 1) * IN_PROJ_COLS] - mean
        vss = vss + jnp.sum(xc * xc, axis=-1, keepdims=True)
    rstd = lax.rsqrt(vss / GATE_WIDTH + EPS)
    for c in range(n_ucol):
        cs = slice(c * IN_PROJ_COLS, (c + 1) * IN_PROJ_COLS)
        v_scr[:, cs] = (v_scr[:, cs] - mean) * rstd * lng_ref[:, cs] + lnb_ref[:, cs]

    if sample:
        vrows_ref[...] = v_scr[...]
        gated_scr[...] = u_scr[...] * (v_scr[...] * ws_ref[...] + bs_ref[...])
    else:
        vrows_ref[0] = v_scr[tm - CHUNK:tm, :]
        t_out = lax.broadcasted_iota(jnp.int32, (CHUNK, CHUNK), 0)
        s_in = lax.broadcasted_iota(jnp.int32, (CHUNK, CHUNK), 1)
        for g in range(N_GATE_GROUPS):
            gs = slice(g * GROUP_WIDTH, (g + 1) * GROUP_WIDTH)
            wsg = jnp.where(s_in <= t_out, ws_ref[g], 0.0).astype(BF16)
            bias = bs_ref[:, g:g + 1]
            for c in range(tm // CHUNK):
                rs = slice(c * CHUNK, (c + 1) * CHUNK)
                mixed = jnp.dot(wsg, v_scr[rs, gs].astype(BF16), preferred_element_type=F32) + bias
                gated_scr[rs, gs] = (u_scr[rs, gs] * mixed).astype(BF16)

    x1 = x + _mm(gated_scr[...], wout_ref[...], sample)
    x1_ref[...] = x1
    _moe_prenorm_and_route(x1, mg_ref, wr_ref, br_ref, hm_ref, route_ref, sample)


def _gmlp(x, w, sample, seq):
    n = x.shape[0]
    mw = w["f32"] if sample else w["bf16"]
    tm = n if sample else ROW_TILE
    row = lambda i: (i, 0)
    if sample:
        ws_spec, bs_spec = _full((1, GATE_WIDTH)), _full((1, GATE_WIDTH))
        vrows_shape = jax.ShapeDtypeStruct((n, GATE_WIDTH), F32)
        vrows_spec = pl.BlockSpec((tm, GATE_WIDTH), row)
    else:
        ws_spec, bs_spec = _full((N_GATE_GROUPS, CHUNK, CHUNK)), _full((CHUNK, N_GATE_GROUPS))
        tiles_per_seq = seq // tm
        vrows_shape = jax.ShapeDtypeStruct((n // seq, CHUNK, GATE_WIDTH), F32)
        vrows_spec = pl.BlockSpec((1, CHUNK, GATE_WIDTH), lambda i: (i // tiles_per_seq, 0, 0))
    return pl.pallas_call(
        functools.partial(_gmlp_kernel, sample, tm),
        grid=(n // tm,),
        in_specs=[pl.BlockSpec((tm, D_MODEL), row), _full((1, D_MODEL)), _full((D_MODEL, 2 * GATE_WIDTH)),
                  _full((1, GATE_WIDTH)), _full((1, GATE_WIDTH)), ws_spec, bs_spec, _full((GATE_WIDTH, D_MODEL)),
                  _full((1, D_MODEL)), _full((D_MODEL, LANES)), _full((1, LANES))],
        out_specs=[pl.BlockSpec((tm, D_MODEL), row), vrows_spec, pl.BlockSpec((tm * ROW_PIECES, LANES), row),
                   pl.BlockSpec((tm, LANES), row)],
        out_shape=[jax.ShapeDtypeStruct((n, D_MODEL), F32), vrows_shape,
                   jax.ShapeDtypeStruct((n * ROW_PIECES, LANES), F32), jax.ShapeDtypeStruct((n, LANES), F32)],
        scratch_shapes=[pltpu.VMEM((tm, GATE_WIDTH), F32), pltpu.VMEM((tm, GATE_WIDTH), F32),
                        pltpu.VMEM((tm, GATE_WIDTH), F32 if sample else BF16)],
        compiler_params=_params(1),
        name="gmlp_sample" if sample else "gmlp_prompt",
    )(x, w["a_norm_g"], mw["a_w_in"], w["a_ln_g"], w["a_ln_b"],
      w["a_ws0"] if sample else w["a_w_s"], w["a_bs0"] if sample else w["a_b_s_t"], mw["a_w_out"],
      w["moe_norm_g"][0], mw["moe_w_route"][0], w["moe_b_route"][0])


def _expert_kernel(tmg, n_tiles, exact, te_ref, nv_ref, ts_ref, od_ref, h_hbm, wg_ref, wu_ref, wd_ref, out_hbm,
                   xbuf, ybuf, sem_in, sem_out, *bf16_weights):
    i = pl.program_id(0)
    rows8 = SUBLANES

    def gather_row(r, slot, pair=0):
        tok8 = pl.multiple_of((pair >> 1) * rows8, rows8)
        return pltpu.make_async_copy(h_hbm.at[pl.ds(tok8, rows8)], xbuf.at[slot, pl.ds(r * rows8, rows8)],
                                     sem_in.at[slot])

    def scatter_row(r, slot, pair=0):
        pair8 = pl.multiple_of(pair * rows8, rows8)
        return pltpu.make_async_copy(ybuf.at[slot, pl.ds(pl.multiple_of(r * rows8, rows8), rows8)],
                                     out_hbm.at[pl.ds(pair8, rows8)], sem_out.at[slot])

    def for_scatter_rows(tile, slot, start):
        base, n_rows = ts_ref[tile], nv_ref[tile]

        def one(r):
            if start:
                scatter_row(r, slot, od_ref[base + r]).start()
            else:
                scatter_row(r, slot).wait()

        @pl.when(n_rows == tmg)
        def _():
            for r in range(tmg):
                one(r)

        @pl.when(n_rows < tmg)
        def _():
            def body(r, c):
                one(r)
                return c
            lax.fori_loop(0, n_rows, body, 0)

    nxt = jnp.minimum(i, n_tiles - 1)

    @pl.when((i < n_tiles) & (nv_ref[nxt] > 0))
    def _():
        base = ts_ref[nxt]
        for r in range(tmg):
            gather_row(r, nxt % 2, od_ref[base + r]).start()

    c = jnp.maximum(i - 1, 0)

    @pl.when((i >= 1) & (nv_ref[c] > 0))
    def _():
        slot = c % 2
        for r in range(tmg):
            gather_row(r, slot).wait()
        if exact:
            wg, wu, wd = wg_ref, wu_ref, wd_ref
        else:
            wg, wu, wd = bf16_weights

            @pl.when((c == 0) | (te_ref[c] != te_ref[jnp.maximum(c - 1, 0)]))
            def _():
                wg[...] = wg_ref[...].astype(BF16)
                wu[...] = wu_ref[...].astype(BF16)
                wd[...] = wd_ref[...].astype(BF16)

        x = _load_tile_rows(xbuf.at[slot], tmg)
        a = jax.nn.silu(_mm(x, wg[...], exact)) * _mm(x, wu[...], exact)
        y = _mm(a, wd[...], exact)

        @pl.when(c >= 2)
        def _():
            for_scatter_rows(c - 2, slot, False)

        _store_tile_rows(ybuf.at[slot], y)
        for_scatter_rows(c, slot, True)

        @pl.when((c == n_tiles - 1) | (nv_ref[jnp.minimum(c + 1, n_tiles - 1)] == 0))
        def _():
            @pl.when(c >= 1)
            def _():
                for_scatter_rows(c - 1, 1 - slot, False)
            for_scatter_rows(c, slot, False)


def _dispatch(route, tmg):
    n_pairs = EXPERT_TOPK * route.shape[0]
    e_flat = route[:, 0:EXPERT_TOPK].astype(jnp.int32).reshape(n_pairs)
    n_tiles = -(-(n_pairs + N_EXPERTS * (tmg - 1)) // tmg)
    order = jnp.argsort(e_flat, stable=True).astype(jnp.int32)
    counts = jnp.sum((e_flat[:, None] == jnp.arange(N_EXPERTS, dtype=jnp.int32)[None, :]).astype(jnp.int32), axis=0)
    tiles_per = -(-counts // tmg)
    t_end = jnp.cumsum(tiles_per)
    t_start = t_end - tiles_per
    c_start = jnp.cumsum(counts) - counts
    tile = jnp.arange(n_tiles, dtype=jnp.int32)
    tile_expert = jnp.minimum(jnp.sum((t_end[None, :] <= tile[:, None]).astype(jnp.int32), axis=1), N_EXPERTS - 1)
    first_row = (tile - t_start[tile_expert]) * tmg
    tile_valid = jnp.clip(counts[tile_expert] - first_row, 0, tmg).astype(jnp.int32)
    tile_src = jnp.where(tile_valid > 0, c_start[tile_expert] + first_row, 0).astype(jnp.int32)
    order = jnp.concatenate([order, jnp.zeros((tmg,), jnp.int32)])
    return tile_expert.astype(jnp.int32), tile_valid, tile_src, order


def _experts(hm, route, w, layer, tmg, exact):
    n = route.shape[0]
    w_gate, w_up, w_down = w["moe_w_gate"], w["moe_w_up"], w["moe_w_down"]
    tile_expert, tile_valid, tile_src, order = _dispatch(route, tmg)
    n_tiles = tile_expert.shape[0]
    by_expert = lambda i, te, nv, ts, od: (layer, te[jnp.maximum(i - 1, 0)], 0, 0)
    scratch = [pltpu.VMEM((2, tmg * ROW_PIECES, LANES), F32), pltpu.VMEM((2, tmg * ROW_PIECES, LANES), F32),
               pltpu.SemaphoreType.DMA((2,)), pltpu.SemaphoreType.DMA((2,))]
    if not exact:
        scratch += [pltpu.VMEM((D_MODEL, EXPERT_FF), BF16), pltpu.VMEM((D_MODEL, EXPERT_FF), BF16),
                    pltpu.VMEM((EXPERT_FF, D_MODEL), BF16)]
    grid_spec = pltpu.PrefetchScalarGridSpec(
        num_scalar_prefetch=4,
        grid=(n_tiles + 1,),
        in_specs=[pl.BlockSpec(memory_space=pl.ANY),
                  pl.BlockSpec((None, None, D_MODEL, EXPERT_FF), by_expert),
                  pl.BlockSpec((None, None, D_MODEL, EXPERT_FF), by_expert),
                  pl.BlockSpec((None, None, EXPERT_FF, D_MODEL), by_expert)],
        out_specs=pl.BlockSpec(memory_space=pl.ANY),
        scratch_shapes=scratch,
    )
    return pl.pallas_call(
        functools.partial(_expert_kernel, tmg, n_tiles, exact),
        grid_spec=grid_spec,
        out_shape=jax.ShapeDtypeStruct((EXPERT_TOPK * n * ROW_PIECES, LANES), F32),
        compiler_params=_params(1),
        name="experts",
    )(tile_expert, tile_valid, tile_src, order, hm, w_gate, w_up, w_down)


def _rope(xh, cos, sin, lane):
    swapped = jnp.where(lane < ROT_HALF, pltpu.roll(xh, LANES - ROT_HALF, 1), pltpu.roll(xh, ROT_HALF, 1))
    return xh * cos + swapped * sin


def _kvq_kernel(prompt, exact, x1_ref, yp_ref, route_ref, kvg_ref, wkv_ref, kg_ref, cos_ref, sin_ref,
                bng_ref, wq_ref, qg_ref, x2_ref, k_ref, v_ref, q_ref, *prompt_refs):
    x2 = _moe_residual(x1_ref[...], yp_ref, route_ref)
    x2_ref[...] = x2
    cos = cos_ref[...]
    sin = sin_ref[...]
    lane = lax.broadcasted_iota(jnp.int32, cos.shape, 1)
    kv = _mm(_rms(x2, kvg_ref[...]), wkv_ref[...], exact)
    for h in range(N_HEADS):
        hs = slice(h * HEAD_DIM, (h + 1) * HEAD_DIM)
        kh = _rope(_rms(kv[:, hs], kg_ref[...]), cos, sin, lane)
        k_ref[:, hs] = kh
        if prompt:
            kb_ref, _, km_ref = prompt_refs
            kb_ref[:, hs] = kh.astype(BF16)
            km_ref[0, :, hs] = jnp.mean(kh, axis=0, keepdims=True)
    v = kv[:, D_MODEL:]
    v_ref[...] = v
    if prompt:
        prompt_refs[1][0] = v.T.astype(BF16)
    q = _mm(_rms(x2, bng_ref[...]), wq_ref[...], exact)
    for h in range(N_HEADS):
        hs = slice(h * HEAD_DIM, (h + 1) * HEAD_DIM)
        q_ref[:, hs] = _rope(_rms(q[:, hs], qg_ref[...]), cos, sin, lane)


def _kvq(x1, ypairs, route, cos_tab, sin_tab, w, prompt, exact):
    n = x1.shape[0]
    mw = w["f32"] if exact else w["bf16"]
    tm = ROW_TILE if prompt else n
    row = lambda i: (i, 0)
    tab_tiles = cos_tab.shape[0] // tm
    tab = lambda i: (i % tab_tiles, 0)
    out_specs = [pl.BlockSpec((tm, D_MODEL), row)] * 4
    out_shape = [jax.ShapeDtypeStruct((n, D_MODEL), F32)] * 4
    if prompt:
        seq = cos_tab.shape[0]
        out_specs += [pl.BlockSpec((tm, D_MODEL), row),
                      pl.BlockSpec((1, D_MODEL, tm), lambda i: (i // tab_tiles, 0, i % tab_tiles)),
                      pl.BlockSpec((1, 1, D_MODEL), lambda i: (i, 0, 0))]
        out_shape += [jax.ShapeDtypeStruct((n, D_MODEL), BF16), jax.ShapeDtypeStruct((n // seq, D_MODEL, seq), BF16),
                      jax.ShapeDtypeStruct((n // tm, 1, D_MODEL), F32)]
    return pl.pallas_call(
        functools.partial(_kvq_kernel, prompt, exact),
        grid=(n // tm,),
        in_specs=[pl.BlockSpec((tm, D_MODEL), row), pl.BlockSpec((EXPERT_TOPK * tm * ROW_PIECES, LANES), row),
                  pl.BlockSpec((tm, LANES), row),
                  _full((1, D_MODEL)), _full((D_MODEL, 2 * D_MODEL)), _full((1, HEAD_DIM)),
                  pl.BlockSpec((tm, HEAD_DIM), tab), pl.BlockSpec((tm, HEAD_DIM), tab),
                  _full((1, D_MODEL)), _full((D_MODEL, D_MODEL)), _full((1, HEAD_DIM))],
        out_specs=out_specs,
        out_shape=out_shape,
        compiler_params=_params(1),
        name="kvq",
    )(x1, ypairs, route, w["kv_norm_g"], mw["w_kv"], w["k_norm_g"], cos_tab, sin_tab,
      w["b_norm_g"], mw["b_w_q"], w["q_norm_g"])


def _attn_kernel(q_ref, kb_ref, vt_ref, km_ref, x2_ref, wo_ref, mg_ref, wr_ref, br_ref,
                 x3_ref, hm_ref, route_ref, qb_scr, sel_scr, o_scr, *acc_scrs):
    j = pl.program_id(1)
    tq = q_ref.shape[0]
    n_blk = km_ref.shape[1]
    scale = HEAD_DIM ** -0.5
    blk = lax.broadcasted_iota(jnp.int32, (n_blk, tq), 0).astype(F32)
    cur = j.astype(F32)
    contract_last = (((1,), (1,)), ((), ()))
    own = pl.multiple_of(j * MOBA_BLOCK, MOBA_BLOCK)
    qb_scr[...] = q_ref[...].astype(BF16)
    chains = [(h, g) for h in range(N_HEADS) for g in range(tq // LANES)]

    def scores(h, g, off):
        hs = slice(h * HEAD_DIM, (h + 1) * HEAD_DIM)
        return lax.dot_general(kb_ref[pl.ds(off, MOBA_BLOCK), hs], qb_scr[g * LANES:(g + 1) * LANES, hs],
                               contract_last, preferred_element_type=F32) * scale

    def weighted_values(h, off, p):
        hs = slice(h * HEAD_DIM, (h + 1) * HEAD_DIM)
        return jnp.dot(vt_ref[0, hs, pl.ds(off, MOBA_BLOCK)], p.astype(BF16), preferred_element_type=F32)

    for h in range(N_HEADS):
        hs = slice(h * HEAD_DIM, (h + 1) * HEAD_DIM)
        gate = lax.dot_general(km_ref[0, :, hs].astype(BF16), qb_scr[:, hs], contract_last,
                               preferred_element_type=F32)
        gate = jnp.where(blk < cur, gate, NEG_INF)
        sel = jnp.zeros((n_blk, tq), F32)
        for _ in range(MOBA_TOPK):
            m = jnp.max(gate, axis=0, keepdims=True)
            first = jnp.min(jnp.where(gate == m, blk, float(n_blk)), axis=0, keepdims=True)
            pick = (blk == first) & (m > NEG_INF)
            sel = jnp.where(pick, 1.0, sel)
            gate = jnp.where(pick, NEG_INF, gate)
        sel_scr[h] = sel

    k_pos = lax.broadcasted_iota(jnp.int32, (MOBA_BLOCK, LANES), 0)
    q_pos = lax.broadcasted_iota(jnp.int32, (MOBA_BLOCK, LANES), 1)
    stats = []
    for h, g in chains:
        s = jnp.where(k_pos <= q_pos + g * LANES, scores(h, g, own), NEG_INF)
        m0 = jnp.max(s, axis=0, keepdims=True)
        p = jnp.exp(s - m0)
        stats += [m0, jnp.sum(p, axis=0, keepdims=True)]
        acc_scrs[h][:, g * LANES:(g + 1) * LANES] = weighted_values(h, own, p)

    def past_block(jj, stats):
        off = pl.multiple_of(jj * MOBA_BLOCK, MOBA_BLOCK)
        new_stats = []
        visible = [sel_scr[h, pl.ds(jj, 1), :] for h in range(N_HEADS)]
        for c, (h, g) in enumerate(chains):
            qs = slice(g * LANES, (g + 1) * LANES)
            m_old, l_old = stats[2 * c], stats[2 * c + 1]
            s = jnp.where(visible[h][:, qs] > 0.0, scores(h, g, off), NEG_INF)
            m_new = jnp.maximum(m_old, jnp.max(s, axis=0, keepdims=True))
            alpha = jnp.exp(m_old - m_new)
            p = jnp.exp(s - m_new)
            new_stats += [m_new, alpha * l_old + jnp.sum(p, axis=0, keepdims=True)]
            acc_scrs[h][:, qs] = alpha * acc_scrs[h][:, qs] + weighted_values(h, off, p)
        return tuple(new_stats)

    stats = lax.fori_loop(0, j, past_block, tuple(stats))

    for c, (h, g) in enumerate(chains):
        qs = slice(g * LANES, (g + 1) * LANES)
        o_scr[qs, h * HEAD_DIM:(h + 1) * HEAD_DIM] = (acc_scrs[h][:, qs] / stats[2 * c + 1]).T.astype(BF16)

    x3 = x2_ref[...] + jnp.dot(o_scr[...], wo_ref[...], preferred_element_type=F32)
    x3_ref[...] = x3
    _moe_prenorm_and_route(x3, mg_ref, wr_ref, br_ref, hm_ref, route_ref, False)


def _attention_prompt(q, kb, vt, kmean, x2, w, seq):
    n = q.shape[0]
    bsz = n // seq
    n_qb = seq // MOBA_BLOCK
    row = lambda b, j: (b * n_qb + j, 0)
    return pl.pallas_call(
        _attn_kernel,
        grid=(bsz, n_qb),
        in_specs=[pl.BlockSpec((MOBA_BLOCK, D_MODEL), row),
                  pl.BlockSpec((seq, D_MODEL), lambda b, j: (b, 0)),
                  pl.BlockSpec((1, D_MODEL, seq), lambda b, j: (b, 0, 0)),
                  pl.BlockSpec((1, n_qb, D_MODEL), lambda b, j: (b, 0, 0)),
                  pl.BlockSpec((MOBA_BLOCK, D_MODEL), row),
                  _full((D_MODEL, D_MODEL)), _full((1, D_MODEL)), _full((D_MODEL, LANES)), _full((1, LANES))],
        out_specs=[pl.BlockSpec((MOBA_BLOCK, D_MODEL), row), pl.BlockSpec((MOBA_BLOCK * ROW_PIECES, LANES), row),
                   pl.BlockSpec((MOBA_BLOCK, LANES), row)],
        out_shape=[jax.ShapeDtypeStruct((n, D_MODEL), F32), jax.ShapeDtypeStruct((n * ROW_PIECES, LANES), F32),
                   jax.ShapeDtypeStruct((n, LANES), F32)],
        scratch_shapes=[pltpu.VMEM((MOBA_BLOCK, D_MODEL), BF16),
                        pltpu.VMEM((N_HEADS, n_qb, MOBA_BLOCK), F32),
                        pltpu.VMEM((MOBA_BLOCK, D_MODEL), BF16)]
                       + [pltpu.VMEM((HEAD_DIM, MOBA_BLOCK), F32)] * N_HEADS,
        compiler_params=_params(2),
        name="moba_prompt",
    )(q, kb, vt, kmean.reshape(bsz, n_qb, D_MODEL), x2, w["bf16"]["b_w_o"],
      w["moe_norm_g"][1], w["bf16"]["moe_w_route"][1], w["moe_b_route"][1])


def _paged_kmean_kernel(pt_ref, *refs):
    del pt_ref
    page_refs, out_ref = refs[:-1], refs[-1]
    pages_per_block = MOBA_BLOCK // page_refs[0].shape[1]
    for b in range(len(page_refs) // pages_per_block):
        tot = jnp.sum(page_refs[b * pages_per_block][0], axis=0)
        for p in range(1, pages_per_block):
            tot = tot + jnp.sum(page_refs[b * pages_per_block + p][0], axis=0)
        out_ref[0, b] = tot / MOBA_BLOCK


def _paged_kmean(cache_k, page_table):
    n_seq, n_pages = page_table.shape
    page = cache_k.shape[1]
    blocks_per_step = PAGES_PER_STEP * page // MOBA_BLOCK
    n_blk = n_pages * page // MOBA_BLOCK
    in_specs = [pl.BlockSpec((1, page, N_HEADS, HEAD_DIM),
                             lambda s, j, pt, i=i: (pt[s, j * PAGES_PER_STEP + i], 0, 0, 0))
                for i in range(PAGES_PER_STEP)]
    grid_spec = pltpu.PrefetchScalarGridSpec(
        num_scalar_prefetch=1,
        grid=(n_seq, n_pages // PAGES_PER_STEP),
        in_specs=in_specs,
        out_specs=pl.BlockSpec((1, blocks_per_step, N_HEADS, HEAD_DIM), lambda s, j, pt: (s, j, 0, 0)),
    )
    return pl.pallas_call(
        _paged_kmean_kernel,
        grid_spec=grid_spec,
        out_shape=jax.ShapeDtypeStruct((n_seq, n_blk, N_HEADS, HEAD_DIM), F32),
        compiler_params=_params(2),
        name="paged_kmean",
    )(page_table, *([cache_k] * PAGES_PER_STEP))


def _block_select_kernel(q_ref, km_ref, sel_ref):
    gate = jnp.sum(km_ref[0] * q_ref[...], axis=-1, keepdims=True)
    n_blk = gate.shape[0]
    blk = lax.broadcasted_iota(jnp.int32, gate.shape, 0).astype(F32)
    rank = lax.broadcasted_iota(jnp.int32, sel_ref.shape[1:], 1)
    out = jnp.zeros(sel_ref.shape[1:], F32)
    for r in range(MOBA_TOPK):
        m = jnp.max(gate, axis=0, keepdims=True)
        first = jnp.min(jnp.where(gate == m, blk, float(n_blk)), axis=0, keepdims=True)
        gate = jnp.where(blk == first, NEG_INF, gate)
        out = jnp.where(rank == r, first[0], out)
    sel_ref[0] = out.astype(jnp.int32)


def _block_select(q4, kmean_s):
    n_seq, n_blk = kmean_s.shape[:2]
    return pl.pallas_call(
        _block_select_kernel,
        grid=(n_seq,),
        in_specs=[pl.BlockSpec((1, N_HEADS, HEAD_DIM), lambda s: (s, 0, 0)),
                  pl.BlockSpec((1, n_blk, N_HEADS, HEAD_DIM), lambda s: (s, 0, 0, 0))],
        out_specs=pl.BlockSpec((1, N_HEADS, LANES), lambda s: (s, 0, 0)),
        out_shape=jax.ShapeDtypeStruct((n_seq, N_HEADS, LANES), jnp.int32),
        compiler_params=_params(1),
        name="block_select",
    )(q4, kmean_s)


def _attn_sample_kernel(n_sel, pg_ref, q_ref, k_own_ref, v_own_ref, ck_hbm, cv_hbm, o_ref, kbuf, vbuf, sem):
    s = pl.program_id(0)

    def page_copies(seq, slot):
        cps = []
        for h in range(N_HEADS):
            for i in range(n_sel):
                pg = pg_ref[(seq * N_HEADS + h) * n_sel + i]
                cps.append(pltpu.make_async_copy(ck_hbm.at[pg, :, h, :], kbuf.at[slot, h, i], sem.at[slot]))
                cps.append(pltpu.make_async_copy(cv_hbm.at[pg, :, h, :], vbuf.at[slot, h, i], sem.at[slot]))
        return cps

    @pl.when(s == 0)
    def _():
        for cp in page_copies(0, 0):
            cp.start()

    @pl.when(s + 1 < pl.num_programs(0))
    def _():
        for cp in page_copies(s + 1, (s + 1) % 2):
            cp.start()

    slot = s % 2
    for cp in page_copies(s, slot):
        cp.wait()

    scale = HEAD_DIM ** -0.5
    for h in range(N_HEADS):
        hs = slice(h * HEAD_DIM, (h + 1) * HEAD_DIM)
        q = q_ref[0, :, hs]
        s_own = jnp.sum(k_own_ref[0, :, hs] * q, axis=-1, keepdims=True) * scale
        scores = [jnp.sum(kbuf[slot, h, i] * q, axis=-1, keepdims=True) * scale for i in range(n_sel)]
        m = s_own
        for sc in scores:
            m = jnp.maximum(m, jnp.max(sc, axis=0, keepdims=True))
        p_own = jnp.exp(s_own - m)
        l = p_own
        acc = p_own * v_own_ref[0, :, hs]
        for i, sc in enumerate(scores):
            p = jnp.exp(sc - m)
            l = l + jnp.sum(p, axis=0, keepdims=True)
            acc = acc + jnp.sum(p * vbuf[slot, h, i], axis=0, keepdims=True)
        o_ref[0, :, hs] = acc / l


def _attention_sample(q3, k3, v3, cache_k, cache_v, sel_pages):
    n_seq = q3.shape[0]
    page = cache_k.shape[1]
    n_sel = sel_pages.shape[-1]
    seq_row = lambda s, pg: (s, 0, 0)
    grid_spec = pltpu.PrefetchScalarGridSpec(
        num_scalar_prefetch=1,
        grid=(n_seq,),
        in_specs=[pl.BlockSpec((1, 1, D_MODEL), seq_row)] * 3
                 + [pl.BlockSpec(memory_space=pl.ANY), pl.BlockSpec(memory_space=pl.ANY)],
        out_specs=pl.BlockSpec((1, 1, D_MODEL), seq_row),
        scratch_shapes=[pltpu.VMEM((2, N_HEADS, n_sel, page, HEAD_DIM), F32),
                        pltpu.VMEM((2, N_HEADS, n_sel, page, HEAD_DIM), F32),
                        pltpu.SemaphoreType.DMA((2,))],
    )
    return pl.pallas_call(
        functools.partial(_attn_sample_kernel, n_sel),
        grid_spec=grid_spec,
        out_shape=jax.ShapeDtypeStruct((n_seq, 1, D_MODEL), F32),
        compiler_params=_params(1),
        name="moba_sample",
    )(sel_pages.reshape(-1), q3, k3, v3, cache_k, cache_v)


def _oproj_kernel(o_ref, x2_ref, wo_ref, mg_ref, wr_ref, br_ref, x3_ref, hm_ref, route_ref):
    x3 = x2_ref[...] + _mm(o_ref[...], wo_ref[...], True)
    x3_ref[...] = x3
    _moe_prenorm_and_route(x3, mg_ref, wr_ref, br_ref, hm_ref, route_ref, True)


def _oproj(o, x2, w):
    n = o.shape[0]
    return pl.pallas_call(
        _oproj_kernel,
        grid=(1,),
        in_specs=[_full((n, D_MODEL)), _full((n, D_MODEL)), _full((D_MODEL, D_MODEL)), _full((1, D_MODEL)),
                  _full((D_MODEL, LANES)), _full((1, LANES))],
        out_specs=[_full((n, D_MODEL)), _full((n * ROW_PIECES, LANES)), _full((n, LANES))],
        out_shape=[jax.ShapeDtypeStruct((n, D_MODEL), F32), jax.ShapeDtypeStruct((n * ROW_PIECES, LANES), F32),
                   jax.ShapeDtypeStruct((n, LANES), F32)],
        compiler_params=_params(1),
        name="oproj_sample",
    )(o, x2, w["f32"]["b_w_o"], w["moe_norm_g"][1], w["f32"]["moe_w_route"][1], w["moe_b_route"][1])


def _combine_kernel(x_ref, yp_ref, route_ref, o_ref):
    o_ref[...] = _moe_residual(x_ref[...], yp_ref, route_ref)


def _combine(x, ypairs, route):
    n = x.shape[0]
    tm = min(n, ROW_TILE)
    row = lambda i: (i, 0)
    return pl.pallas_call(
        _combine_kernel,
        grid=(n // tm,),
        in_specs=[pl.BlockSpec((tm, D_MODEL), row), pl.BlockSpec((EXPERT_TOPK * tm * ROW_PIECES, LANES), row),
                  pl.BlockSpec((tm, LANES), row)],
        out_specs=pl.BlockSpec((tm, D_MODEL), row),
        out_shape=jax.ShapeDtypeStruct((n, D_MODEL), F32),
        compiler_params=_params(1),
        name="combine",
    )(x, ypairs, route)


def _rope_tables(pos):
    inv_freq = ROPE_THETA ** (-jnp.arange(ROT_HALF, dtype=F32) / ROT_HALF)
    ang = pos.astype(F32)[:, None] * inv_freq[None, :]
    cos, sin = jnp.cos(ang), jnp.sin(ang)
    rest = HEAD_DIM - ROT_DIM
    cos_tab = jnp.concatenate([cos, cos, jnp.ones((pos.shape[0], rest), F32)], axis=-1)
    sin_tab = jnp.concatenate([-sin, sin, jnp.zeros((pos.shape[0], rest), F32)], axis=-1)
    return cos_tab, sin_tab


def kernel(x_prompt, x_sample, cache_k, cache_v, page_table, a_norm_g, a_w_in, a_ln_g, a_ln_b, a_w_s, a_b_s, a_w_out, kv_norm_g, w_kv, k_norm_g, b_norm_g, b_w_q, q_norm_g, b_w_o, moe_norm_g, moe_w_rg, moe_b_rg, moe_w_re, moe_b_re, moe_w_gate, moe_w_up, moe_w_down):
    bsz, seq, _ = x_prompt.shape
    n_dec = x_sample.shape[0]
    assert x_sample.shape[1] == 1 and a_w_in.shape[0] == 1 and b_w_q.shape[0] == 1
    assert seq % ROW_TILE == 0 and ROW_TILE == MOBA_BLOCK
    page = cache_k.shape[1]
    past = page_table.shape[1] * page
    assert past % MOBA_BLOCK == 0 and past // MOBA_BLOCK >= MOBA_TOPK and MOBA_BLOCK % page == 0

    depth = moe_norm_g.shape[0]
    pad = LANES - N_EXPERT_GROUPS - N_EXPERTS
    mats = {
        "a_w_in": a_w_in[0], "a_w_out": a_w_out[0], "w_kv": w_kv, "b_w_q": b_w_q[0], "b_w_o": b_w_o[0],
        "moe_w_route": jnp.concatenate([moe_w_rg, moe_w_re, jnp.zeros((depth, D_MODEL, pad), F32)], axis=-1),
    }
    w = {
        "f32": mats, "bf16": {name: m.astype(BF16) for name, m in mats.items()},
        "moe_w_gate": moe_w_gate, "moe_w_up": moe_w_up, "moe_w_down": moe_w_down,
        "a_norm_g": a_norm_g[0][None], "a_ln_g": a_ln_g[0][None], "a_ln_b": a_ln_b[0][None],
        "a_w_s": a_w_s[0], "a_b_s_t": a_b_s[0].T,
        "a_ws0": jnp.repeat(a_w_s[0, :, 0, 0], GROUP_WIDTH)[None], "a_bs0": jnp.repeat(a_b_s[0, :, 0], GROUP_WIDTH)[None],
        "kv_norm_g": kv_norm_g[None], "k_norm_g": k_norm_g[None],
        "b_norm_g": b_norm_g[0][None], "q_norm_g": q_norm_g[0][None],
        "moe_norm_g": moe_norm_g[:, None, :],
        "moe_b_route": jnp.concatenate([moe_b_rg, moe_b_re, jnp.zeros((depth, pad), F32)], axis=-1)[:, None, :],
    }

    xp = x_prompt.reshape(bsz * seq, D_MODEL)
    xs = x_sample.reshape(n_dec, D_MODEL)
    cos_p, sin_p = _rope_tables(jnp.arange(seq, dtype=jnp.int32))
    cos_s, sin_s = _rope_tables(jnp.full((n_dec,), past, dtype=jnp.int32))

    x1p, vrows_p, hmp, route0p = _gmlp(xp, w, False, seq)
    x1s, vrows_s, hms, route0s = _gmlp(xs, w, True, seq)
    yp = _experts(hmp, route0p, w, 0, EXPERT_TILE, False)
    ys = _experts(hms, route0s, w, 0, EXPERT_TILE_SMALL, True)

    x2p, k_p, v_p, q_p, kb_p, vt_p, kmean_p = _kvq(x1p, yp, route0p, cos_p, sin_p, w, True, False)
    x2s, k_s, v_s, q_s = _kvq(x1s, ys, route0s, cos_s, sin_s, w, False, True)

    x3p, hmp, route1p = _attention_prompt(q_p, kb_p, vt_p, kmean_p, x2p, w, seq)
    kmean_s = _paged_kmean(cache_k, page_table)
    sel_blocks = _block_select(q_s.reshape(n_dec, N_HEADS, HEAD_DIM), kmean_s)[:, :, :MOBA_TOPK]
    pages_per_block = MOBA_BLOCK // page
    page_slots = sel_blocks[..., None] * pages_per_block + jnp.arange(pages_per_block, dtype=jnp.int32)
    page_slots = page_slots.reshape(n_dec, N_HEADS * MOBA_TOPK * pages_per_block)
    sel_pages = jnp.take_along_axis(page_table, page_slots, axis=1).reshape(n_dec, N_HEADS, MOBA_TOPK * pages_per_block)
    o_s = _attention_sample(q_s.reshape(n_dec, 1, D_MODEL), k_s.reshape(n_dec, 1, D_MODEL),
                            v_s.reshape(n_dec, 1, D_MODEL), cache_k, cache_v, sel_pages)
    x3s, hms, route1s = _oproj(o_s.reshape(n_dec, D_MODEL), x2s, w)
    yp = _experts(hmp, route1p, w, 1, EXPERT_TILE, False)
    ys = _experts(hms, route1s, w, 1, EXPERT_TILE_SMALL, True)
    y_prompt = _combine(x3p, yp, route1p)
    y_sample = _combine(x3s, ys, route1s)

    return (y_prompt.reshape(bsz, seq, D_MODEL), y_sample.reshape(n_dec, 1, D_MODEL),
            vrows_p.reshape(1, bsz, CHUNK, GATE_WIDTH), vrows_s.reshape(1, n_dec, 1, GATE_WIDTH),
            k_p.reshape(bsz, seq, N_HEADS, HEAD_DIM), v_p.reshape(bsz, seq, N_HEADS, HEAD_DIM),
            k_s.reshape(n_dec, 1, N_HEADS, HEAD_DIM), v_s.reshape(n_dec, 1, N_HEADS, HEAD_DIM))
```

```python
import functools

import jax
import jax.numpy as jnp
from jax import lax
from jax.experimental import pallas as pl
from jax.experimental.pallas import tpu as pltpu

D_MODEL = 1024
CHUNK = 128
GATE_WIDTH = 2 * D_MODEL
N_GATE_GROUPS = 8
GROUP_WIDTH = GATE_WIDTH // N_GATE_GROUPS
HEAD_DIM = 128
N_HEADS = D_MODEL // HEAD_DIM
ROT_DIM = HEAD_DIM // 4
ROT_HALF = ROT_DIM // 2
ROPE_THETA = 500000.0
MOBA_BLOCK = 256
MOBA_TOPK = 3
N_EXPERT_GROUPS = 4
EXPERTS_PER_GROUP = 4
N_EXPERTS = N_EXPERT_GROUPS * EXPERTS_PER_GROUP
EXPERT_TOPK = 2
EXPERT_FF = D_MODEL // 2
EPS = 1e-6

LANES = 128
SUBLANES = 8
ROW_PIECES = D_MODEL // LANES
ROW_TILE = 256
EXPERT_TILE = 256
EXPERT_TILE_SMALL = 16
IN_PROJ_COLS = 512
N_IN_PROJ_CHUNKS = 2 * GATE_WIDTH // IN_PROJ_COLS
PAGE_BATCH = 16
N_PAGE_BATCHES = 2
N_DMA_PRIORITIES = 2
VMEM_LIMIT = 60 * 1024 * 1024

F32 = jnp.float32
BF16 = jnp.bfloat16
NEG_INF = float("-inf")

assert ROW_PIECES == SUBLANES


def _params(n_axes):
    return pltpu.CompilerParams(dimension_semantics=("arbitrary",) * n_axes, vmem_limit_bytes=VMEM_LIMIT)


def _rms(x, g):
    return x * lax.rsqrt(jnp.mean(x * x, axis=-1, keepdims=True) + EPS) * g


def _mm(a, b, exact):
    if exact:
        return jnp.dot(a, b, precision=lax.Precision.HIGHEST, preferred_element_type=F32)
    return jnp.dot(a.astype(BF16), b, preferred_element_type=F32)


def _full(shape):
    nd = len(shape)
    return pl.BlockSpec(shape, lambda *_: (0,) * nd, pipeline_mode=pl.Buffered(1))


def _store_tile_rows(ref, x):
    for s in range(ROW_PIECES):
        ref[pl.ds(s, x.shape[0], stride=ROW_PIECES), :] = x[:, s * LANES:(s + 1) * LANES]


def _load_tile_rows(ref, rows, first=0, step=ROW_PIECES):
    return jnp.concatenate([ref[pl.ds(first + s, rows, stride=step), :] for s in range(ROW_PIECES)], axis=-1)


def _route(hm, wr_ref, br_ref, exact):
    logits = _mm(hm, wr_ref[...], exact) + br_ref[...]
    lane = lax.broadcasted_iota(jnp.int32, logits.shape, 1).astype(F32)
    lg = jnp.where(lane < N_EXPERT_GROUPS, logits, NEG_INF)
    mg = jnp.max(lg, axis=-1, keepdims=True)
    g_sel = jnp.min(jnp.where(lg == mg, lane, float(LANES)), axis=-1, keepdims=True)
    p_grp = 1.0 / jnp.sum(jnp.exp(lg - mg), axis=-1, keepdims=True)
    lo = N_EXPERT_GROUPS + g_sel * EXPERTS_PER_GROUP
    le = jnp.where((lane >= lo) & (lane < lo + EXPERTS_PER_GROUP), logits, NEG_INF)
    v1 = jnp.max(le, axis=-1, keepdims=True)
    i1 = jnp.min(jnp.where(le == v1, lane, float(LANES)), axis=-1, keepdims=True)
    le2 = jnp.where(lane == i1, NEG_INF, le)
    v2 = jnp.max(le2, axis=-1, keepdims=True)
    i2 = jnp.min(jnp.where(le2 == v2, lane, float(LANES)), axis=-1, keepdims=True)
    t = jnp.exp(v2 - v1)
    w1 = p_grp / (1.0 + t)
    w2 = p_grp * t / (1.0 + t)
    out = jnp.where(lane == 0.0, i1 - N_EXPERT_GROUPS, 0.0)
    out = jnp.where(lane == 1.0, i2 - N_EXPERT_GROUPS, out)
    out = jnp.where(lane == 2.0, w1, out)
    out = jnp.where(lane == 3.0, w2, out)
    return out


def _moe_prenorm_and_route(x, mg_ref, wr_ref, br_ref, hm_ref, route_ref, exact):
    hm = _rms(x, mg_ref[...])
    _store_tile_rows(hm_ref, hm)
    route_ref[...] = _route(hm, wr_ref, br_ref, exact)


def _moe_residual(x, y_ref, route_ref):
    rows = x.shape[0]
    step = EXPERT_TOPK * ROW_PIECES
    y1 = _load_tile_rows(y_ref, rows, 0, step)
    y2 = _load_tile_rows(y_ref, rows, ROW_PIECES, step)
    return x + (route_ref[:, 2:3] * y1 + route_ref[:, 3:4] * y2)


def _page_batch_copies(pt_ref, ck_hbm, pbuf, psem, step, k):
    n_pages = pt_ref.shape[1]
    first = (step * N_PAGE_BATCHES + k) * PAGE_BATCH
    seq, page0 = first // n_pages, first % n_pages
    return [pltpu.make_async_copy(ck_hbm.at[pt_ref[seq, page0 + p]], pbuf.at[k, p], psem.at[k])
            for p in range(PAGE_BATCH)]


def _reduce_page_batch(pbuf, km_ref, k, pages_per_block):
    blocks = PAGE_BATCH // pages_per_block
    for b in range(blocks):
        tot = jnp.sum(pbuf[k, b * pages_per_block], axis=0)
        for p in range(1, pages_per_block):
            tot = tot + jnp.sum(pbuf[k, b * pages_per_block + p], axis=0)
        km_ref[0, k * blocks + b] = tot / MOBA_BLOCK


def _gmlp_kernel(sample, tm, *refs):
    if sample:
        (x_ref, ng_ref, win_ref, lng_ref, lnb_ref, ws_ref, bs_ref, wout_ref, mg_ref, wr_ref, br_ref,
         x1_ref, vrows_ref, hm_ref, route_ref, u_scr, v_scr, gated_scr) = refs
    else:
        (pt_ref, x_ref, ng_ref, win_ref, lng_ref, lnb_ref, ws_ref, bs_ref, wout_ref, mg_ref, wr_ref, br_ref, ck_hbm,
         x1_ref, vrows_ref, hm_ref, route_ref, km_ref, u_scr, v_scr, gated_scr, pbuf, psem) = refs
        step = pl.program_id(0)
        batch = functools.partial(_page_batch_copies, pt_ref, ck_hbm, pbuf, psem)
        pages_per_block = MOBA_BLOCK // ck_hbm.shape[1]

        @pl.when(step == 0)
        def _():
            for cp in batch(step, 0):
                cp.start()

        for cp in batch(step, 1):
            cp.start()

    x = x_ref[...]
    h = _rms(x, ng_ref[...])
    if not sample:
        h = h.astype(BF16)
    n_col = N_IN_PROJ_CHUNKS
    n_ucol = GATE_WIDTH // IN_PROJ_COLS
    vsum = jnp.zeros((tm, 1), F32)
    for c in range(n_col):
        z = _mm(h, win_ref[:, c * IN_PROJ_COLS:(c + 1) * IN_PROJ_COLS], sample)
        z = jax.nn.gelu(z, approximate=True)
        if c < n_ucol:
            u_scr[:, c * IN_PROJ_COLS:(c + 1) * IN_PROJ_COLS] = z
        else:
            v_scr[:, (c - n_ucol) * IN_PROJ_COLS:(c - n_ucol + 1) * IN_PROJ_COLS] = z
            vsum = vsum + jnp.sum(z, axis=-1, keepdims=True)

    if not sample:
        for cp in batch(step, 0):
            cp.wait()
        _reduce_page_batch(pbuf, km_ref, 0, pages_per_block)

        @pl.when(step + 1 < pl.num_programs(0))
        def _():
            for cp in batch(step + 1, 0):
                cp.start()

    mean = vsum / GATE_WIDTH
    vss = jnp.zeros((tm, 1), F32)
    for c in range(n_ucol):
        xc = v_scr[:, c * IN_PROJ_COLS:(c + 1) * IN_PROJ_COLS] - mean
        vss = vss + jnp.sum(xc * xc, axis=-1, keepdims=True)
    rstd = lax.rsqrt(vss / GATE_WIDTH + EPS)
    for c in range(n_ucol):
        cs = slice(c * IN_PROJ_COLS, (c + 1) * IN_PROJ_COLS)
        v_scr[:, cs] = (v_scr[:, cs] - mean) * rstd * lng_ref[:, cs] + lnb_ref[:, cs]

    if sample:
        vrows_ref[...] = v_scr[...]
        gated_scr[...] = u_scr[...] * (v_scr[...] * ws_ref[...] + bs_ref[...])
    else:
        vrows_ref[0] = v_scr[tm - CHUNK:tm, :]
        t_out = lax.broadcasted_iota(jnp.int32, (CHUNK, CHUNK), 0)
        s_in = lax.broadcasted_iota(jnp.int32, (CHUNK, CHUNK), 1)
        for g in range(N_GATE_GROUPS):
            gs = slice(g * GROUP_WIDTH, (g + 1) * GROUP_WIDTH)
            wsg = jnp.where(s_in <= t_out, ws_ref[g], 0.0).astype(BF16)
            bias = bs_ref[:, g:g + 1]
            for c in range(tm // CHUNK):
                rs = slice(c * CHUNK, (c + 1) * CHUNK)
                mixed = jnp.dot(wsg, v_scr[rs, gs].astype(BF16), preferred_element_type=F32) + bias
                gated_scr[rs, gs] = (u_scr[rs, gs] * mixed).astype(BF16)

    x1 = x + _mm(gated_scr[...], wout_ref[...], sample)
    x1_ref[...] = x1
    if not sample:
        for cp in batch(step, 1):
            cp.wait()
        _reduce_page_batch(pbuf, km_ref, 1, pages_per_block)
    _moe_prenorm_and_route(x1, mg_ref, wr_ref, br_ref, hm_ref, route_ref, sample)


def _gmlp(x, w, sample, seq, cache_k=None, page_table=None):
    n = x.shape[0]
    mw = w["f32"] if sample else w["bf16"]
    tm = n if sample else ROW_TILE
    n_steps = n // tm
    row = lambda i, *_: (i, 0)
    in_specs = [pl.BlockSpec((tm, D_MODEL), row), _full((1, D_MODEL)), _full((D_MODEL, 2 * GATE_WIDTH)),
                _full((1, GATE_WIDTH)), _full((1, GATE_WIDTH)), None, None, _full((GATE_WIDTH, D_MODEL)),
                _full((1, D_MODEL)), _full((D_MODEL, LANES)), _full((1, LANES))]
    out_specs = [pl.BlockSpec((tm, D_MODEL), row), None, pl.BlockSpec((tm * ROW_PIECES, LANES), row),
                 pl.BlockSpec((tm, LANES), row)]
    out_shape = [jax.ShapeDtypeStruct((n, D_MODEL), F32), None,
                 jax.ShapeDtypeStruct((n * ROW_PIECES, LANES), F32), jax.ShapeDtypeStruct((n, LANES), F32)]
    scratch = [pltpu.VMEM((tm, GATE_WIDTH), F32), pltpu.VMEM((tm, GATE_WIDTH), F32),
               pltpu.VMEM((tm, GATE_WIDTH), F32 if sample else BF16)]
    args = [x, w["a_norm_g"], mw["a_w_in"], w["a_ln_g"], w["a_ln_b"],
            w["a_ws0"] if sample else w["a_w_s"], w["a_bs0"] if sample else w["a_b_s_t"], mw["a_w_out"],
            w["moe_norm_g"][0], mw["moe_w_route"][0], w["moe_b_route"][0]]
    if sample:
        in_specs[5], in_specs[6] = _full((1, GATE_WIDTH)), _full((1, GATE_WIDTH))
        out_shape[1] = jax.ShapeDtypeStruct((n, GATE_WIDTH), F32)
        out_specs[1] = pl.BlockSpec((tm, GATE_WIDTH), row)
        n_prefetch = 0
    else:
        in_specs[5], in_specs[6] = _full((N_GATE_GROUPS, CHUNK, CHUNK)), _full((CHUNK, N_GATE_GROUPS))
        tiles_per_seq = seq // tm
        out_shape[1] = jax.ShapeDtypeStruct((n // seq, CHUNK, GATE_WIDTH), F32)
        out_specs[1] = pl.BlockSpec((1, CHUNK, GATE_WIDTH), lambda i, *_: (i // tiles_per_seq, 0, 0))
        n_dec, n_pages = page_table.shape
        page = cache_k.shape[1]
        pages_per_step = N_PAGE_BATCHES * PAGE_BATCH
        assert n_dec * n_pages == n_steps * pages_per_step and n_pages % pages_per_step == 0
        blocks_per_step = pages_per_step * page // MOBA_BLOCK
        steps_per_seq = n_pages // pages_per_step
        in_specs.append(pl.BlockSpec(memory_space=pl.ANY))
        args = [page_table] + args + [cache_k]
        out_specs.append(pl.BlockSpec((1, blocks_per_step, N_HEADS, HEAD_DIM),
                                      lambda i, *_: (i // steps_per_seq, i % steps_per_seq, 0, 0)))
        out_shape.append(jax.ShapeDtypeStruct((n_dec, n_pages * page // MOBA_BLOCK, N_HEADS, HEAD_DIM), F32))
        scratch += [pltpu.VMEM((N_PAGE_BATCHES, PAGE_BATCH, page, N_HEADS, HEAD_DIM), F32),
                    pltpu.SemaphoreType.DMA((N_PAGE_BATCHES,))]
        n_prefetch = 1
    grid_spec = pltpu.PrefetchScalarGridSpec(num_scalar_prefetch=n_prefetch, grid=(n_steps,), in_specs=in_specs,
                                             out_specs=out_specs, scratch_shapes=scratch)
    return pl.pallas_call(
        functools.partial(_gmlp_kernel, sample, tm),
        grid_spec=grid_spec,
        out_shape=out_shape,
        compiler_params=_params(1),
        name="gmlp_sample" if sample else "gmlp_prompt",
    )(*args)


def _expert_kernel(tmg, n_tiles, exact, te_ref, nv_ref, ts_ref, od_ref, h_hbm, wg_ref, wu_ref, wd_ref, out_hbm,
                   xbuf, ybuf, sem_in, sem_out, *bf16_weights):
    i = pl.program_id(0)
    rows8 = SUBLANES

    def gather_row(r, slot, pair=0):
        tok8 = pl.multiple_of((pair >> 1) * rows8, rows8)
        return pltpu.make_async_copy(h_hbm.at[pl.ds(tok8, rows8)], xbuf.at[slot, pl.ds(r * rows8, rows8)],
                                     sem_in.at[slot])

    def scatter_row(r, slot, pair=0):
        pair8 = pl.multiple_of(pair * rows8, rows8)
        return pltpu.make_async_copy(ybuf.at[slot, pl.ds(pl.multiple_of(r * rows8, rows8), rows8)],
                                     out_hbm.at[pl.ds(pair8, rows8)], sem_out.at[slot])

    def for_scatter_rows(tile, slot, start):
        base, n_rows = ts_ref[tile], nv_ref[tile]

        def one(r, priority=0):
            if start:
                scatter_row(r, slot, od_ref[base + r]).start(priority=priority)
            else:
                scatter_row(r, slot).wait()

        @pl.when(n_rows == tmg)
        def _():
            for r in range(tmg):
                one(r, r % N_DMA_PRIORITIES)

        @pl.when(n_rows < tmg)
        def _():
            def body(r, c):
                one(r)
                return c
            lax.fori_loop(0, n_rows, body, 0)

    nxt = jnp.minimum(i, n_tiles - 1)

    @pl.when((i < n_tiles) & (nv_ref[nxt] > 0))
    def _():
        base = ts_ref[nxt]
        for r in range(tmg):
            gather_row(r, nxt % 2, od_ref[base + r]).start(priority=r % N_DMA_PRIORITIES)

    c = jnp.maximum(i - 1, 0)

    @pl.when((i >= 1) & (nv_ref[c] > 0))
    def _():
        slot = c % 2
        for r in range(tmg):
            gather_row(r, slot).wait()
        if exact:
            wg, wu, wd = wg_ref, wu_ref, wd_ref
        else:
            wg, wu, wd = bf16_weights

            @pl.when((c == 0) | (te_ref[c] != te_ref[jnp.maximum(c - 1, 0)]))
            def _():
                wg[...] = wg_ref[...].astype(BF16)
                wu[...] = wu_ref[...].astype(BF16)
                wd[...] = wd_ref[...].astype(BF16)

        x = _load_tile_rows(xbuf.at[slot], tmg)
        a = jax.nn.silu(_mm(x, wg[...], exact)) * _mm(x, wu[...], exact)
        y = _mm(a, wd[...], exact)

        @pl.when(c >= 2)
        def _():
            for_scatter_rows(c - 2, slot, False)

        _store_tile_rows(ybuf.at[slot], y)
        for_scatter_rows(c, slot, True)

        @pl.when((c == n_tiles - 1) | (nv_ref[jnp.minimum(c + 1, n_tiles - 1)] == 0))
        def _():
            @pl.when(c >= 1)
            def _():
                for_scatter_rows(c - 1, 1 - slot, False)
            for_scatter_rows(c, slot, False)


def _dispatch(route, tmg):
    n_pairs = EXPERT_TOPK * route.shape[0]
    e_flat = route[:, 0:EXPERT_TOPK].astype(jnp.int32).reshape(n_pairs)
    n_tiles = -(-(n_pairs + N_EXPERTS * (tmg - 1)) // tmg)
    order = jnp.argsort(e_flat, stable=True).astype(jnp.int32)
    counts = jnp.sum((e_flat[:, None] == jnp.arange(N_EXPERTS, dtype=jnp.int32)[None, :]).astype(jnp.int32), axis=0)
    tiles_per = -(-counts // tmg)
    t_end = jnp.cumsum(tiles_per)
    t_start = t_end - tiles_per
    c_start = jnp.cumsum(counts) - counts
    tile = jnp.arange(n_tiles, dtype=jnp.int32)
    tile_expert = jnp.minimum(jnp.sum((t_end[None, :] <= tile[:, None]).astype(jnp.int32), axis=1), N_EXPERTS - 1)
    first_row = (tile - t_start[tile_expert]) * tmg
    tile_valid = jnp.clip(counts[tile_expert] - first_row, 0, tmg).astype(jnp.int32)
    tile_src = jnp.where(tile_valid > 0, c_start[tile_expert] + first_row, 0).astype(jnp.int32)
    order = jnp.concatenate([order, jnp.zeros((tmg,), jnp.int32)])
    return tile_expert.astype(jnp.int32), tile_valid, tile_src, order


def _experts(hm, route, w, layer, tmg, exact):
    n = route.shape[0]
    w_gate, w_up, w_down = w["moe_w_gate"], w["moe_w_up"], w["moe_w_down"]
    tile_expert, tile_valid, tile_src, order = _dispatch(route, tmg)
    n_tiles = tile_expert.shape[0]
    by_expert = lambda i, te, nv, ts, od: (layer, te[jnp.maximum(i - 1, 0)], 0, 0)
    scratch = [pltpu.VMEM((2, tmg * ROW_PIECES, LANES), F32), pltpu.VMEM((2, tmg * ROW_PIECES, LANES), F32),
               pltpu.SemaphoreType.DMA((2,)), pltpu.SemaphoreType.DMA((2,))]
    if not exact:
        scratch += [pltpu.VMEM((D_MODEL, EXPERT_FF), BF16), pltpu.VMEM((D_MODEL, EXPERT_FF), BF16),
                    pltpu.VMEM((EXPERT_FF, D_MODEL), BF16)]
    grid_spec = pltpu.PrefetchScalarGridSpec(
        num_scalar_prefetch=4,
        grid=(n_tiles + 1,),
        in_specs=[pl.BlockSpec(memory_space=pl.ANY),
                  pl.BlockSpec((None, None, D_MODEL, EXPERT_FF), by_expert),
                  pl.BlockSpec((None, None, D_MODEL, EXPERT_FF), by_expert),
                  pl.BlockSpec((None, None, EXPERT_FF, D_MODEL), by_expert)],
        out_specs=pl.BlockSpec(memory_space=pl.ANY),
        scratch_shapes=scratch,
    )
    return pl.pallas_call(
        functools.partial(_expert_kernel, tmg, n_tiles, exact),
        grid_spec=grid_spec,
        out_shape=jax.ShapeDtypeStruct((EXPERT_TOPK * n * ROW_PIECES, LANES), F32),
        compiler_params=_params(1),
        name="experts",
    )(tile_expert, tile_valid, tile_src, order, hm, w_gate, w_up, w_down)


def _rope(xh, cos, sin, lane):
    swapped = jnp.where(lane < ROT_HALF, pltpu.roll(xh, LANES - ROT_HALF, 1), pltpu.roll(xh, ROT_HALF, 1))
    return xh * cos + swapped * sin


def _kvq_kernel(prompt, exact, x1_ref, yp_ref, route_ref, kvg_ref, wkv_ref, kg_ref, cos_ref, sin_ref,
                bng_ref, wq_ref, qg_ref, x2_ref, k_ref, v_ref, q_ref, *prompt_refs):
    x2 = _moe_residual(x1_ref[...], yp_ref, route_ref)
    x2_ref[...] = x2
    cos = cos_ref[...]
    sin = sin_ref[...]
    lane = lax.broadcasted_iota(jnp.int32, cos.shape, 1)
    kv = _mm(_rms(x2, kvg_ref[...]), wkv_ref[...], exact)
    for h in range(N_HEADS):
        hs = slice(h * HEAD_DIM, (h + 1) * HEAD_DIM)
        kh = _rope(_rms(kv[:, hs], kg_ref[...]), cos, sin, lane)
        k_ref[:, hs] = kh
        if prompt:
            kb_ref, _, km_ref = prompt_refs
            kb_ref[:, hs] = kh.astype(BF16)
            km_ref[0, :, hs] = jnp.mean(kh, axis=0, keepdims=True)
    v = kv[:, D_MODEL:]
    v_ref[...] = v
    if prompt:
        prompt_refs[1][0] = v.T.astype(BF16)
    q = _mm(_rms(x2, bng_ref[...]), wq_ref[...], exact)
    for h in range(N_HEADS):
        hs = slice(h * HEAD_DIM, (h + 1) * HEAD_DIM)
        q_ref[:, hs] = _rope(_rms(q[:, hs], qg_ref[...]), cos, sin, lane)


def _kvq(x1, ypairs, route, cos_tab, sin_tab, w, prompt, exact):
    n = x1.shape[0]
    mw = w["f32"] if exact else w["bf16"]
    tm = ROW_TILE if prompt else n
    row = lambda i: (i, 0)
    tab_tiles = cos_tab.shape[0] // tm
    tab = lambda i: (i % tab_tiles, 0)
    out_specs = [pl.BlockSpec((tm, D_MODEL), row)] * 4
    out_shape = [jax.ShapeDtypeStruct((n, D_MODEL), F32)] * 4
    if prompt:
        seq = cos_tab.shape[0]
        out_specs += [pl.BlockSpec((tm, D_MODEL), row),
                      pl.BlockSpec((1, D_MODEL, tm), lambda i: (i // tab_tiles, 0, i % tab_tiles)),
                      pl.BlockSpec((1, 1, D_MODEL), lambda i: (i, 0, 0))]
        out_shape += [jax.ShapeDtypeStruct((n, D_MODEL), BF16), jax.ShapeDtypeStruct((n // seq, D_MODEL, seq), BF16),
                      jax.ShapeDtypeStruct((n // tm, 1, D_MODEL), F32)]
    return pl.pallas_call(
        functools.partial(_kvq_kernel, prompt, exact),
        grid=(n // tm,),
        in_specs=[pl.BlockSpec((tm, D_MODEL), row), pl.BlockSpec((EXPERT_TOPK * tm * ROW_PIECES, LANES), row),
                  pl.BlockSpec((tm, LANES), row),
                  _full((1, D_MODEL)), _full((D_MODEL, 2 * D_MODEL)), _full((1, HEAD_DIM)),
                  pl.BlockSpec((tm, HEAD_DIM), tab), pl.BlockSpec((tm, HEAD_DIM), tab),
                  _full((1, D_MODEL)), _full((D_MODEL, D_MODEL)), _full((1, HEAD_DIM))],
        out_specs=out_specs,
        out_shape=out_shape,
        compiler_params=_params(1),
        name="kvq",
    )(x1, ypairs, route, w["kv_norm_g"], mw["w_kv"], w["k_norm_g"], cos_tab, sin_tab,
      w["b_norm_g"], mw["b_w_q"], w["q_norm_g"])


def _attn_kernel(q_ref, kb_ref, vt_ref, km_ref, x2_ref, wo_ref, mg_ref, wr_ref, br_ref,
                 x3_ref, hm_ref, route_ref, qb_scr, sel_scr, o_scr, *acc_scrs):
    j = pl.program_id(1)
    tq = q_ref.shape[0]
    n_blk = km_ref.shape[1]
    scale = HEAD_DIM ** -0.5
    blk = lax.broadcasted_iota(jnp.int32, (n_blk, tq), 0).astype(F32)
    cur = j.astype(F32)
    contract_last = (((1,), (1,)), ((), ()))
    own = pl.multiple_of(j * MOBA_BLOCK, MOBA_BLOCK)
    qb_scr[...] = q_ref[...].astype(BF16)
    chains = [(h, g) for h in range(N_HEADS) for g in range(tq // LANES)]

    def scores(h, g, off):
        hs = slice(h * HEAD_DIM, (h + 1) * HEAD_DIM)
        return lax.dot_general(kb_ref[pl.ds(off, MOBA_BLOCK), hs], qb_scr[g * LANES:(g + 1) * LANES, hs],
                               contract_last, preferred_element_type=F32) * scale

    def weighted_values(h, off, p):
        hs = slice(h * HEAD_DIM, (h + 1) * HEAD_DIM)
        return jnp.dot(vt_ref[0, hs, pl.ds(off, MOBA_BLOCK)], p.astype(BF16), preferred_element_type=F32)

    for h in range(N_HEADS):
        hs = slice(h * HEAD_DIM, (h + 1) * HEAD_DIM)
        gate = lax.dot_general(km_ref[0, :, hs].astype(BF16), qb_scr[:, hs], contract_last,
                               preferred_element_type=F32)
        gate = jnp.where(blk < cur, gate, NEG_INF)
        sel = jnp.zeros((n_blk, tq), F32)
        for _ in range(MOBA_TOPK):
            m = jnp.max(gate, axis=0, keepdims=True)
            first = jnp.min(jnp.where(gate == m, blk, float(n_blk)), axis=0, keepdims=True)
            pick = (blk == first) & (m > NEG_INF)
            sel = jnp.where(pick, 1.0, sel)
            gate = jnp.where(pick, NEG_INF, gate)
        sel_scr[h] = sel

    k_pos = lax.broadcasted_iota(jnp.int32, (MOBA_BLOCK, LANES), 0)
    q_pos = lax.broadcasted_iota(jnp.int32, (MOBA_BLOCK, LANES), 1)
    stats = []
    for h, g in chains:
        s = jnp.where(k_pos <= q_pos + g * LANES, scores(h, g, own), NEG_INF)
        m0 = jnp.max(s, axis=0, keepdims=True)
        p = jnp.exp(s - m0)
        stats += [m0, jnp.sum(p, axis=0, keepdims=True)]
        acc_scrs[h][:, g * LANES:(g + 1) * LANES] = weighted_values(h, own, p)

    def past_block(jj, stats):
        off = pl.multiple_of(jj * MOBA_BLOCK, MOBA_BLOCK)
        new_stats = []
        visible = [sel_scr[h, pl.ds(jj, 1), :] for h in range(N_HEADS)]
        for c, (h, g) in enumerate(chains):
            qs = slice(g * LANES, (g + 1) * LANES)
            m_old, l_old = stats[2 * c], stats[2 * c + 1]
            s = jnp.where(visible[h][:, qs] > 0.0, scores(h, g, off), NEG_INF)
            m_new = jnp.maximum(m_old, jnp.max(s, axis=0, keepdims=True))
            alpha = jnp.exp(m_old - m_new)
            p = jnp.exp(s - m_new)
            new_stats += [m_new, alpha * l_old + jnp.sum(p, axis=0, keepdims=True)]
            acc_scrs[h][:, qs] = alpha * acc_scrs[h][:, qs] + weighted_values(h, off, p)
        return tuple(new_stats)

    stats = lax.fori_loop(0, j, past_block, tuple(stats))

    for c, (h, g) in enumerate(chains):
        qs = slice(g * LANES, (g + 1) * LANES)
        o_scr[qs, h * HEAD_DIM:(h + 1) * HEAD_DIM] = (acc_scrs[h][:, qs] / stats[2 * c + 1]).T.astype(BF16)

    x3 = x2_ref[...] + jnp.dot(o_scr[...], wo_ref[...], preferred_element_type=F32)
    x3_ref[...] = x3
    _moe_prenorm_and_route(x3, mg_ref, wr_ref, br_ref, hm_ref, route_ref, False)


def _attention_prompt(q, kb, vt, kmean, x2, w, seq):
    n = q.shape[0]
    bsz = n // seq
    n_qb = seq // MOBA_BLOCK
    row = lambda b, j: (b * n_qb + j, 0)
    return pl.pallas_call(
        _attn_kernel,
        grid=(bsz, n_qb),
        in_specs=[pl.BlockSpec((MOBA_BLOCK, D_MODEL), row),
                  pl.BlockSpec((seq, D_MODEL), lambda b, j: (b, 0)),
                  pl.BlockSpec((1, D_MODEL, seq), lambda b, j: (b, 0, 0)),
                  pl.BlockSpec((1, n_qb, D_MODEL), lambda b, j: (b, 0, 0)),
                  pl.BlockSpec((MOBA_BLOCK, D_MODEL), row),
                  _full((D_MODEL, D_MODEL)), _full((1, D_MODEL)), _full((D_MODEL, LANES)), _full((1, LANES))],
        out_specs=[pl.BlockSpec((MOBA_BLOCK, D_MODEL), row), pl.BlockSpec((MOBA_BLOCK * ROW_PIECES, LANES), row),
                   pl.BlockSpec((MOBA_BLOCK, LANES), row)],
        out_shape=[jax.ShapeDtypeStruct((n, D_MODEL), F32), jax.ShapeDtypeStruct((n * ROW_PIECES, LANES), F32),
                   jax.ShapeDtypeStruct((n, LANES), F32)],
        scratch_shapes=[pltpu.VMEM((MOBA_BLOCK, D_MODEL), BF16),
                        pltpu.VMEM((N_HEADS, n_qb, MOBA_BLOCK), F32),
                        pltpu.VMEM((MOBA_BLOCK, D_MODEL), BF16)]
                       + [pltpu.VMEM((HEAD_DIM, MOBA_BLOCK), F32)] * N_HEADS,
        compiler_params=_params(2),
        name="moba_prompt",
    )(q, kb, vt, kmean.reshape(bsz, n_qb, D_MODEL), x2, w["bf16"]["b_w_o"],
      w["moe_norm_g"][1], w["bf16"]["moe_w_route"][1], w["moe_b_route"][1])


def _block_select_kernel(q_ref, km_ref, sel_ref):
    gate = jnp.sum(km_ref[0] * q_ref[...], axis=-1, keepdims=True)
    n_blk = gate.shape[0]
    blk = lax.broadcasted_iota(jnp.int32, gate.shape, 0).astype(F32)
    rank = lax.broadcasted_iota(jnp.int32, sel_ref.shape[1:], 1)
    out = jnp.zeros(sel_ref.shape[1:], F32)
    for r in range(MOBA_TOPK):
        m = jnp.max(gate, axis=0, keepdims=True)
        first = jnp.min(jnp.where(gate == m, blk, float(n_blk)), axis=0, keepdims=True)
        gate = jnp.where(blk == first, NEG_INF, gate)
        out = jnp.where(rank == r, first[0], out)
    sel_ref[0] = out.astype(jnp.int32)


def _block_select(q4, kmean_s):
    n_seq, n_blk = kmean_s.shape[:2]
    return pl.pallas_call(
        _block_select_kernel,
        grid=(n_seq,),
        in_specs=[pl.BlockSpec((1, N_HEADS, HEAD_DIM), lambda s: (s, 0, 0)),
                  pl.BlockSpec((1, n_blk, N_HEADS, HEAD_DIM), lambda s: (s, 0, 0, 0))],
        out_specs=pl.BlockSpec((1, N_HEADS, LANES), lambda s: (s, 0, 0)),
        out_shape=jax.ShapeDtypeStruct((n_seq, N_HEADS, LANES), jnp.int32),
        compiler_params=_params(1),
        name="block_select",
    )(q4, kmean_s)


def _attn_sample_kernel(n_sel, pg_ref, q_ref, k_own_ref, v_own_ref, ck_hbm, cv_hbm, o_ref, kbuf, vbuf, sem):
    s = pl.program_id(0)

    def page_copies(seq, slot):
        cps = []
        for h in range(N_HEADS):
            for i in range(n_sel):
                pg = pg_ref[(seq * N_HEADS + h) * n_sel + i]
                cps.append(pltpu.make_async_copy(ck_hbm.at[pg, :, h, :], kbuf.at[slot, h, i], sem.at[slot]))
                cps.append(pltpu.make_async_copy(cv_hbm.at[pg, :, h, :], vbuf.at[slot, h, i], sem.at[slot]))
        return cps

    @pl.when(s == 0)
    def _():
        for cp in page_copies(0, 0):
            cp.start()

    @pl.when(s + 1 < pl.num_programs(0))
    def _():
        for cp in page_copies(s + 1, (s + 1) % 2):
            cp.start()

    slot = s % 2
    for cp in page_copies(s, slot):
        cp.wait()

    scale = HEAD_DIM ** -0.5
    for h in range(N_HEADS):
        hs = slice(h * HEAD_DIM, (h + 1) * HEAD_DIM)
        q = q_ref[0, :, hs]
        s_own = jnp.sum(k_own_ref[0, :, hs] * q, axis=-1, keepdims=True) * scale
        scores = [jnp.sum(kbuf[slot, h, i] * q, axis=-1, keepdims=True) * scale for i in range(n_sel)]
        m = s_own
        for sc in scores:
            m = jnp.maximum(m, jnp.max(sc, axis=0, keepdims=True))
        p_own = jnp.exp(s_own - m)
        l = p_own
        acc = p_own * v_own_ref[0, :, hs]
        for i, sc in enumerate(scores):
            p = jnp.exp(sc - m)
            l = l + jnp.sum(p, axis=0, keepdims=True)
            acc = acc + jnp.sum(p * vbuf[slot, h, i], axis=0, keepdims=True)
        o_ref[0, :, hs] = acc / l


def _attention_sample(q3, k3, v3, cache_k, cache_v, sel_pages):
    n_seq = q3.shape[0]
    page = cache_k.shape[1]
    n_sel = sel_pages.shape[-1]
    seq_row = lambda s, pg: (s, 0, 0)
    grid_spec = pltpu.PrefetchScalarGridSpec(
        num_scalar_prefetch=1,
        grid=(n_seq,),
        in_specs=[pl.BlockSpec((1, 1, D_MODEL), seq_row)] * 3
                 + [pl.BlockSpec(memory_space=pl.ANY), pl.BlockSpec(memory_space=pl.ANY)],
        out_specs=pl.BlockSpec((1, 1, D_MODEL), seq_row),
        scratch_shapes=[pltpu.VMEM((2, N_HEADS, n_sel, page, HEAD_DIM), F32),
                        pltpu.VMEM((2, N_HEADS, n_sel, page, HEAD_DIM), F32),
                        pltpu.SemaphoreType.DMA((2,))],
    )
    return pl.pallas_call(
        functools.partial(_attn_sample_kernel, n_sel),
        grid_spec=grid_spec,
        out_shape=jax.ShapeDtypeStruct((n_seq, 1, D_MODEL), F32),
        compiler_params=_params(1),
        name="moba_sample",
    )(sel_pages.reshape(-1), q3, k3, v3, cache_k, cache_v)


def _oproj_kernel(o_ref, x2_ref, wo_ref, mg_ref, wr_ref, br_ref, x3_ref, hm_ref, route_ref):
    x3 = x2_ref[...] + _mm(o_ref[...], wo_ref[...], True)
    x3_ref[...] = x3
    _moe_prenorm_and_route(x3, mg_ref, wr_ref, br_ref, hm_ref, route_ref, True)


def _oproj(o, x2, w):
    n = o.shape[0]
    return pl.pallas_call(
        _oproj_kernel,
        grid=(1,),
        in_specs=[_full((n, D_MODEL)), _full((n, D_MODEL)), _full((D_MODEL, D_MODEL)), _full((1, D_MODEL)),
                  _full((D_MODEL, LANES)), _full((1, LANES))],
        out_specs=[_full((n, D_MODEL)), _full((n * ROW_PIECES, LANES)), _full((n, LANES))],
        out_shape=[jax.ShapeDtypeStruct((n, D_MODEL), F32), jax.ShapeDtypeStruct((n * ROW_PIECES, LANES), F32),
                   jax.ShapeDtypeStruct((n, LANES), F32)],
        compiler_params=_params(1),
        name="oproj_sample",
    )(o, x2, w["f32"]["b_w_o"], w["moe_norm_g"][1], w["f32"]["moe_w_route"][1], w["moe_b_route"][1])


def _combine_kernel(x_ref, yp_ref, route_ref, o_ref):
    o_ref[...] = _moe_residual(x_ref[...], yp_ref, route_ref)


def _combine(x, ypairs, route):
    n = x.shape[0]
    tm = min(n, ROW_TILE)
    row = lambda i: (i, 0)
    return pl.pallas_call(
        _combine_kernel,
        grid=(n // tm,),
        in_specs=[pl.BlockSpec((tm, D_MODEL), row), pl.BlockSpec((EXPERT_TOPK * tm * ROW_PIECES, LANES), row),
                  pl.BlockSpec((tm, LANES), row)],
        out_specs=pl.BlockSpec((tm, D_MODEL), row),
        out_shape=jax.ShapeDtypeStruct((n, D_MODEL), F32),
        compiler_params=_params(1),
        name="combine",
    )(x, ypairs, route)


def _rope_tables(pos):
    inv_freq = ROPE_THETA ** (-jnp.arange(ROT_HALF, dtype=F32) / ROT_HALF)
    ang = pos.astype(F32)[:, None] * inv_freq[None, :]
    cos, sin = jnp.cos(ang), jnp.sin(ang)
    rest = HEAD_DIM - ROT_DIM
    cos_tab = jnp.concatenate([cos, cos, jnp.ones((pos.shape[0], rest), F32)], axis=-1)
    sin_tab = jnp.concatenate([-sin, sin, jnp.zeros((pos.shape[0], rest), F32)], axis=-1)
    return cos_tab, sin_tab


def kernel(x_prompt, x_sample, cache_k, cache_v, page_table, a_norm_g, a_w_in, a_ln_g, a_ln_b, a_w_s, a_b_s, a_w_out, kv_norm_g, w_kv, k_norm_g, b_norm_g, b_w_q, q_norm_g, b_w_o, moe_norm_g, moe_w_rg, moe_b_rg, moe_w_re, moe_b_re, moe_w_gate, moe_w_up, moe_w_down):
    bsz, seq, _ = x_prompt.shape
    n_dec = x_sample.shape[0]
    assert x_sample.shape[1] == 1 and a_w_in.shape[0] == 1 and b_w_q.shape[0] == 1
    assert seq % ROW_TILE == 0 and ROW_TILE == MOBA_BLOCK
    page = cache_k.shape[1]
    past = page_table.shape[1] * page
    assert past % MOBA_BLOCK == 0 and past // MOBA_BLOCK >= MOBA_TOPK and MOBA_BLOCK % page == 0

    depth = moe_norm_g.shape[0]
    pad = LANES - N_EXPERT_GROUPS - N_EXPERTS
    mats = {
        "a_w_in": a_w_in[0], "a_w_out": a_w_out[0], "w_kv": w_kv, "b_w_q": b_w_q[0], "b_w_o": b_w_o[0],
        "moe_w_route": jnp.concatenate([moe_w_rg, moe_w_re, jnp.zeros((depth, D_MODEL, pad), F32)], axis=-1),
    }
    w = {
        "f32": mats, "bf16": {name: m.astype(BF16) for name, m in mats.items()},
        "moe_w_gate": moe_w_gate, "moe_w_up": moe_w_up, "moe_w_down": moe_w_down,
        "a_norm_g": a_norm_g[0][None], "a_ln_g": a_ln_g[0][None], "a_ln_b": a_ln_b[0][None],
        "a_w_s": a_w_s[0], "a_b_s_t": a_b_s[0].T,
        "a_ws0": jnp.repeat(a_w_s[0, :, 0, 0], GROUP_WIDTH)[None], "a_bs0": jnp.repeat(a_b_s[0, :, 0], GROUP_WIDTH)[None],
        "kv_norm_g": kv_norm_g[None], "k_norm_g": k_norm_g[None],
        "b_norm_g": b_norm_g[0][None], "q_norm_g": q_norm_g[0][None],
        "moe_norm_g": moe_norm_g[:, None, :],
        "moe_b_route": jnp.concatenate([moe_b_rg, moe_b_re, jnp.zeros((depth, pad), F32)], axis=-1)[:, None, :],
    }

    xp = x_prompt.reshape(bsz * seq, D_MODEL)
    xs = x_sample.reshape(n_dec, D_MODEL)
    cos_p, sin_p = _rope_tables(jnp.arange(seq, dtype=jnp.int32))
    cos_s, sin_s = _rope_tables(jnp.full((n_dec,), past, dtype=jnp.int32))

    x1p, vrows_p, hmp, route0p, kmean_s = _gmlp(xp, w, False, seq, cache_k, page_table)
    x1s, vrows_s, hms, route0s = _gmlp(xs, w, True, seq)
    yp = _experts(hmp, route0p, w, 0, EXPERT_TILE, False)
    ys = _experts(hms, route0s, w, 0, EXPERT_TILE_SMALL, True)

    x2p, k_p, v_p, q_p, kb_p, vt_p, kmean_p = _kvq(x1p, yp, route0p, cos_p, sin_p, w, True, False)
    x2s, k_s, v_s, q_s = _kvq(x1s, ys, route0s, cos_s, sin_s, w, False, True)

    x3p, hmp, route1p = _attention_prompt(q_p, kb_p, vt_p, kmean_p, x2p, w, seq)
    sel_blocks = _block_select(q_s.reshape(n_dec, N_HEADS, HEAD_DIM), kmean_s)[:, :, :MOBA_TOPK]
    pages_per_block = MOBA_BLOCK // page
    page_slots = sel_blocks[..., None] * pages_per_block + jnp.arange(pages_per_block, dtype=jnp.int32)
    page_slots = page_slots.reshape(n_dec, N_HEADS * MOBA_TOPK * pages_per_block)
    sel_pages = jnp.take_along_axis(page_table, page_slots, axis=1).reshape(n_dec, N_HEADS, MOBA_TOPK * pages_per_block)
    o_s = _attention_sample(q_s.reshape(n_dec, 1, D_MODEL), k_s.reshape(n_dec, 1, D_MODEL),
                            v_s.reshape(n_dec, 1, D_MODEL), cache_k, cache_v, sel_pages)
    x3s, hms, route1s = _oproj(o_s.reshape(n_dec, D_MODEL), x2s, w)
    yp = _experts(hmp, route1p, w, 1, EXPERT_TILE, False)
    ys = _experts(hms, route1s, w, 1, EXPERT_TILE_SMALL, True)
    y_prompt = _combine(x3p, yp, route1p)
    y_sample = _combine(x3s, ys, route1s)

    return (y_prompt.reshape(bsz, seq, D_MODEL), y_sample.reshape(n_dec, 1, D_MODEL),
            vrows_p.reshape(1, bsz, CHUNK, GATE_WIDTH), vrows_s.reshape(1, n_dec, 1, GATE_WIDTH),
            k_p.reshape(bsz, seq, N_HEADS, HEAD_DIM), v_p.reshape(bsz, seq, N_HEADS, HEAD_DIM),
            k_s.reshape(n_dec, 1, N_HEADS, HEAD_DIM), v_s.reshape(n_dec, 1, N_HEADS, HEAD_DIM))
```

```python
import functools

import jax
import jax.numpy as jnp
from jax import lax
from jax.experimental import pallas as pl
from jax.experimental.pallas import tpu as pltpu

D_MODEL = 1024
CHUNK = 128
GATE_WIDTH = 2 * D_MODEL
N_GATE_GROUPS = 8
GROUP_WIDTH = GATE_WIDTH // N_GATE_GROUPS
HEAD_DIM = 128
N_HEADS = D_MODEL // HEAD_DIM
ROT_DIM = HEAD_DIM // 4
ROT_HALF = ROT_DIM // 2
ROPE_THETA = 500000.0
MOBA_BLOCK = 256
MOBA_TOPK = 3
N_EXPERT_GROUPS = 4
EXPERTS_PER_GROUP = 4
N_EXPERTS = N_EXPERT_GROUPS * EXPERTS_PER_GROUP
EXPERT_TOPK = 2
EXPERT_FF = D_MODEL // 2
EPS = 1e-6

LANES = 128
SUBLANES = 8
ROW_PIECES = D_MODEL // LANES
ROW_TILE = 256
EXPERT_TILE = 256
EXPERT_TILE_SMALL = 16
IN_PROJ_COLS = 512
N_IN_PROJ_CHUNKS = 2 * GATE_WIDTH // IN_PROJ_COLS
PAGE_BATCH = 16
N_PAGE_BATCHES = 2
VMEM_LIMIT = 60 * 1024 * 1024

F32 = jnp.float32
BF16 = jnp.bfloat16
NEG_INF = float("-inf")

assert ROW_PIECES == SUBLANES


def _params(n_axes):
    return pltpu.CompilerParams(dimension_semantics=("arbitrary",) * n_axes, vmem_limit_bytes=VMEM_LIMIT)


def _rms(x, g):
    return x * lax.rsqrt(jnp.mean(x * x, axis=-1, keepdims=True) + EPS) * g


def _mm(a, b, exact):
    if exact:
        return jnp.dot(a, b, precision=lax.Precision.HIGHEST, preferred_element_type=F32)
    return jnp.dot(a.astype(BF16), b, preferred_element_type=F32)


def _full(shape):
    nd = len(shape)
    return pl.BlockSpec(shape, lambda *_: (0,) * nd, pipeline_mode=pl.Buffered(1))


def _store_tile_rows(ref, x):
    for s in range(ROW_PIECES):
        ref[pl.ds(s, x.shape[0], stride=ROW_PIECES), :] = x[:, s * LANES:(s + 1) * LANES]


def _load_tile_rows(ref, rows, first=0, step=ROW_PIECES):
    return jnp.concatenate([ref[pl.ds(first + s, rows, stride=step), :] for s in range(ROW_PIECES)], axis=-1)


def _route(hm, wr_ref, br_ref, exact):
    logits = _mm(hm, wr_ref[...], exact) + br_ref[...]
    lane = lax.broadcasted_iota(jnp.int32, logits.shape, 1).astype(F32)
    lg = jnp.where(lane < N_EXPERT_GROUPS, logits, NEG_INF)
    mg = jnp.max(lg, axis=-1, keepdims=True)
    g_sel = jnp.min(jnp.where(lg == mg, lane, float(LANES)), axis=-1, keepdims=True)
    p_grp = 1.0 / jnp.sum(jnp.exp(lg - mg), axis=-1, keepdims=True)
    lo = N_EXPERT_GROUPS + g_sel * EXPERTS_PER_GROUP
    le = jnp.where((lane >= lo) & (lane < lo + EXPERTS_PER_GROUP), logits, NEG_INF)
    v1 = jnp.max(le, axis=-1, keepdims=True)
    i1 = jnp.min(jnp.where(le == v1, lane, float(LANES)), axis=-1, keepdims=True)
    le2 = jnp.where(lane == i1, NEG_INF, le)
    v2 = jnp.max(le2, axis=-1, keepdims=True)
    i2 = jnp.min(jnp.where(le2 == v2, lane, float(LANES)), axis=-1, keepdims=True)
    t = jnp.exp(v2 - v1)
    w1 = p_grp / (1.0 + t)
    w2 = p_grp * t / (1.0 + t)
    out = jnp.where(lane == 0.0, i1 - N_EXPERT_GROUPS, 0.0)
    out = jnp.where(lane == 1.0, i2 - N_EXPERT_GROUPS, out)
    out = jnp.where(lane == 2.0, w1, out)
    out = jnp.where(lane == 3.0, w2, out)
    return out


def _moe_prenorm_and_route(x, mg_ref, wr_ref, br_ref, hm_ref, route_ref, exact):
    hm = _rms(x, mg_ref[...])
    _store_tile_rows(hm_ref, hm)
    route_ref[...] = _route(hm, wr_ref, br_ref, exact)


def _moe_residual(x, y_ref, route_ref):
    rows = x.shape[0]
    step = EXPERT_TOPK * ROW_PIECES
    y1 = _load_tile_rows(y_ref, rows, 0, step)
    y2 = _load_tile_rows(y_ref, rows, ROW_PIECES, step)
    return x + (route_ref[:, 2:3] * y1 + route_ref[:, 3:4] * y2)


def _page_batch_copies(pt_ref, ck_hbm, pbuf, psem, step, k):
    n_pages = pt_ref.shape[1]
    first = (step * N_PAGE_BATCHES + k) * PAGE_BATCH
    seq, page0 = first // n_pages, first % n_pages
    return [pltpu.make_async_copy(ck_hbm.at[pt_ref[seq, page0 + p]], pbuf.at[k, p], psem.at[k])
            for p in range(PAGE_BATCH)]


def _reduce_page_batch(pbuf, km_ref, k, pages_per_block):
    blocks = PAGE_BATCH // pages_per_block
    for b in range(blocks):
        tot = jnp.sum(pbuf[k, b * pages_per_block], axis=0)
        for p in range(1, pages_per_block):
            tot = tot + jnp.sum(pbuf[k, b * pages_per_block + p], axis=0)
        km_ref[0, k * blocks + b] = tot / MOBA_BLOCK


def _gmlp_kernel(sample, tm, *refs):
    if sample:
        (x_ref, ng_ref, win_ref, lng_ref, lnb_ref, ws_ref, bs_ref, wout_ref, mg_ref, wr_ref, br_ref,
         x1_ref, vrows_ref, hm_ref, route_ref, u_scr, v_scr, gated_scr) = refs
    else:
        (pt_ref, x_ref, ng_ref, win_ref, lng_ref, lnb_ref, ws_ref, bs_ref, wout_ref, mg_ref, wr_ref, br_ref, ck_hbm,
         x1_ref, vrows_ref, hm_ref, route_ref, km_ref, u_scr, v_scr, gated_scr, pbuf, psem) = refs
        step = pl.program_id(0)
        batch = functools.partial(_page_batch_copies, pt_ref, ck_hbm, pbuf, psem)
        pages_per_block = MOBA_BLOCK // ck_hbm.shape[1]

        @pl.when(step == 0)
        def _():
            for cp in batch(step, 0):
                cp.start()

        for cp in batch(step, 1):
            cp.start()

    x = x_ref[...]
    h = _rms(x, ng_ref[...])
    if not sample:
        h = h.astype(BF16)
    n_col = N_IN_PROJ_CHUNKS
    n_ucol = GATE_WIDTH // IN_PROJ_COLS
    vsum = jnp.zeros((tm, 1), F32)
    for c in range(n_col):
        z = _mm(h, win_ref[:, c * IN_PROJ_COLS:(c + 1) * IN_PROJ_COLS], sample)
        z = jax.nn.gelu(z, approximate=True)
        if c < n_ucol:
            u_scr[:, c * IN_PROJ_COLS:(c + 1) * IN_PROJ_COLS] = z
        else:
            v_scr[:, (c - n_ucol) * IN_PROJ_COLS:(c - n_ucol + 1) * IN_PROJ_COLS] = z
            vsum = vsum + jnp.sum(z, axis=-1, keepdims=True)

    if not sample:
        for cp in batch(step, 0):
            cp.wait()
        _reduce_page_batch(pbuf, km_ref, 0, pages_per_block)

        @pl.when(step + 1 < pl.num_programs(0))
        def _():
            for cp in batch(step + 1, 0):
                cp.start()

    mean = vsum / GATE_WIDTH
    vss = jnp.zeros((tm, 1), F32)
    for c in range(n_ucol):
        xc = v_scr[:, c * IN_PROJ_COLS:(c + 1) * IN_PROJ_COLS] - mean
        vss = vss + jnp.sum(xc * xc, axis=-1, keepdims=True)
    rstd = lax.rsqrt(vss / GATE_WIDTH + EPS)
    for c in range(n_ucol):
        cs = slice(c * IN_PROJ_COLS, (c + 1) * IN_PROJ_COLS)
        v_scr[:, cs] = (v_scr[:, cs] - mean) * rstd * lng_ref[:, cs] + lnb_ref[:, cs]

    if sample:
        vrows_ref[...] = v_scr[...]
        gated_scr[...] = u_scr[...] * (v_scr[...] * ws_ref[...] + bs_ref[...])
    else:
        vrows_ref[0] = v_scr[tm - CHUNK:tm, :]
        t_out = lax.broadcasted_iota(jnp.int32, (CHUNK, CHUNK), 0)
        s_in = lax.broadcasted_iota(jnp.int32, (CHUNK, CHUNK), 1)
        for g in range(N_GATE_GROUPS):
            gs = slice(g * GROUP_WIDTH, (g + 1) * GROUP_WIDTH)
            wsg = jnp.where(s_in <= t_out, ws_ref[g], 0.0).astype(BF16)
            bias = bs_ref[:, g:g + 1]
            for c in range(tm // CHUNK):
                rs = slice(c * CHUNK, (c + 1) * CHUNK)
                mixed = jnp.dot(wsg, v_scr[rs, gs].astype(BF16), preferred_element_type=F32) + bias
                gated_scr[rs, gs] = (u_scr[rs, gs] * mixed).astype(BF16)

    x1 = x + _mm(gated_scr[...], wout_ref[...], sample)
    x1_ref[...] = x1
    if not sample:
        for cp in batch(step, 1):
            cp.wait()
        _reduce_page_batch(pbuf, km_ref, 1, pages_per_block)
    _moe_prenorm_and_route(x1, mg_ref, wr_ref, br_ref, hm_ref, route_ref, sample)


def _gmlp(x, w, sample, seq, cache_k=None, page_table=None):
    n = x.shape[0]
    mw = w["f32"] if sample else w["bf16"]
    tm = n if sample else ROW_TILE
    n_steps = n // tm
    row = lambda i, *_: (i, 0)
    in_specs = [pl.BlockSpec((tm, D_MODEL), row), _full((1, D_MODEL)), _full((D_MODEL, 2 * GATE_WIDTH)),
                _full((1, GATE_WIDTH)), _full((1, GATE_WIDTH)), None, None, _full((GATE_WIDTH, D_MODEL)),
                _full((1, D_MODEL)), _full((D_MODEL, LANES)), _full((1, LANES))]
    out_specs = [pl.BlockSpec((tm, D_MODEL), row), None, pl.BlockSpec((tm * ROW_PIECES, LANES), row),
                 pl.BlockSpec((tm, LANES), row)]
    out_shape = [jax.ShapeDtypeStruct((n, D_MODEL), F32), None,
                 jax.ShapeDtypeStruct((n * ROW_PIECES, LANES), F32), jax.ShapeDtypeStruct((n, LANES), F32)]
    scratch = [pltpu.VMEM((tm, GATE_WIDTH), F32), pltpu.VMEM((tm, GATE_WIDTH), F32),
               pltpu.VMEM((tm, GATE_WIDTH), F32 if sample else BF16)]
    args = [x, w["a_norm_g"], mw["a_w_in"], w["a_ln_g"], w["a_ln_b"],
            w["a_ws0"] if sample else w["a_w_s"], w["a_bs0"] if sample else w["a_b_s_t"], mw["a_w_out"],
            w["moe_norm_g"][0], mw["moe_w_route"][0], w["moe_b_route"][0]]
    if sample:
        in_specs[5], in_specs[6] = _full((1, GATE_WIDTH)), _full((1, GATE_WIDTH))
        out_shape[1] = jax.ShapeDtypeStruct((n, GATE_WIDTH), F32)
        out_specs[1] = pl.BlockSpec((tm, GATE_WIDTH), row)
        n_prefetch = 0
    else:
        in_specs[5], in_specs[6] = _full((N_GATE_GROUPS, CHUNK, CHUNK)), _full((CHUNK, N_GATE_GROUPS))
        tiles_per_seq = seq // tm
        out_shape[1] = jax.ShapeDtypeStruct((n // seq, CHUNK, GATE_WIDTH), F32)
        out_specs[1] = pl.BlockSpec((1, CHUNK, GATE_WIDTH), lambda i, *_: (i // tiles_per_seq, 0, 0))
        n_dec, n_pages = page_table.shape
        page = cache_k.shape[1]
        pages_per_step = N_PAGE_BATCHES * PAGE_BATCH
        assert n_dec * n_pages == n_steps * pages_per_step and n_pages % pages_per_step == 0
        blocks_per_step = pages_per_step * page // MOBA_BLOCK
        steps_per_seq = n_pages // pages_per_step
        in_specs.append(pl.BlockSpec(memory_space=pl.ANY))
        args = [page_table] + args + [cache_k]
        out_specs.append(pl.BlockSpec((1, blocks_per_step, N_HEADS, HEAD_DIM),
                                      lambda i, *_: (i // steps_per_seq, i % steps_per_seq, 0, 0)))
        out_shape.append(jax.ShapeDtypeStruct((n_dec, n_pages * page // MOBA_BLOCK, N_HEADS, HEAD_DIM), F32))
        scratch += [pltpu.VMEM((N_PAGE_BATCHES, PAGE_BATCH, page, N_HEADS, HEAD_DIM), F32),
                    pltpu.SemaphoreType.DMA((N_PAGE_BATCHES,))]
        n_prefetch = 1
    grid_spec = pltpu.PrefetchScalarGridSpec(num_scalar_prefetch=n_prefetch, grid=(n_steps,), in_specs=in_specs,
                                             out_specs=out_specs, scratch_shapes=scratch)
    return pl.pallas_call(
        functools.partial(_gmlp_kernel, sample, tm),
        grid_spec=grid_spec,
        out_shape=out_shape,
        compiler_params=_params(1),
        name="gmlp_sample" if sample else "gmlp_prompt",
    )(*args)


def _expert_kernel(tmg, n_pairs, n_tiles, exact, te_ref, nv_ref, ts_ref, od_ref, h_hbm, wg_ref, wu_ref, wd_ref, out_hbm,
                   xbuf0, xbuf1, ybuf0, ybuf1, sem_in, sem_out, *bf16_weights):
    i = pl.program_id(0)
    xbufs, ybufs = (xbuf0, xbuf1), (ybuf0, ybuf1)
    rows8 = SUBLANES
    weights = (wg_ref, wu_ref, wd_ref) if exact else bf16_weights

    def valid(t):
        return (t >= 0) & (t < n_tiles) & (nv_ref[jnp.clip(t, 0, n_tiles - 1)] > 0)

    def gather_row(r, p, tok=0):
        return pltpu.make_async_copy(h_hbm.at[pl.ds(pl.multiple_of(tok * rows8, rows8), rows8)],
                                     xbufs[p].at[pl.ds(r * rows8, rows8)], sem_in.at[p])

    def scatter_row(r, p, pair=0):
        row8 = r * rows8 if isinstance(r, int) else pl.multiple_of(r * rows8, rows8)
        return pltpu.make_async_copy(ybufs[p].at[pl.ds(row8, rows8)],
                                     out_hbm.at[pl.ds(pl.multiple_of(pair * rows8, rows8), rows8)], sem_out.at[p])

    def start_gather(t, p):
        base = ts_ref[t]
        for r in range(tmg):
            gather_row(r, p, od_ref[base + r] >> 1).start()

    def for_scatter_rows(t, p, start, all_rows):
        base = ts_ref[t]

        def one(r):
            if start:
                scatter_row(r, p, od_ref[base + r]).start()
            else:
                scatter_row(r, p).wait()

        if all_rows:
            for r in range(tmg):
                one(r)
        else:
            def body(r, carry):
                one(r)
                return carry
            lax.fori_loop(0, jnp.minimum(tmg, n_pairs - base), body, 0)

    def refresh_weights(t):
        if not exact:
            @pl.when((t == 0) | (te_ref[t] != te_ref[jnp.maximum(t - 1, 0)]))
            def _():
                for w_bf16, w_ref in zip(bf16_weights, (wg_ref, wu_ref, wd_ref)):
                    w_bf16[...] = w_ref[...].astype(BF16)

    def compute(p):
        wg, wu, wd = weights
        x = _load_tile_rows(xbufs[p], tmg)
        a = jax.nn.silu(_mm(x, wg[...], exact)) * _mm(x, wu[...], exact)
        _store_tile_rows(ybufs[p], _mm(a, wd[...], exact))

    def step(p, guarded):
        when = pl.when if guarded else (lambda cond: (lambda body: body()))

        @when(valid(i - 1))
        def _():
            for r in range(tmg):
                gather_row(r, 1 - p).wait()
            refresh_weights(i - 1)

        @when(valid(i - 3))
        def _():
            for_scatter_rows(i - 3, 1 - p, False, not guarded)

        @when(valid(i - 2))
        def _():
            for_scatter_rows(i - 2, p, True, not guarded)

        @when(valid(i))
        def _():
            start_gather(i, p)

        @when(valid(i - 1))
        def _():
            compute(1 - p)

    steady = (i >= 3) & valid(i) & (ts_ref[jnp.clip(i - 2, 0, n_tiles - 1)] + tmg <= n_pairs)
    for p in range(2):
        @pl.when(steady & (i % 2 == p))
        def _():
            step(p, False)

        @pl.when(jnp.logical_not(steady) & (i % 2 == p))
        def _():
            step(p, True)


def _dispatch(route, tmg):
    n_pairs = EXPERT_TOPK * route.shape[0]
    e_flat = route[:, 0:EXPERT_TOPK].astype(jnp.int32).reshape(n_pairs)
    n_tiles = -(-(n_pairs + N_EXPERTS * (tmg - 1)) // tmg)
    order = jnp.argsort(e_flat, stable=True).astype(jnp.int32)
    counts = jnp.sum((e_flat[:, None] == jnp.arange(N_EXPERTS, dtype=jnp.int32)[None, :]).astype(jnp.int32), axis=0)
    tiles_per = -(-counts // tmg)
    t_end = jnp.cumsum(tiles_per)
    t_start = t_end - tiles_per
    c_start = jnp.cumsum(counts) - counts
    tile = jnp.arange(n_tiles, dtype=jnp.int32)
    tile_expert = jnp.minimum(jnp.sum((t_end[None, :] <= tile[:, None]).astype(jnp.int32), axis=1), N_EXPERTS - 1)
    first_row = (tile - t_start[tile_expert]) * tmg
    tile_valid = jnp.clip(counts[tile_expert] - first_row, 0, tmg).astype(jnp.int32)
    tile_src = jnp.where(tile_valid > 0, c_start[tile_expert] + first_row, 0).astype(jnp.int32)
    order = jnp.concatenate([order, jnp.zeros((tmg,), jnp.int32)])
    return tile_expert.astype(jnp.int32), tile_valid, tile_src, order


def _experts(hm, route, w, layer, tmg, exact):
    n = route.shape[0]
    w_gate, w_up, w_down = w["moe_w_gate"], w["moe_w_up"], w["moe_w_down"]
    tile_expert, tile_valid, tile_src, order = _dispatch(route, tmg)
    n_tiles = tile_expert.shape[0]
    by_expert = lambda i, te, nv, ts, od: (layer, te[jnp.clip(i - 1, 0, n_tiles - 1)], 0, 0)
    scratch = [pltpu.VMEM((tmg * ROW_PIECES, LANES), F32)] * 4 \
        + [pltpu.SemaphoreType.DMA((2,)), pltpu.SemaphoreType.DMA((2,))]
    if not exact:
        scratch += [pltpu.VMEM((D_MODEL, EXPERT_FF), BF16), pltpu.VMEM((D_MODEL, EXPERT_FF), BF16),
                    pltpu.VMEM((EXPERT_FF, D_MODEL), BF16)]
    grid_spec = pltpu.PrefetchScalarGridSpec(
        num_scalar_prefetch=4,
        grid=(n_tiles + 3,),
        in_specs=[pl.BlockSpec(memory_space=pl.ANY),
                  pl.BlockSpec((None, None, D_MODEL, EXPERT_FF), by_expert),
                  pl.BlockSpec((None, None, D_MODEL, EXPERT_FF), by_expert),
                  pl.BlockSpec((None, None, EXPERT_FF, D_MODEL), by_expert)],
        out_specs=pl.BlockSpec(memory_space=pl.ANY),
        scratch_shapes=scratch,
    )
    return pl.pallas_call(
        functools.partial(_expert_kernel, tmg, EXPERT_TOPK * n, n_tiles, exact),
        grid_spec=grid_spec,
        out_shape=jax.ShapeDtypeStruct((EXPERT_TOPK * n * ROW_PIECES, LANES), F32),
        compiler_params=_params(1),
        name="experts",
    )(tile_expert, tile_valid, tile_src, order, hm, w_gate, w_up, w_down)


def _rope(xh, cos, sin, lane):
    swapped = jnp.where(lane < ROT_HALF, pltpu.roll(xh, LANES - ROT_HALF, 1), pltpu.roll(xh, ROT_HALF, 1))
    return xh * cos + swapped * sin


def _kvq_kernel(prompt, exact, x1_ref, yp_ref, route_ref, kvg_ref, wkv_ref, kg_ref, cos_ref, sin_ref,
                bng_ref, wq_ref, qg_ref, x2_ref, k_ref, v_ref, q_ref, *prompt_refs):
    x2 = _moe_residual(x1_ref[...], yp_ref, route_ref)
    x2_ref[...] = x2
    cos = cos_ref[...]
    sin = sin_ref[...]
    lane = lax.broadcasted_iota(jnp.int32, cos.shape, 1)
    kv = _mm(_rms(x2, kvg_ref[...]), wkv_ref[...], exact)
    for h in range(N_HEADS):
        hs = slice(h * HEAD_DIM, (h + 1) * HEAD_DIM)
        kh = _rope(_rms(kv[:, hs], kg_ref[...]), cos, sin, lane)
        k_ref[:, hs] = kh
        if prompt:
            kb_ref, _, km_ref = prompt_refs
            kb_ref[:, hs] = kh.astype(BF16)
            km_ref[0, :, hs] = jnp.mean(kh, axis=0, keepdims=True)
    v = kv[:, D_MODEL:]
    v_ref[...] = v
    if prompt:
        prompt_refs[1][0] = v.T.astype(BF16)
    q = _mm(_rms(x2, bng_ref[...]), wq_ref[...], exact)
    for h in range(N_HEADS):
        hs = slice(h * HEAD_DIM, (h + 1) * HEAD_DIM)
        q_ref[:, hs] = _rope(_rms(q[:, hs], qg_ref[...]), cos, sin, lane)


def _kvq(x1, ypairs, route, cos_tab, sin_tab, w, prompt, exact):
    n = x1.shape[0]
    mw = w["f32"] if exact else w["bf16"]
    tm = ROW_TILE if prompt else n
    row = lambda i: (i, 0)
    tab_tiles = cos_tab.shape[0] // tm
    tab = lambda i: (i % tab_tiles, 0)
    out_specs = [pl.BlockSpec((tm, D_MODEL), row)] * 4
    out_shape = [jax.ShapeDtypeStruct((n, D_MODEL), F32)] * 4
    if prompt:
        seq = cos_tab.shape[0]
        out_specs += [pl.BlockSpec((tm, D_MODEL), row),
                      pl.BlockSpec((1, D_MODEL, tm), lambda i: (i // tab_tiles, 0, i % tab_tiles)),
                      pl.BlockSpec((1, 1, D_MODEL), lambda i: (i, 0, 0))]
        out_shape += [jax.ShapeDtypeStruct((n, D_MODEL), BF16), jax.ShapeDtypeStruct((n // seq, D_MODEL, seq), BF16),
                      jax.ShapeDtypeStruct((n // tm, 1, D_MODEL), F32)]
    return pl.pallas_call(
        functools.partial(_kvq_kernel, prompt, exact),
        grid=(n // tm,),
        in_specs=[pl.BlockSpec((tm, D_MODEL), row), pl.BlockSpec((EXPERT_TOPK * tm * ROW_PIECES, LANES), row),
                  pl.BlockSpec((tm, LANES), row),
                  _full((1, D_MODEL)), _full((D_MODEL, 2 * D_MODEL)), _full((1, HEAD_DIM)),
                  pl.BlockSpec((tm, HEAD_DIM), tab), pl.BlockSpec((tm, HEAD_DIM), tab),
                  _full((1, D_MODEL)), _full((D_MODEL, D_MODEL)), _full((1, HEAD_DIM))],
        out_specs=out_specs,
        out_shape=out_shape,
        compiler_params=_params(1),
        name="kvq",
    )(x1, ypairs, route, w["kv_norm_g"], mw["w_kv"], w["k_norm_g"], cos_tab, sin_tab,
      w["b_norm_g"], mw["b_w_q"], w["q_norm_g"])


def _attn_kernel(q_ref, kb_ref, vt_ref, km_ref, x2_ref, wo_ref, mg_ref, wr_ref, br_ref,
                 x3_ref, hm_ref, route_ref, qb_scr, sel_scr, o_scr, *acc_scrs):
    j = pl.program_id(1)
    tq = q_ref.shape[0]
    n_blk = km_ref.shape[1]
    scale = HEAD_DIM ** -0.5
    blk = lax.broadcasted_iota(jnp.int32, (n_blk, tq), 0).astype(F32)
    cur = j.astype(F32)
    contract_last = (((1,), (1,)), ((), ()))
    own = pl.multiple_of(j * MOBA_BLOCK, MOBA_BLOCK)
    qb_scr[...] = q_ref[...].astype(BF16)
    chains = [(h, g) for h in range(N_HEADS) for g in range(tq // LANES)]

    def scores(h, g, off):
        hs = slice(h * HEAD_DIM, (h + 1) * HEAD_DIM)
        return lax.dot_general(kb_ref[pl.ds(off, MOBA_BLOCK), hs], qb_scr[g * LANES:(g + 1) * LANES, hs],
                               contract_last, preferred_element_type=F32) * scale

    def weighted_values(h, off, p):
        hs = slice(h * HEAD_DIM, (h + 1) * HEAD_DIM)
        return jnp.dot(vt_ref[0, hs, pl.ds(off, MOBA_BLOCK)], p.astype(BF16), preferred_element_type=F32)

    for h in range(N_HEADS):
        hs = slice(h * HEAD_DIM, (h + 1) * HEAD_DIM)
        gate = lax.dot_general(km_ref[0, :, hs].astype(BF16), qb_scr[:, hs], contract_last,
                               preferred_element_type=F32)
        gate = jnp.where(blk < cur, gate, NEG_INF)
        sel = jnp.zeros((n_blk, tq), F32)
        for _ in range(MOBA_TOPK):
            m = jnp.max(gate, axis=0, keepdims=True)
            first = jnp.min(jnp.where(gate == m, blk, float(n_blk)), axis=0, keepdims=True)
            pick = (blk == first) & (m > NEG_INF)
            sel = jnp.where(pick, 1.0, sel)
            gate = jnp.where(pick, NEG_INF, gate)
        sel_scr[h] = sel

    k_pos = lax.broadcasted_iota(jnp.int32, (MOBA_BLOCK, LANES), 0)
    q_pos = lax.broadcasted_iota(jnp.int32, (MOBA_BLOCK, LANES), 1)
    stats = []
    for h, g in chains:
        s = jnp.where(k_pos <= q_pos + g * LANES, scores(h, g, own), NEG_INF)
        m0 = jnp.max(s, axis=0, keepdims=True)
        p = jnp.exp(s - m0)
        stats += [m0, jnp.sum(p, axis=0, keepdims=True)]
        acc_scrs[h][:, g * LANES:(g + 1) * LANES] = weighted_values(h, own, p)

    def past_block(jj, stats):
        off = pl.multiple_of(jj * MOBA_BLOCK, MOBA_BLOCK)
        new_stats = []
        visible = [sel_scr[h, pl.ds(jj, 1), :] for h in range(N_HEADS)]
        for c, (h, g) in enumerate(chains):
            qs = slice(g * LANES, (g + 1) * LANES)
            m_old, l_old = stats[2 * c], stats[2 * c + 1]
            s = jnp.where(visible[h][:, qs] > 0.0, scores(h, g, off), NEG_INF)
            m_new = jnp.maximum(m_old, jnp.max(s, axis=0, keepdims=True))
            alpha = jnp.exp(m_old - m_new)
            p = jnp.exp(s - m_new)
            new_stats += [m_new, alpha * l_old + jnp.sum(p, axis=0, keepdims=True)]
            acc_scrs[h][:, qs] = alpha * acc_scrs[h][:, qs] + weighted_values(h, off, p)
        return tuple(new_stats)

    stats = lax.fori_loop(0, j, past_block, tuple(stats))

    for c, (h, g) in enumerate(chains):
        qs = slice(g * LANES, (g + 1) * LANES)
        o_scr[qs, h * HEAD_DIM:(h + 1) * HEAD_DIM] = (acc_scrs[h][:, qs] / stats[2 * c + 1]).T.astype(BF16)

    x3 = x2_ref[...] + jnp.dot(o_scr[...], wo_ref[...], preferred_element_type=F32)
    x3_ref[...] = x3
    _moe_prenorm_and_route(x3, mg_ref, wr_ref, br_ref, hm_ref, route_ref, False)


def _attention_prompt(q, kb, vt, kmean, x2, w, seq):
    n = q.shape[0]
    bsz = n // seq
    n_qb = seq // MOBA_BLOCK
    row = lambda b, j: (b * n_qb + j, 0)
    return pl.pallas_call(
        _attn_kernel,
        grid=(bsz, n_qb),
        in_specs=[pl.BlockSpec((MOBA_BLOCK, D_MODEL), row),
                  pl.BlockSpec((seq, D_MODEL), lambda b, j: (b, 0)),
                  pl.BlockSpec((1, D_MODEL, seq), lambda b, j: (b, 0, 0)),
                  pl.BlockSpec((1, n_qb, D_MODEL), lambda b, j: (b, 0, 0)),
                  pl.BlockSpec((MOBA_BLOCK, D_MODEL), row),
                  _full((D_MODEL, D_MODEL)), _full((1, D_MODEL)), _full((D_MODEL, LANES)), _full((1, LANES))],
        out_specs=[pl.BlockSpec((MOBA_BLOCK, D_MODEL), row), pl.BlockSpec((MOBA_BLOCK * ROW_PIECES, LANES), row),
                   pl.BlockSpec((MOBA_BLOCK, LANES), row)],
        out_shape=[jax.ShapeDtypeStruct((n, D_MODEL), F32), jax.ShapeDtypeStruct((n * ROW_PIECES, LANES), F32),
                   jax.ShapeDtypeStruct((n, LANES), F32)],
        scratch_shapes=[pltpu.VMEM((MOBA_BLOCK, D_MODEL), BF16),
                        pltpu.VMEM((N_HEADS, n_qb, MOBA_BLOCK), F32),
                        pltpu.VMEM((MOBA_BLOCK, D_MODEL), BF16)]
                       + [pltpu.VMEM((HEAD_DIM, MOBA_BLOCK), F32)] * N_HEADS,
        compiler_params=_params(2),
        name="moba_prompt",
    )(q, kb, vt, kmean.reshape(bsz, n_qb, D_MODEL), x2, w["bf16"]["b_w_o"],
      w["moe_norm_g"][1], w["bf16"]["moe_w_route"][1], w["moe_b_route"][1])


def _block_select_kernel(q_ref, km_ref, sel_ref):
    gate = jnp.sum(km_ref[0] * q_ref[...], axis=-1, keepdims=True)
    n_blk = gate.shape[0]
    blk = lax.broadcasted_iota(jnp.int32, gate.shape, 0).astype(F32)
    rank = lax.broadcasted_iota(jnp.int32, sel_ref.shape[1:], 1)
    out = jnp.zeros(sel_ref.shape[1:], F32)
    for r in range(MOBA_TOPK):
        m = jnp.max(gate, axis=0, keepdims=True)
        first = jnp.min(jnp.where(gate == m, blk, float(n_blk)), axis=0, keepdims=True)
        gate = jnp.where(blk == first, NEG_INF, gate)
        out = jnp.where(rank == r, first[0], out)
    sel_ref[0] = out.astype(jnp.int32)


def _block_select(q4, kmean_s):
    n_seq, n_blk = kmean_s.shape[:2]
    return pl.pallas_call(
        _block_select_kernel,
        grid=(n_seq,),
        in_specs=[pl.BlockSpec((1, N_HEADS, HEAD_DIM), lambda s: (s, 0, 0)),
                  pl.BlockSpec((1, n_blk, N_HEADS, HEAD_DIM), lambda s: (s, 0, 0, 0))],
        out_specs=pl.BlockSpec((1, N_HEADS, LANES), lambda s: (s, 0, 0)),
        out_shape=jax.ShapeDtypeStruct((n_seq, N_HEADS, LANES), jnp.int32),
        compiler_params=_params(1),
        name="block_select",
    )(q4, kmean_s)


def _attn_sample_kernel(n_sel, pg_ref, q_ref, k_own_ref, v_own_ref, ck_hbm, cv_hbm, o_ref, kbuf, vbuf, sem):
    s = pl.program_id(0)

    def page_copies(seq, slot):
        cps = []
        for h in range(N_HEADS):
            for i in range(n_sel):
                pg = pg_ref[(seq * N_HEADS + h) * n_sel + i]
                cps.append(pltpu.make_async_copy(ck_hbm.at[pg, :, h, :], kbuf.at[slot, h, i], sem.at[slot]))
                cps.append(pltpu.make_async_copy(cv_hbm.at[pg, :, h, :], vbuf.at[slot, h, i], sem.at[slot]))
        return cps

    @pl.when(s == 0)
    def _():
        for cp in page_copies(0, 0):
            cp.start()

    @pl.when(s + 1 < pl.num_programs(0))
    def _():
        for cp in page_copies(s + 1, (s + 1) % 2):
            cp.start()

    slot = s % 2
    for cp in page_copies(s, slot):
        cp.wait()

    scale = HEAD_DIM ** -0.5
    for h in range(N_HEADS):
        hs = slice(h * HEAD_DIM, (h + 1) * HEAD_DIM)
        q = q_ref[0, :, hs]
        s_own = jnp.sum(k_own_ref[0, :, hs] * q, axis=-1, keepdims=True) * scale
        scores = [jnp.sum(kbuf[slot, h, i] * q, axis=-1, keepdims=True) * scale for i in range(n_sel)]
        m = s_own
        for sc in scores:
            m = jnp.maximum(m, jnp.max(sc, axis=0, keepdims=True))
        p_own = jnp.exp(s_own - m)
        l = p_own
        acc = p_own * v_own_ref[0, :, hs]
        for i, sc in enumerate(scores):
            p = jnp.exp(sc - m)
            l = l + jnp.sum(p, axis=0, keepdims=True)
            acc = acc + jnp.sum(p * vbuf[slot, h, i], axis=0, keepdims=True)
        o_ref[0, :, hs] = acc / l


def _attention_sample(q3, k3, v3, cache_k, cache_v, sel_pages):
    n_seq = q3.shape[0]
    page = cache_k.shape[1]
    n_sel = sel_pages.shape[-1]
    seq_row = lambda s, pg: (s, 0, 0)
    grid_spec = pltpu.PrefetchScalarGridSpec(
        num_scalar_prefetch=1,
        grid=(n_seq,),
        in_specs=[pl.BlockSpec((1, 1, D_MODEL), seq_row)] * 3
                 + [pl.BlockSpec(memory_space=pl.ANY), pl.BlockSpec(memory_space=pl.ANY)],
        out_specs=pl.BlockSpec((1, 1, D_MODEL), seq_row),
        scratch_shapes=[pltpu.VMEM((2, N_HEADS, n_sel, page, HEAD_DIM), F32),
                        pltpu.VMEM((2, N_HEADS, n_sel, page, HEAD_DIM), F32),
                        pltpu.SemaphoreType.DMA((2,))],
    )
    return pl.pallas_call(
        functools.partial(_attn_sample_kernel, n_sel),
        grid_spec=grid_spec,
        out_shape=jax.ShapeDtypeStruct((n_seq, 1, D_MODEL), F32),
        compiler_params=_params(1),
        name="moba_sample",
    )(sel_pages.reshape(-1), q3, k3, v3, cache_k, cache_v)


def _oproj_kernel(o_ref, x2_ref, wo_ref, mg_ref, wr_ref, br_ref, x3_ref, hm_ref, route_ref):
    x3 = x2_ref[...] + _mm(o_ref[...], wo_ref[...], True)
    x3_ref[...] = x3
    _moe_prenorm_and_route(x3, mg_ref, wr_ref, br_ref, hm_ref, route_ref, True)


def _oproj(o, x2, w):
    n = o.shape[0]
    return pl.pallas_call(
        _oproj_kernel,
        grid=(1,),
        in_specs=[_full((n, D_MODEL)), _full((n, D_MODEL)), _full((D_MODEL, D_MODEL)), _full((1, D_MODEL)),
                  _full((D_MODEL, LANES)), _full((1, LANES))],
        out_specs=[_full((n, D_MODEL)), _full((n * ROW_PIECES, LANES)), _full((n, LANES))],
        out_shape=[jax.ShapeDtypeStruct((n, D_MODEL), F32), jax.ShapeDtypeStruct((n * ROW_PIECES, LANES), F32),
                   jax.ShapeDtypeStruct((n, LANES), F32)],
        compiler_params=_params(1),
        name="oproj_sample",
    )(o, x2, w["f32"]["b_w_o"], w["moe_norm_g"][1], w["f32"]["moe_w_route"][1], w["moe_b_route"][1])


def _combine_kernel(x_ref, yp_ref, route_ref, o_ref):
    o_ref[...] = _moe_residual(x_ref[...], yp_ref, route_ref)


def _combine(x, ypairs, route):
    n = x.shape[0]
    tm = min(n, ROW_TILE)
    row = lambda i: (i, 0)
    return pl.pallas_call(
        _combine_kernel,
        grid=(n // tm,),
        in_specs=[pl.BlockSpec((tm, D_MODEL), row), pl.BlockSpec((EXPERT_TOPK * tm * ROW_PIECES, LANES), row),
                  pl.BlockSpec((tm, LANES), row)],
        out_specs=pl.BlockSpec((tm, D_MODEL), row),
        out_shape=jax.ShapeDtypeStruct((n, D_MODEL), F32),
        compiler_params=_params(1),
        name="combine",
    )(x, ypairs, route)


def _rope_tables(pos):
    inv_freq = ROPE_THETA ** (-jnp.arange(ROT_HALF, dtype=F32) / ROT_HALF)
    ang = pos.astype(F32)[:, None] * inv_freq[None, :]
    cos, sin = jnp.cos(ang), jnp.sin(ang)
    rest = HEAD_DIM - ROT_DIM
    cos_tab = jnp.concatenate([cos, cos, jnp.ones((pos.shape[0], rest), F32)], axis=-1)
    sin_tab = jnp.concatenate([-sin, sin, jnp.zeros((pos.shape[0], rest), F32)], axis=-1)
    return cos_tab, sin_tab


def kernel(x_prompt, x_sample, cache_k, cache_v, page_table, a_norm_g, a_w_in, a_ln_g, a_ln_b, a_w_s, a_b_s, a_w_out, kv_norm_g, w_kv, k_norm_g, b_norm_g, b_w_q, q_norm_g, b_w_o, moe_norm_g, moe_w_rg, moe_b_rg, moe_w_re, moe_b_re, moe_w_gate, moe_w_up, moe_w_down):
    bsz, seq, _ = x_prompt.shape
    n_dec = x_sample.shape[0]
    assert x_sample.shape[1] == 1 and a_w_in.shape[0] == 1 and b_w_q.shape[0] == 1
    assert seq % ROW_TILE == 0 and ROW_TILE == MOBA_BLOCK
    page = cache_k.shape[1]
    past = page_table.shape[1] * page
    assert past % MOBA_BLOCK == 0 and past // MOBA_BLOCK >= MOBA_TOPK and MOBA_BLOCK % page == 0

    depth = moe_norm_g.shape[0]
    pad = LANES - N_EXPERT_GROUPS - N_EXPERTS
    mats = {
        "a_w_in": a_w_in[0], "a_w_out": a_w_out[0], "w_kv": w_kv, "b_w_q": b_w_q[0], "b_w_o": b_w_o[0],
        "moe_w_route": jnp.concatenate([moe_w_rg, moe_w_re, jnp.zeros((depth, D_MODEL, pad), F32)], axis=-1),
    }
    w = {
        "f32": mats, "bf16": {name: m.astype(BF16) for name, m in mats.items()},
        "moe_w_gate": moe_w_gate, "moe_w_up": moe_w_up, "moe_w_down": moe_w_down,
        "a_norm_g": a_norm_g[0][None], "a_ln_g": a_ln_g[0][None], "a_ln_b": a_ln_b[0][None],
        "a_w_s": a_w_s[0], "a_b_s_t": a_b_s[0].T,
        "a_ws0": jnp.repeat(a_w_s[0, :, 0, 0], GROUP_WIDTH)[None], "a_bs0": jnp.repeat(a_b_s[0, :, 0], GROUP_WIDTH)[None],
        "kv_norm_g": kv_norm_g[None], "k_norm_g": k_norm_g[None],
        "b_norm_g": b_norm_g[0][None], "q_norm_g": q_norm_g[0][None],
        "moe_norm_g": moe_norm_g[:, None, :],
        "moe_b_route": jnp.concatenate([moe_b_rg, moe_b_re, jnp.zeros((depth, pad), F32)], axis=-1)[:, None, :],
    }

    xp = x_prompt.reshape(bsz * seq, D_MODEL)
    xs = x_sample.reshape(n_dec, D_MODEL)
    cos_p, sin_p = _rope_tables(jnp.arange(seq, dtype=jnp.int32))
    cos_s, sin_s = _rope_tables(jnp.full((n_dec,), past, dtype=jnp.int32))

    x1p, vrows_p, hmp, route0p, kmean_s = _gmlp(xp, w, False, seq, cache_k, page_table)
    x1s, vrows_s, hms, route0s = _gmlp(xs, w, True, seq)
    yp = _experts(hmp, route0p, w, 0, EXPERT_TILE, False)
    ys = _experts(hms, route0s, w, 0, EXPERT_TILE_SMALL, True)

    x2p, k_p, v_p, q_p, kb_p, vt_p, kmean_p = _kvq(x1p, yp, route0p, cos_p, sin_p, w, True, False)
    x2s, k_s, v_s, q_s = _kvq(x1s, ys, route0s, cos_s, sin_s, w, False, True)

    x3p, hmp, route1p = _attention_prompt(q_p, kb_p, vt_p, kmean_p, x2p, w, seq)
    sel_blocks = _block_select(q_s.reshape(n_dec, N_HEADS, HEAD_DIM), kmean_s)[:, :, :MOBA_TOPK]
    pages_per_block = MOBA_BLOCK // page
    page_slots = sel_blocks[..., None] * pages_per_block + jnp.arange(pages_per_block, dtype=jnp.int32)
    page_slots = page_slots.reshape(n_dec, N_HEADS * MOBA_TOPK * pages_per_block)
    sel_pages = jnp.take_along_axis(page_table, page_slots, axis=1).reshape(n_dec, N_HEADS, MOBA_TOPK * pages_per_block)
    o_s = _attention_sample(q_s.reshape(n_dec, 1, D_MODEL), k_s.reshape(n_dec, 1, D_MODEL),
                            v_s.reshape(n_dec, 1, D_MODEL), cache_k, cache_v, sel_pages)
    x3s, hms, route1s = _oproj(o_s.reshape(n_dec, D_MODEL), x2s, w)
    yp = _experts(hmp, route1p, w, 1, EXPERT_TILE, False)
    ys = _experts(hms, route1s, w, 1, EXPERT_TILE_SMALL, True)
    y_prompt = _combine(x3p, yp, route1p)
    y_sample = _combine(x3s, ys, route1s)

    return (y_prompt.reshape(bsz, seq, D_MODEL), y_sample.reshape(n_dec, 1, D_MODEL),
            vrows_p.reshape(1, bsz, CHUNK, GATE_WIDTH), vrows_s.reshape(1, n_dec, 1, GATE_WIDTH),
            k_p.reshape(bsz, seq, N_HEADS, HEAD_DIM), v_p.reshape(bsz, seq, N_HEADS, HEAD_DIM),
            k_s.reshape(n_dec, 1, N_HEADS, HEAD_DIM), v_s.reshape(n_dec, 1, N_HEADS, HEAD_DIM))
```

```python
import functools

import jax
import jax.numpy as jnp
from jax import lax
from jax.experimental import pallas as pl
from jax.experimental.pallas import tpu as pltpu

D_MODEL = 1024
CHUNK = 128
GATE_WIDTH = 2 * D_MODEL
N_GATE_GROUPS = 8
GROUP_WIDTH = GATE_WIDTH // N_GATE_GROUPS
HEAD_DIM = 128
N_HEADS = D_MODEL // HEAD_DIM
ROT_DIM = HEAD_DIM // 4
ROT_HALF = ROT_DIM // 2
ROPE_THETA = 500000.0
MOBA_BLOCK = 256
MOBA_TOPK = 3
N_EXPERT_GROUPS = 4
EXPERTS_PER_GROUP = 4
N_EXPERTS = N_EXPERT_GROUPS * EXPERTS_PER_GROUP
EXPERT_TOPK = 2
EXPERT_FF = D_MODEL // 2
EPS = 1e-6

LANES = 128
SUBLANES = 8
ROW_PIECES = D_MODEL // LANES
ROW_TILE = 256
EXPERT_TILE = 256
EXPERT_TILE_SMALL = 16
IN_PROJ_COLS = 512
N_IN_PROJ_CHUNKS = 2 * GATE_WIDTH // IN_PROJ_COLS
PAGE_BATCH = 16
N_PAGE_BATCHES = 2
VMEM_LIMIT = 60 * 1024 * 1024

F32 = jnp.float32
BF16 = jnp.bfloat16
NEG_INF = float("-inf")

assert ROW_PIECES == SUBLANES


def _params(n_axes):
    return pltpu.CompilerParams(dimension_semantics=("arbitrary",) * n_axes, vmem_limit_bytes=VMEM_LIMIT)


def _rms(x, g):
    return x * lax.rsqrt(jnp.mean(x * x, axis=-1, keepdims=True) + EPS) * g


def _mm(a, b, exact):
    if exact:
        return jnp.dot(a, b, precision=lax.Precision.HIGHEST, preferred_element_type=F32)
    return jnp.dot(a.astype(BF16), b, preferred_element_type=F32)


def _full(shape):
    nd = len(shape)
    return pl.BlockSpec(shape, lambda *_: (0,) * nd, pipeline_mode=pl.Buffered(1))


def _store_tile_rows(ref, x):
    for s in range(ROW_PIECES):
        ref[pl.ds(s, x.shape[0], stride=ROW_PIECES), :] = x[:, s * LANES:(s + 1) * LANES]


def _load_tile_rows(ref, rows, first=0, step=ROW_PIECES):
    return jnp.concatenate([ref[pl.ds(first + s, rows, stride=step), :] for s in range(ROW_PIECES)], axis=-1)


def _route(hm, wr_ref, br_ref, exact):
    logits = _mm(hm, wr_ref[...], exact) + br_ref[...]
    lane = lax.broadcasted_iota(jnp.int32, logits.shape, 1).astype(F32)
    lg = jnp.where(lane < N_EXPERT_GROUPS, logits, NEG_INF)
    mg = jnp.max(lg, axis=-1, keepdims=True)
    g_sel = jnp.min(jnp.where(lg == mg, lane, float(LANES)), axis=-1, keepdims=True)
    p_grp = 1.0 / jnp.sum(jnp.exp(lg - mg), axis=-1, keepdims=True)
    lo = N_EXPERT_GROUPS + g_sel * EXPERTS_PER_GROUP
    le = jnp.where((lane >= lo) & (lane < lo + EXPERTS_PER_GROUP), logits, NEG_INF)
    v1 = jnp.max(le, axis=-1, keepdims=True)
    i1 = jnp.min(jnp.where(le == v1, lane, float(LANES)), axis=-1, keepdims=True)
    le2 = jnp.where(lane == i1, NEG_INF, le)
    v2 = jnp.max(le2, axis=-1, keepdims=True)
    i2 = jnp.min(jnp.where(le2 == v2, lane, float(LANES)), axis=-1, keepdims=True)
    t = jnp.exp(v2 - v1)
    w1 = p_grp / (1.0 + t)
    w2 = p_grp * t / (1.0 + t)
    out = jnp.where(lane == 0.0, i1 - N_EXPERT_GROUPS, 0.0)
    out = jnp.where(lane == 1.0, i2 - N_EXPERT_GROUPS, out)
    out = jnp.where(lane == 2.0, w1, out)
    out = jnp.where(lane == 3.0, w2, out)
    return out


def _moe_prenorm_and_route(x, mg_ref, wr_ref, br_ref, hm_ref, route_ref, exact):
    hm = _rms(x, mg_ref[...])
    _store_tile_rows(hm_ref, hm)
    route_ref[...] = _route(hm, wr_ref, br_ref, exact)


def _moe_residual(x, y_ref, route_ref):
    rows = x.shape[0]
    step = EXPERT_TOPK * ROW_PIECES
    y1 = _load_tile_rows(y_ref, rows, 0, step)
    y2 = _load_tile_rows(y_ref, rows, ROW_PIECES, step)
    return x + (route_ref[:, 2:3] * y1 + route_ref[:, 3:4] * y2)


def _page_batch_copies(pt_ref, ck_hbm, pbuf, psem, step, k):
    n_pages = pt_ref.shape[1]
    first = (step * N_PAGE_BATCHES + k) * PAGE_BATCH
    seq, page0 = first // n_pages, first % n_pages
    return [pltpu.make_async_copy(ck_hbm.at[pt_ref[seq, page0 + p]], pbuf.at[k, p], psem.at[k])
            for p in range(PAGE_BATCH)]


def _reduce_page_batch(pbuf, km_ref, k, pages_per_block):
    blocks = PAGE_BATCH // pages_per_block
    for b in range(blocks):
        tot = jnp.sum(pbuf[k, b * pages_per_block], axis=0)
        for p in range(1, pages_per_block):
            tot = tot + jnp.sum(pbuf[k, b * pages_per_block + p], axis=0)
        km_ref[0, k * blocks + b] = tot / MOBA_BLOCK


def _gmlp_kernel(sample, tm, *refs):
    if sample:
        (x_ref, ng_ref, win_ref, lng_ref, lnb_ref, ws_ref, bs_ref, wout_ref, mg_ref, wr_ref, br_ref,
         x1_ref, vrows_ref, hm_ref, route_ref, u_scr, v_scr, gated_scr) = refs
    else:
        (pt_ref, x_ref, ng_ref, win_ref, lng_ref, lnb_ref, ws_ref, bs_ref, wout_ref, mg_ref, wr_ref, br_ref, ck_hbm,
         x1_ref, vrows_ref, hm_ref, route_ref, km_ref, u_scr, v_scr, gated_scr, pbuf, psem) = refs
        step = pl.program_id(0)
        batch = functools.partial(_page_batch_copies, pt_ref, ck_hbm, pbuf, psem)
        pages_per_block = MOBA_BLOCK // ck_hbm.shape[1]

        @pl.when(step == 0)
        def _():
            for cp in batch(step, 0):
                cp.start()

        for cp in batch(step, 1):
            cp.start()

    x = x_ref[...]
    h = _rms(x, ng_ref[...])
    if not sample:
        h = h.astype(BF16)
    n_col = N_IN_PROJ_CHUNKS
    n_ucol = GATE_WIDTH // IN_PROJ_COLS
    vsum = jnp.zeros((tm, 1), F32)
    for c in range(n_col):
        z = _mm(h, win_ref[:, c * IN_PROJ_COLS:(c + 1) * IN_PROJ_COLS], sample)
        z = jax.nn.gelu(z, approximate=True)
        if c < n_ucol:
            u_scr[:, c * IN_PROJ_COLS:(c + 1) * IN_PROJ_COLS] = z
        else:
            v_scr[:, (c - n_ucol) * IN_PROJ_COLS:(c - n_ucol + 1) * IN_PROJ_COLS] = z
            vsum = vsum + jnp.sum(z, axis=-1, keepdims=True)

    if not sample:
        for cp in batch(step, 0):
            cp.wait()
        _reduce_page_batch(pbuf, km_ref, 0, pages_per_block)

        @pl.when(step + 1 < pl.num_programs(0))
        def _():
            for cp in batch(step + 1, 0):
                cp.start()

    mean = vsum / GATE_WIDTH
    vss = jnp.zeros((tm, 1), F32)
    for c in range(n_ucol):
        xc = v_scr[:, c * IN_PROJ_COLS:(c + 1) * IN_PROJ_COLS] - mean
        vss = vss + jnp.sum(xc * xc, axis=-1, keepdims=True)
    rstd = lax.rsqrt(vss / GATE_WIDTH + EPS)
    for c in range(n_ucol):
        cs = slice(c * IN_PROJ_COLS, (c + 1) * IN_PROJ_COLS)
        v_scr[:, cs] = (v_scr[:, cs] - mean) * rstd * lng_ref[:, cs] + lnb_ref[:, cs]

    if sample:
        vrows_ref[...] = v_scr[...]
        gated_scr[...] = u_scr[...] * (v_scr[...] * ws_ref[...] + bs_ref[...])
    else:
        vrows_ref[0] = v_scr[tm - CHUNK:tm, :]
        t_out = lax.broadcasted_iota(jnp.int32, (CHUNK, CHUNK), 0)
        s_in = lax.broadcasted_iota(jnp.int32, (CHUNK, CHUNK), 1)
        for g in range(N_GATE_GROUPS):
            gs = slice(g * GROUP_WIDTH, (g + 1) * GROUP_WIDTH)
            wsg = jnp.where(s_in <= t_out, ws_ref[g], 0.0).astype(BF16)
            bias = bs_ref[:, g:g + 1]
            for c in range(tm // CHUNK):
                rs = slice(c * CHUNK, (c + 1) * CHUNK)
                mixed = jnp.dot(wsg, v_scr[rs, gs].astype(BF16), preferred_element_type=F32) + bias
                gated_scr[rs, gs] = (u_scr[rs, gs] * mixed).astype(BF16)

    x1 = x + _mm(gated_scr[...], wout_ref[...], sample)
    x1_ref[...] = x1
    if not sample:
        for cp in batch(step, 1):
            cp.wait()
        _reduce_page_batch(pbuf, km_ref, 1, pages_per_block)
    _moe_prenorm_and_route(x1, mg_ref, wr_ref, br_ref, hm_ref, route_ref, sample)


def _gmlp(x, w, sample, seq, cache_k=None, page_table=None):
    n = x.shape[0]
    mw = w["f32"] if sample else w["bf16"]
    tm = n if sample else ROW_TILE
    n_steps = n // tm
    row = lambda i, *_: (i, 0)
    in_specs = [pl.BlockSpec((tm, D_MODEL), row), _full((1, D_MODEL)), _full((D_MODEL, 2 * GATE_WIDTH)),
                _full((1, GATE_WIDTH)), _full((1, GATE_WIDTH)), None, None, _full((GATE_WIDTH, D_MODEL)),
                _full((1, D_MODEL)), _full((D_MODEL, LANES)), _full((1, LANES))]
    out_specs = [pl.BlockSpec((tm, D_MODEL), row), None, pl.BlockSpec((tm * ROW_PIECES, LANES), row),
                 pl.BlockSpec((tm, LANES), row)]
    out_shape = [jax.ShapeDtypeStruct((n, D_MODEL), F32), None,
                 jax.ShapeDtypeStruct((n * ROW_PIECES, LANES), F32), jax.ShapeDtypeStruct((n, LANES), F32)]
    scratch = [pltpu.VMEM((tm, GATE_WIDTH), F32), pltpu.VMEM((tm, GATE_WIDTH), F32),
               pltpu.VMEM((tm, GATE_WIDTH), F32 if sample else BF16)]
    args = [x, w["a_norm_g"], mw["a_w_in"], w["a_ln_g"], w["a_ln_b"],
            w["a_ws0"] if sample else w["a_w_s"], w["a_bs0"] if sample else w["a_b_s_t"], mw["a_w_out"],
            w["moe_norm_g"][0], mw["moe_w_route"][0], w["moe_b_route"][0]]
    if sample:
        in_specs[5], in_specs[6] = _full((1, GATE_WIDTH)), _full((1, GATE_WIDTH))
        out_shape[1] = jax.ShapeDtypeStruct((n, GATE_WIDTH), F32)
        out_specs[1] = pl.BlockSpec((tm, GATE_WIDTH), row)
        n_prefetch = 0
    else:
        in_specs[5], in_specs[6] = _full((N_GATE_GROUPS, CHUNK, CHUNK)), _full((CHUNK, N_GATE_GROUPS))
        tiles_per_seq = seq // tm
        out_shape[1] = jax.ShapeDtypeStruct((n // seq, CHUNK, GATE_WIDTH), F32)
        out_specs[1] = pl.BlockSpec((1, CHUNK, GATE_WIDTH), lambda i, *_: (i // tiles_per_seq, 0, 0))
        n_dec, n_pages = page_table.shape
        page = cache_k.shape[1]
        pages_per_step = N_PAGE_BATCHES * PAGE_BATCH
        assert n_dec * n_pages == n_steps * pages_per_step and n_pages % pages_per_step == 0
        blocks_per_step = pages_per_step * page // MOBA_BLOCK
        steps_per_seq = n_pages // pages_per_step
        in_specs.append(pl.BlockSpec(memory_space=pl.ANY))
        args = [page_table] + args + [cache_k]
        out_specs.append(pl.BlockSpec((1, blocks_per_step, N_HEADS, HEAD_DIM),
                                      lambda i, *_: (i // steps_per_seq, i % steps_per_seq, 0, 0)))
        out_shape.append(jax.ShapeDtypeStruct((n_dec, n_pages * page // MOBA_BLOCK, N_HEADS, HEAD_DIM), F32))
        scratch += [pltpu.VMEM((N_PAGE_BATCHES, PAGE_BATCH, page, N_HEADS, HEAD_DIM), F32),
                    pltpu.SemaphoreType.DMA((N_PAGE_BATCHES,))]
        n_prefetch = 1
    grid_spec = pltpu.PrefetchScalarGridSpec(num_scalar_prefetch=n_prefetch, grid=(n_steps,), in_specs=in_specs,
                                             out_specs=out_specs, scratch_shapes=scratch)
    return pl.pallas_call(
        functools.partial(_gmlp_kernel, sample, tm),
        grid_spec=grid_spec,
        out_shape=out_shape,
        compiler_params=_params(1),
        name="gmlp_sample" if sample else "gmlp_prompt",
    )(*args)


def _expert_kernel(tmg, n_pairs, n_tiles, exact, te_ref, nv_ref, ts_ref, od_ref, h_hbm, wg_ref, wu_ref, wd_ref, out_hbm,
                   xbuf0, xbuf1, ybuf0, ybuf1, sem_in, sem_out, *bf16_weights):
    i = pl.program_id(0)
    xbufs, ybufs = (xbuf0, xbuf1), (ybuf0, ybuf1)
    rows8 = SUBLANES
    weights = (wg_ref, wu_ref, wd_ref) if exact else bf16_weights

    def valid(t):
        return (t >= 0) & (t < n_tiles) & (nv_ref[jnp.clip(t, 0, n_tiles - 1)] > 0)

    def gather_row(r, p, tok=0):
        return pltpu.make_async_copy(h_hbm.at[pl.ds(pl.multiple_of(tok * rows8, rows8), rows8)],
                                     xbufs[p].at[pl.ds(r * rows8, rows8)], sem_in.at[p])

    def scatter_row(r, p, pair=0):
        row8 = r * rows8 if isinstance(r, int) else pl.multiple_of(r * rows8, rows8)
        return pltpu.make_async_copy(ybufs[p].at[pl.ds(row8, rows8)],
                                     out_hbm.at[pl.ds(pl.multiple_of(pair * rows8, rows8), rows8)], sem_out.at[p])

    def start_gather(t, p):
        base = ts_ref[t]
        for r in range(tmg):
            gather_row(r, p, od_ref[base + r] >> 1).start()

    def for_scatter_rows(t, p, start, all_rows):
        base = ts_ref[t]

        def one(r):
            if start:
                scatter_row(r, p, od_ref[base + r]).start()
            else:
                scatter_row(r, p).wait()

        if all_rows:
            for r in range(tmg):
                one(r)
        else:
            def body(r, carry):
                one(r)
                return carry
            lax.fori_loop(0, jnp.minimum(tmg, n_pairs - base), body, 0)

    def refresh_weights(t):
        if not exact:
            @pl.when((t == 0) | (te_ref[t] != te_ref[jnp.maximum(t - 1, 0)]))
            def _():
                for w_bf16, w_ref in zip(bf16_weights, (wg_ref, wu_ref, wd_ref)):
                    w_bf16[...] = w_ref[...].astype(BF16)

    def compute(p):
        wg, wu, wd = weights
        x = _load_tile_rows(xbufs[p], tmg)
        a = jax.nn.silu(_mm(x, wg[...], exact)) * _mm(x, wu[...], exact)
        _store_tile_rows(ybufs[p], _mm(a, wd[...], exact))

    def step(p, guarded):
        when = pl.when if guarded else (lambda cond: (lambda body: body()))

        @when(valid(i - 1))
        def _():
            for r in range(tmg):
                gather_row(r, 1 - p).wait()
            refresh_weights(i - 1)

        @when(valid(i - 3))
        def _():
            for_scatter_rows(i - 3, 1 - p, False, not guarded)

        @when(valid(i - 2))
        def _():
            for_scatter_rows(i - 2, p, True, not guarded)

        @when(valid(i))
        def _():
            start_gather(i, p)

        @when(valid(i - 1))
        def _():
            compute(1 - p)

    steady = (i >= 3) & valid(i) & (ts_ref[jnp.clip(i - 2, 0, n_tiles - 1)] + tmg <= n_pairs)
    for p in range(2):
        @pl.when(steady & (i % 2 == p))
        def _():
            step(p, False)

        @pl.when(jnp.logical_not(steady) & (i % 2 == p))
        def _():
            step(p, True)


def _dispatch(route, tmg):
    n_pairs = EXPERT_TOPK * route.shape[0]
    e_flat = route[:, 0:EXPERT_TOPK].astype(jnp.int32).reshape(n_pairs)
    n_tiles = -(-(n_pairs + N_EXPERTS * (tmg - 1)) // tmg)
    order = jnp.argsort(e_flat, stable=True).astype(jnp.int32)
    counts = jnp.sum((e_flat[:, None] == jnp.arange(N_EXPERTS, dtype=jnp.int32)[None, :]).astype(jnp.int32), axis=0)
    tiles_per = -(-counts // tmg)
    t_end = jnp.cumsum(tiles_per)
    t_start = t_end - tiles_per
    c_start = jnp.cumsum(counts) - counts
    tile = jnp.arange(n_tiles, dtype=jnp.int32)
    tile_expert = jnp.minimum(jnp.sum((t_end[None, :] <= tile[:, None]).astype(jnp.int32), axis=1), N_EXPERTS - 1)
    first_row = (tile - t_start[tile_expert]) * tmg
    tile_valid = jnp.clip(counts[tile_expert] - first_row, 0, tmg).astype(jnp.int32)
    tile_src = jnp.where(tile_valid > 0, c_start[tile_expert] + first_row, 0).astype(jnp.int32)
    order = jnp.concatenate([order, jnp.zeros((tmg,), jnp.int32)])
    return tile_expert.astype(jnp.int32), tile_valid, tile_src, order


def _experts(hm, route, w, layer, tmg, exact):
    n = route.shape[0]
    w_gate, w_up, w_down = w["moe_w_gate"], w["moe_w_up"], w["moe_w_down"]
    tile_expert, tile_valid, tile_src, order = _dispatch(route, tmg)
    n_tiles = tile_expert.shape[0]
    by_expert = lambda i, te, nv, ts, od: (layer, te[jnp.clip(i - 1, 0, n_tiles - 1)], 0, 0)
    scratch = [pltpu.VMEM((tmg * ROW_PIECES, LANES), F32)] * 4 \
        + [pltpu.SemaphoreType.DMA((2,)), pltpu.SemaphoreType.DMA((2,))]
    if not exact:
        scratch += [pltpu.VMEM((D_MODEL, EXPERT_FF), BF16), pltpu.VMEM((D_MODEL, EXPERT_FF), BF16),
                    pltpu.VMEM((EXPERT_FF, D_MODEL), BF16)]
    grid_spec = pltpu.PrefetchScalarGridSpec(
        num_scalar_prefetch=4,
        grid=(n_tiles + 3,),
        in_specs=[pl.BlockSpec(memory_space=pl.ANY),
                  pl.BlockSpec((None, None, D_MODEL, EXPERT_FF), by_expert),
                  pl.BlockSpec((None, None, D_MODEL, EXPERT_FF), by_expert),
                  pl.BlockSpec((None, None, EXPERT_FF, D_MODEL), by_expert)],
        out_specs=pl.BlockSpec(memory_space=pl.ANY),
        scratch_shapes=scratch,
    )
    return pl.pallas_call(
        functools.partial(_expert_kernel, tmg, EXPERT_TOPK * n, n_tiles, exact),
        grid_spec=grid_spec,
        out_shape=jax.ShapeDtypeStruct((EXPERT_TOPK * n * ROW_PIECES, LANES), F32),
        compiler_params=_params(1),
        name="experts",
    )(tile_expert, tile_valid, tile_src, order, hm, w_gate, w_up, w_down)


def _rope(xh, cos, sin, lane):
    swapped = jnp.where(lane < ROT_HALF, pltpu.roll(xh, LANES - ROT_HALF, 1), pltpu.roll(xh, ROT_HALF, 1))
    return xh * cos + swapped * sin


def _kvq_kernel(prompt, exact, x1_ref, yp_ref, route_ref, kvg_ref, wkv_ref, kg_ref, cos_ref, sin_ref,
                bng_ref, wq_ref, qg_ref, x2_ref, k_ref, v_ref, q_ref, *prompt_refs):
    x2 = _moe_residual(x1_ref[...], yp_ref, route_ref)
    x2_ref[...] = x2
    cos = cos_ref[...]
    sin = sin_ref[...]
    lane = lax.broadcasted_iota(jnp.int32, cos.shape, 1)
    kv = _mm(_rms(x2, kvg_ref[...]), wkv_ref[...], exact)
    for h in range(N_HEADS):
        hs = slice(h * HEAD_DIM, (h + 1) * HEAD_DIM)
        kh = _rope(_rms(kv[:, hs], kg_ref[...]), cos, sin, lane)
        k_ref[:, hs] = kh
        if prompt:
            kb_ref, _, km_ref = prompt_refs
            kb_ref[:, hs] = kh.astype(BF16)
            km_ref[0, :, hs] = jnp.mean(kh, axis=0, keepdims=True)
    v = kv[:, D_MODEL:]
    v_ref[...] = v
    if prompt:
        prompt_refs[1][0] = v.T.astype(BF16)
    q = _mm(_rms(x2, bng_ref[...]), wq_ref[...], exact)
    for h in range(N_HEADS):
        hs = slice(h * HEAD_DIM, (h + 1) * HEAD_DIM)
        q_ref[:, hs] = _rope(_rms(q[:, hs], qg_ref[...]), cos, sin, lane).astype(q_ref.dtype)


def _kvq(x1, ypairs, route, cos_tab, sin_tab, w, prompt, exact):
    n = x1.shape[0]
    mw = w["f32"] if exact else w["bf16"]
    tm = ROW_TILE if prompt else n
    row = lambda i: (i, 0)
    tab_tiles = cos_tab.shape[0] // tm
    tab = lambda i: (i % tab_tiles, 0)
    out_specs = [pl.BlockSpec((tm, D_MODEL), row)] * 4
    out_shape = [jax.ShapeDtypeStruct((n, D_MODEL), F32)] * 3 + [jax.ShapeDtypeStruct((n, D_MODEL), BF16 if prompt else F32)]
    if prompt:
        seq = cos_tab.shape[0]
        out_specs += [pl.BlockSpec((tm, D_MODEL), row),
                      pl.BlockSpec((1, D_MODEL, tm), lambda i: (i // tab_tiles, 0, i % tab_tiles)),
                      pl.BlockSpec((1, 1, D_MODEL), lambda i: (i, 0, 0))]
        out_shape += [jax.ShapeDtypeStruct((n, D_MODEL), BF16), jax.ShapeDtypeStruct((n // seq, D_MODEL, seq), BF16),
                      jax.ShapeDtypeStruct((n // tm, 1, D_MODEL), F32)]
    return pl.pallas_call(
        functools.partial(_kvq_kernel, prompt, exact),
        grid=(n // tm,),
        in_specs=[pl.BlockSpec((tm, D_MODEL), row), pl.BlockSpec((EXPERT_TOPK * tm * ROW_PIECES, LANES), row),
                  pl.BlockSpec((tm, LANES), row),
                  _full((1, D_MODEL)), _full((D_MODEL, 2 * D_MODEL)), _full((1, HEAD_DIM)),
                  pl.BlockSpec((tm, HEAD_DIM), tab), pl.BlockSpec((tm, HEAD_DIM), tab),
                  _full((1, D_MODEL)), _full((D_MODEL, D_MODEL)), _full((1, HEAD_DIM))],
        out_specs=out_specs,
        out_shape=out_shape,
        compiler_params=_params(1),
        name="kvq",
    )(x1, ypairs, route, w["kv_norm_g"], mw["w_kv"], w["k_norm_g"], cos_tab, sin_tab,
      w["b_norm_g"], mw["b_w_q"], w["q_norm_g"])


def _attn_kernel(q_ref, kb_ref, vt_ref, km_ref, x2_ref, wo_ref, mg_ref, wr_ref, br_ref,
                 x3_ref, hm_ref, route_ref, sel_scr, o_scr, *acc_scrs):
    j = pl.program_id(1)
    tq = q_ref.shape[0]
    n_blk = km_ref.shape[1]
    scale_log2e = HEAD_DIM ** -0.5 * 1.4426950408889634
    blk = lax.broadcasted_iota(jnp.int32, (n_blk, tq), 0).astype(F32)
    cur = j.astype(F32)
    contract_last = (((1,), (1,)), ((), ()))
    own = pl.multiple_of(j * MOBA_BLOCK, MOBA_BLOCK)
    chains = [(h, g) for h in range(N_HEADS) for g in range(tq // LANES)]

    def scores(h, g, off):
        hs = slice(h * HEAD_DIM, (h + 1) * HEAD_DIM)
        return lax.dot_general(kb_ref[pl.ds(off, MOBA_BLOCK), hs], q_ref[g * LANES:(g + 1) * LANES, hs],
                               contract_last, preferred_element_type=F32) * scale_log2e

    def weighted_values(h, off, p):
        hs = slice(h * HEAD_DIM, (h + 1) * HEAD_DIM)
        return jnp.dot(vt_ref[0, hs, pl.ds(off, MOBA_BLOCK)], p.astype(BF16), preferred_element_type=F32)

    for h in range(N_HEADS):
        hs = slice(h * HEAD_DIM, (h + 1) * HEAD_DIM)
        gate = lax.dot_general(km_ref[0, :, hs].astype(BF16), q_ref[:, hs], contract_last,
                               preferred_element_type=F32)
        gate = jnp.where(blk < cur, gate, NEG_INF)
        sel = jnp.zeros((n_blk, tq), F32)
        for _ in range(MOBA_TOPK):
            m = jnp.max(gate, axis=0, keepdims=True)
            first = jnp.min(jnp.where(gate == m, blk, float(n_blk)), axis=0, keepdims=True)
            pick = (blk == first) & (m > NEG_INF)
            sel = jnp.where(pick, 1.0, sel)
            gate = jnp.where(pick, NEG_INF, gate)
        sel_scr[h] = sel

    k_pos = lax.broadcasted_iota(jnp.int32, (MOBA_BLOCK, LANES), 0)
    q_pos = lax.broadcasted_iota(jnp.int32, (MOBA_BLOCK, LANES), 1)
    stats = []
    for h, g in chains:
        s = jnp.where(k_pos <= q_pos + g * LANES, scores(h, g, own), NEG_INF)
        m0 = jnp.max(s, axis=0, keepdims=True)
        p = jnp.exp2(s - m0)
        stats += [m0, jnp.sum(p, axis=0, keepdims=True)]
        acc_scrs[h][:, g * LANES:(g + 1) * LANES] = weighted_values(h, own, p)

    def past_block(jj, stats):
        off = pl.multiple_of(jj * MOBA_BLOCK, MOBA_BLOCK)
        new_stats = []
        visible = [sel_scr[h, pl.ds(jj, 1), :] for h in range(N_HEADS)]
        for c, (h, g) in enumerate(chains):
            qs = slice(g * LANES, (g + 1) * LANES)
            m_old, l_old = stats[2 * c], stats[2 * c + 1]
            s = jnp.where(visible[h][:, qs] > 0.0, scores(h, g, off), NEG_INF)
            m_new = jnp.maximum(m_old, jnp.max(s, axis=0, keepdims=True))
            alpha = jnp.exp2(m_old - m_new)
            p = jnp.exp2(s - m_new)
            new_stats += [m_new, alpha * l_old + jnp.sum(p, axis=0, keepdims=True)]
            acc_scrs[h][:, qs] = alpha * acc_scrs[h][:, qs] + weighted_values(h, off, p)
        return tuple(new_stats)

    stats = lax.fori_loop(0, j, past_block, tuple(stats))

    for c, (h, g) in enumerate(chains):
        qs = slice(g * LANES, (g + 1) * LANES)
        o_scr[qs, h * HEAD_DIM:(h + 1) * HEAD_DIM] = (acc_scrs[h][:, qs] / stats[2 * c + 1]).T.astype(BF16)

    x3 = x2_ref[...] + jnp.dot(o_scr[...], wo_ref[...], preferred_element_type=F32)
    x3_ref[...] = x3
    _moe_prenorm_and_route(x3, mg_ref, wr_ref, br_ref, hm_ref, route_ref, False)


def _attention_prompt(q, kb, vt, kmean, x2, w, seq):
    n = q.shape[0]
    bsz = n // seq
    n_qb = seq // MOBA_BLOCK
    row = lambda b, j: (b * n_qb + j, 0)
    return pl.pallas_call(
        _attn_kernel,
        grid=(bsz, n_qb),
        in_specs=[pl.BlockSpec((MOBA_BLOCK, D_MODEL), row),
                  pl.BlockSpec((seq, D_MODEL), lambda b, j: (b, 0)),
                  pl.BlockSpec((1, D_MODEL, seq), lambda b, j: (b, 0, 0)),
                  pl.BlockSpec((1, n_qb, D_MODEL), lambda b, j: (b, 0, 0)),
                  pl.BlockSpec((MOBA_BLOCK, D_MODEL), row),
                  _full((D_MODEL, D_MODEL)), _full((1, D_MODEL)), _full((D_MODEL, LANES)), _full((1, LANES))],
        out_specs=[pl.BlockSpec((MOBA_BLOCK, D_MODEL), row), pl.BlockSpec((MOBA_BLOCK * ROW_PIECES, LANES), row),
                   pl.BlockSpec((MOBA_BLOCK, LANES), row)],
        out_shape=[jax.ShapeDtypeStruct((n, D_MODEL), F32), jax.ShapeDtypeStruct((n * ROW_PIECES, LANES), F32),
                   jax.ShapeDtypeStruct((n, LANES), F32)],
        scratch_shapes=[pltpu.VMEM((N_HEADS, n_qb, MOBA_BLOCK), F32),
                        pltpu.VMEM((MOBA_BLOCK, D_MODEL), BF16)]
                       + [pltpu.VMEM((HEAD_DIM, MOBA_BLOCK), F32)] * N_HEADS,
        compiler_params=_params(2),
        name="moba_prompt",
    )(q, kb, vt, kmean.reshape(bsz, n_qb, D_MODEL), x2, w["bf16"]["b_w_o"],
      w["moe_norm_g"][1], w["bf16"]["moe_w_route"][1], w["moe_b_route"][1])


def _block_select_kernel(q_ref, km_ref, sel_ref):
    gate = jnp.sum(km_ref[0] * q_ref[...], axis=-1, keepdims=True)
    n_blk = gate.shape[0]
    blk = lax.broadcasted_iota(jnp.int32, gate.shape, 0).astype(F32)
    rank = lax.broadcasted_iota(jnp.int32, sel_ref.shape[1:], 1)
    out = jnp.zeros(sel_ref.shape[1:], F32)
    for r in range(MOBA_TOPK):
        m = jnp.max(gate, axis=0, keepdims=True)
        first = jnp.min(jnp.where(gate == m, blk, float(n_blk)), axis=0, keepdims=True)
        gate = jnp.where(blk == first, NEG_INF, gate)
        out = jnp.where(rank == r, first[0], out)
    sel_ref[0] = out.astype(jnp.int32)


def _block_select(q4, kmean_s):
    n_seq, n_blk = kmean_s.shape[:2]
    return pl.pallas_call(
        _block_select_kernel,
        grid=(n_seq,),
        in_specs=[pl.BlockSpec((1, N_HEADS, HEAD_DIM), lambda s: (s, 0, 0)),
                  pl.BlockSpec((1, n_blk, N_HEADS, HEAD_DIM), lambda s: (s, 0, 0, 0))],
        out_specs=pl.BlockSpec((1, N_HEADS, LANES), lambda s: (s, 0, 0)),
        out_shape=jax.ShapeDtypeStruct((n_seq, N_HEADS, LANES), jnp.int32),
        compiler_params=_params(1),
        name="block_select",
    )(q4, kmean_s)


def _attn_sample_kernel(n_sel, pg_ref, q_ref, k_own_ref, v_own_ref, ck_hbm, cv_hbm, o_ref, kbuf, vbuf, sem):
    s = pl.program_id(0)

    def page_copies(seq, slot):
        cps = []
        for h in range(N_HEADS):
            for i in range(n_sel):
                pg = pg_ref[(seq * N_HEADS + h) * n_sel + i]
                cps.append(pltpu.make_async_copy(ck_hbm.at[pg, :, h, :], kbuf.at[slot, h, i], sem.at[slot]))
                cps.append(pltpu.make_async_copy(cv_hbm.at[pg, :, h, :], vbuf.at[slot, h, i], sem.at[slot]))
        return cps

    @pl.when(s == 0)
    def _():
        for cp in page_copies(0, 0):
            cp.start()

    @pl.when(s + 1 < pl.num_programs(0))
    def _():
        for cp in page_copies(s + 1, (s + 1) % 2):
            cp.start()

    slot = s % 2
    for cp in page_copies(s, slot):
        cp.wait()

    scale = HEAD_DIM ** -0.5
    for h in range(N_HEADS):
        hs = slice(h * HEAD_DIM, (h + 1) * HEAD_DIM)
        q = q_ref[0, :, hs]
        s_own = jnp.sum(k_own_ref[0, :, hs] * q, axis=-1, keepdims=True) * scale
        scores = [jnp.sum(kbuf[slot, h, i] * q, axis=-1, keepdims=True) * scale for i in range(n_sel)]
        m = s_own
        for sc in scores:
            m = jnp.maximum(m, jnp.max(sc, axis=0, keepdims=True))
        p_own = jnp.exp(s_own - m)
        l = p_own
        acc = p_own * v_own_ref[0, :, hs]
        for i, sc in enumerate(scores):
            p = jnp.exp(sc - m)
            l = l + jnp.sum(p, axis=0, keepdims=True)
            acc = acc + jnp.sum(p * vbuf[slot, h, i], axis=0, keepdims=True)
        o_ref[0, :, hs] = acc / l


def _attention_sample(q3, k3, v3, cache_k, cache_v, sel_pages):
    n_seq = q3.shape[0]
    page = cache_k.shape[1]
    n_sel = sel_pages.shape[-1]
    seq_row = lambda s, pg: (s, 0, 0)
    grid_spec = pltpu.PrefetchScalarGridSpec(
        num_scalar_prefetch=1,
        grid=(n_seq,),
        in_specs=[pl.BlockSpec((1, 1, D_MODEL), seq_row)] * 3
                 + [pl.BlockSpec(memory_space=pl.ANY), pl.BlockSpec(memory_space=pl.ANY)],
        out_specs=pl.BlockSpec((1, 1, D_MODEL), seq_row),
        scratch_shapes=[pltpu.VMEM((2, N_HEADS, n_sel, page, HEAD_DIM), F32),
                        pltpu.VMEM((2, N_HEADS, n_sel, page, HEAD_DIM), F32),
                        pltpu.SemaphoreType.DMA((2,))],
    )
    return pl.pallas_call(
        functools.partial(_attn_sample_kernel, n_sel),
        grid_spec=grid_spec,
        out_shape=jax.ShapeDtypeStruct((n_seq, 1, D_MODEL), F32),
        compiler_params=_params(1),
        name="moba_sample",
    )(sel_pages.reshape(-1), q3, k3, v3, cache_k, cache_v)


def _oproj_kernel(o_ref, x2_ref, wo_ref, mg_ref, wr_ref, br_ref, x3_ref, hm_ref, route_ref):
    x3 = x2_ref[...] + _mm(o_ref[...], wo_ref[...], True)
    x3_ref[...] = x3
    _moe_prenorm_and_route(x3, mg_ref, wr_ref, br_ref, hm_ref, route_ref, True)


def _oproj(o, x2, w):
    n = o.shape[0]
    return pl.pallas_call(
        _oproj_kernel,
        grid=(1,),
        in_specs=[_full((n, D_MODEL)), _full((n, D_MODEL)), _full((D_MODEL, D_MODEL)), _full((1, D_MODEL)),
                  _full((D_MODEL, LANES)), _full((1, LANES))],
        out_specs=[_full((n, D_MODEL)), _full((n * ROW_PIECES, LANES)), _full((n, LANES))],
        out_shape=[jax.ShapeDtypeStruct((n, D_MODEL), F32), jax.ShapeDtypeStruct((n * ROW_PIECES, LANES), F32),
                   jax.ShapeDtypeStruct((n, LANES), F32)],
        compiler_params=_params(1),
        name="oproj_sample",
    )(o, x2, w["f32"]["b_w_o"], w["moe_norm_g"][1], w["f32"]["moe_w_route"][1], w["moe_b_route"][1])


def _combine_kernel(x_ref, yp_ref, route_ref, o_ref):
    o_ref[...] = _moe_residual(x_ref[...], yp_ref, route_ref)


def _combine(x, ypairs, route):
    n = x.shape[0]
    tm = min(n, 2 * ROW_TILE)
    row = lambda i: (i, 0)
    return pl.pallas_call(
        _combine_kernel,
        grid=(n // tm,),
        in_specs=[pl.BlockSpec((tm, D_MODEL), row), pl.BlockSpec((EXPERT_TOPK * tm * ROW_PIECES, LANES), row),
                  pl.BlockSpec((tm, LANES), row)],
        out_specs=pl.BlockSpec((tm, D_MODEL), row),
        out_shape=jax.ShapeDtypeStruct((n, D_MODEL), F32),
        compiler_params=_params(1),
        name="combine",
    )(x, ypairs, route)


def _rope_tables(pos):
    inv_freq = ROPE_THETA ** (-jnp.arange(ROT_HALF, dtype=F32) / ROT_HALF)
    ang = pos.astype(F32)[:, None] * inv_freq[None, :]
    cos, sin = jnp.cos(ang), jnp.sin(ang)
    rest = HEAD_DIM - ROT_DIM
    cos_tab = jnp.concatenate([cos, cos, jnp.ones((pos.shape[0], rest), F32)], axis=-1)
    sin_tab = jnp.concatenate([-sin, sin, jnp.zeros((pos.shape[0], rest), F32)], axis=-1)
    return cos_tab, sin_tab


def kernel(x_prompt, x_sample, cache_k, cache_v, page_table, a_norm_g, a_w_in, a_ln_g, a_ln_b, a_w_s, a_b_s, a_w_out, kv_norm_g, w_kv, k_norm_g, b_norm_g, b_w_q, q_norm_g, b_w_o, moe_norm_g, moe_w_rg, moe_b_rg, moe_w_re, moe_b_re, moe_w_gate, moe_w_up, moe_w_down):
    bsz, seq, _ = x_prompt.shape
    n_dec = x_sample.shape[0]
    assert x_sample.shape[1] == 1 and a_w_in.shape[0] == 1 and b_w_q.shape[0] == 1
    assert seq % ROW_TILE == 0 and ROW_TILE == MOBA_BLOCK
    page = cache_k.shape[1]
    past = page_table.shape[1] * page
    assert past % MOBA_BLOCK == 0 and past // MOBA_BLOCK >= MOBA_TOPK and MOBA_BLOCK % page == 0

    depth = moe_norm_g.shape[0]
    pad = LANES - N_EXPERT_GROUPS - N_EXPERTS
    mats = {
        "a_w_in": a_w_in[0], "a_w_out": a_w_out[0], "w_kv": w_kv, "b_w_q": b_w_q[0], "b_w_o": b_w_o[0],
        "moe_w_route": jnp.concatenate([moe_w_rg, moe_w_re, jnp.zeros((depth, D_MODEL, pad), F32)], axis=-1),
    }
    w = {
        "f32": mats, "bf16": {name: m.astype(BF16) for name, m in mats.items()},
        "moe_w_gate": moe_w_gate, "moe_w_up": moe_w_up, "moe_w_down": moe_w_down,
        "a_norm_g": a_norm_g[0][None], "a_ln_g": a_ln_g[0][None], "a_ln_b": a_ln_b[0][None],
        "a_w_s": a_w_s[0], "a_b_s_t": a_b_s[0].T,
        "a_ws0": jnp.repeat(a_w_s[0, :, 0, 0], GROUP_WIDTH)[None], "a_bs0": jnp.repeat(a_b_s[0, :, 0], GROUP_WIDTH)[None],
        "kv_norm_g": kv_norm_g[None], "k_norm_g": k_norm_g[None],
        "b_norm_g": b_norm_g[0][None], "q_norm_g": q_norm_g[0][None],
        "moe_norm_g": moe_norm_g[:, None, :],
        "moe_b_route": jnp.concatenate([moe_b_rg, moe_b_re, jnp.zeros((depth, pad), F32)], axis=-1)[:, None, :],
    }

    xp = x_prompt.reshape(bsz * seq, D_MODEL)
    xs = x_sample.reshape(n_dec, D_MODEL)
    cos_p, sin_p = _rope_tables(jnp.arange(seq, dtype=jnp.int32))
    cos_s, sin_s = _rope_tables(jnp.full((n_dec,), past, dtype=jnp.int32))

    x1p, vrows_p, hmp, route0p, kmean_s = _gmlp(xp, w, False, seq, cache_k, page_table)
    x1s, vrows_s, hms, route0s = _gmlp(xs, w, True, seq)
    yp = _experts(hmp, route0p, w, 0, EXPERT_TILE, False)
    ys = _experts(hms, route0s, w, 0, EXPERT_TILE_SMALL, True)

    x2p, k_p, v_p, q_p, kb_p, vt_p, kmean_p = _kvq(x1p, yp, route0p, cos_p, sin_p, w, True, False)
    x2s, k_s, v_s, q_s = _kvq(x1s, ys, route0s, cos_s, sin_s, w, False, True)

    x3p, hmp, route1p = _attention_prompt(q_p, kb_p, vt_p, kmean_p, x2p, w, seq)
    sel_blocks = _block_select(q_s.reshape(n_dec, N_HEADS, HEAD_DIM), kmean_s)[:, :, :MOBA_TOPK]
    pages_per_block = MOBA_BLOCK // page
    page_slots = sel_blocks[..., None] * pages_per_block + jnp.arange(pages_per_block, dtype=jnp.int32)
    page_slots = page_slots.reshape(n_dec, N_HEADS * MOBA_TOPK * pages_per_block)
    sel_pages = jnp.take_along_axis(page_table, page_slots, axis=1).reshape(n_dec, N_HEADS, MOBA_TOPK * pages_per_block)
    o_s = _attention_sample(q_s.reshape(n_dec, 1, D_MODEL), k_s.reshape(n_dec, 1, D_MODEL),
                            v_s.reshape(n_dec, 1, D_MODEL), cache_k, cache_v, sel_pages)
    x3s, hms, route1s = _oproj(o_s.reshape(n_dec, D_MODEL), x2s, w)
    yp = _experts(hmp, route1p, w, 1, EXPERT_TILE, False)
    ys = _experts(hms, route1s, w, 1, EXPERT_TILE_SMALL, True)
    y_prompt = _combine(x3p, yp, route1p)
    y_sample = _combine(x3s, ys, route1s)

    return (y_prompt.reshape(bsz, seq, D_MODEL), y_sample.reshape(n_dec, 1, D_MODEL),
            vrows_p.reshape(1, bsz, CHUNK, GATE_WIDTH), vrows_s.reshape(1, n_dec, 1, GATE_WIDTH),
            k_p.reshape(bsz, seq, N_HEADS, HEAD_DIM), v_p.reshape(bsz, seq, N_HEADS, HEAD_DIM),
            k_s.reshape(n_dec, 1, N_HEADS, HEAD_DIM), v_s.reshape(n_dec, 1, N_HEADS, HEAD_DIM))
```

```python
import functools

import jax
import jax.numpy as jnp
from jax import lax
from jax.experimental import pallas as pl
from jax.experimental.pallas import tpu as pltpu

D_MODEL = 1024
CHUNK = 128
GATE_WIDTH = 2 * D_MODEL
N_GATE_GROUPS = 8
GROUP_WIDTH = GATE_WIDTH // N_GATE_GROUPS
HEAD_DIM = 128
N_HEADS = D_MODEL // HEAD_DIM
ROT_DIM = HEAD_DIM // 4
ROT_HALF = ROT_DIM // 2
ROPE_THETA = 500000.0
MOBA_BLOCK = 256
MOBA_TOPK = 3
N_EXPERT_GROUPS = 4
EXPERTS_PER_GROUP = 4
N_EXPERTS = N_EXPERT_GROUPS * EXPERTS_PER_GROUP
EXPERT_TOPK = 2
EXPERT_FF = D_MODEL // 2
EPS = 1e-6

LANES = 128
SUBLANES = 8
ROW_PIECES = D_MODEL // LANES
ROW_TILE = 256
EXPERT_TILE = 256
EXPERT_TILE_SMALL = 16
IN_PROJ_COLS = 512
N_IN_PROJ_CHUNKS = 2 * GATE_WIDTH // IN_PROJ_COLS
PAGE_BATCH = 16
N_PAGE_BATCHES = 2
VMEM_LIMIT = 60 * 1024 * 1024

F32 = jnp.float32
BF16 = jnp.bfloat16
NEG_INF = float("-inf")

assert ROW_PIECES == SUBLANES


def _params(n_axes):
    return pltpu.CompilerParams(dimension_semantics=("arbitrary",) * n_axes, vmem_limit_bytes=VMEM_LIMIT)


def _rms(x, g):
    return x * lax.rsqrt(jnp.mean(x * x, axis=-1, keepdims=True) + EPS) * g


def _gelu_tanh(z):
    c = 0.7978845608028654
    half_z = 0.5 * z
    return half_z + half_z * jnp.tanh(z * (c + (c * 0.044715) * (z * z)))


def _mm(a, b, exact):
    if exact:
        return jnp.dot(a, b, precision=lax.Precision.HIGHEST, preferred_element_type=F32)
    return jnp.dot(a.astype(BF16), b, preferred_element_type=F32)


def _full(shape):
    nd = len(shape)
    return pl.BlockSpec(shape, lambda *_: (0,) * nd, pipeline_mode=pl.Buffered(1))


def _store_tile_rows(ref, x):
    for s in range(ROW_PIECES):
        ref[pl.ds(s, x.shape[0], stride=ROW_PIECES), :] = x[:, s * LANES:(s + 1) * LANES]


def _load_tile_rows(ref, rows, first=0, step=ROW_PIECES):
    return jnp.concatenate([ref[pl.ds(first + s, rows, stride=step), :] for s in range(ROW_PIECES)], axis=-1)


def _route(hm, wr_ref, br_ref, exact):
    logits = _mm(hm, wr_ref[...], exact) + br_ref[...]
    lane = lax.broadcasted_iota(jnp.int32, logits.shape, 1).astype(F32)
    lg = jnp.where(lane < N_EXPERT_GROUPS, logits, NEG_INF)
    mg = jnp.max(lg, axis=-1, keepdims=True)
    g_sel = jnp.min(jnp.where(lg == mg, lane, float(LANES)), axis=-1, keepdims=True)
    p_grp = 1.0 / jnp.sum(jnp.exp(lg - mg), axis=-1, keepdims=True)
    lo = N_EXPERT_GROUPS + g_sel * EXPERTS_PER_GROUP
    le = jnp.where((lane >= lo) & (lane < lo + EXPERTS_PER_GROUP), logits, NEG_INF)
    v1 = jnp.max(le, axis=-1, keepdims=True)
    i1 = jnp.min(jnp.where(le == v1, lane, float(LANES)), axis=-1, keepdims=True)
    le2 = jnp.where(lane == i1, NEG_INF, le)
    v2 = jnp.max(le2, axis=-1, keepdims=True)
    i2 = jnp.min(jnp.where(le2 == v2, lane, float(LANES)), axis=-1, keepdims=True)
    t = jnp.exp(v2 - v1)
    w1 = p_grp / (1.0 + t)
    w2 = p_grp * t / (1.0 + t)
    out = jnp.where(lane == 0.0, i1 - N_EXPERT_GROUPS, 0.0)
    out = jnp.where(lane == 1.0, i2 - N_EXPERT_GROUPS, out)
    out = jnp.where(lane == 2.0, w1, out)
    out = jnp.where(lane == 3.0, w2, out)
    return out


def _moe_prenorm_and_route(x, mg_ref, wr_ref, br_ref, hm_ref, route_ref, exact):
    hm = _rms(x, mg_ref[...])
    _store_tile_rows(hm_ref, hm)
    route_ref[...] = _route(hm, wr_ref, br_ref, exact)


def _moe_residual(x, y_ref, route_ref):
    rows = x.shape[0]
    step = EXPERT_TOPK * ROW_PIECES
    y1 = _load_tile_rows(y_ref, rows, 0, step)
    y2 = _load_tile_rows(y_ref, rows, ROW_PIECES, step)
    return x + (route_ref[:, 2:3] * y1 + route_ref[:, 3:4] * y2)


def _page_batch_copies(pt_ref, ck_hbm, pbuf, psem, step, k):
    n_pages = pt_ref.shape[1]
    first = (step * N_PAGE_BATCHES + k) * PAGE_BATCH
    seq, page0 = first // n_pages, first % n_pages
    return [pltpu.make_async_copy(ck_hbm.at[pt_ref[seq, page0 + p]], pbuf.at[k, p], psem.at[k])
            for p in range(PAGE_BATCH)]


def _reduce_page_batch(pbuf, km_ref, k, pages_per_block):
    blocks = PAGE_BATCH // pages_per_block
    for b in range(blocks):
        tot = jnp.sum(pbuf[k, b * pages_per_block], axis=0)
        for p in range(1, pages_per_block):
            tot = tot + jnp.sum(pbuf[k, b * pages_per_block + p], axis=0)
        km_ref[0, k * blocks + b] = tot / MOBA_BLOCK


def _gmlp_kernel(sample, tm, *refs):
    if sample:
        (x_ref, ng_ref, win_ref, lng_ref, lnb_ref, ws_ref, bs_ref, wout_ref, mg_ref, wr_ref, br_ref,
         x1_ref, vrows_ref, hm_ref, route_ref, u_scr, v_scr, gated_scr) = refs
    else:
        (pt_ref, x_ref, ng_ref, win_ref, lng_ref, lnb_ref, ws_ref, bs_ref, wout_ref, mg_ref, wr_ref, br_ref, ck_hbm,
         x1_ref, vrows_ref, hm_ref, route_ref, km_ref, u_scr, v_scr, gated_scr, pbuf, psem) = refs
        step = pl.program_id(0)
        batch = functools.partial(_page_batch_copies, pt_ref, ck_hbm, pbuf, psem)
        pages_per_block = MOBA_BLOCK // ck_hbm.shape[1]

        @pl.when(step == 0)
        def _():
            for cp in batch(step, 0):
                cp.start()

        for cp in batch(step, 1):
            cp.start()

    x = x_ref[...]
    h = _rms(x, ng_ref[...])
    if not sample:
        h = h.astype(BF16)
    n_col = N_IN_PROJ_CHUNKS
    n_ucol = GATE_WIDTH // IN_PROJ_COLS
    vsum = jnp.zeros((tm, 1), F32)
    for c in range(n_col):
        z = _mm(h, win_ref[:, c * IN_PROJ_COLS:(c + 1) * IN_PROJ_COLS], sample)
        z = _gelu_tanh(z)
        if c < n_ucol:
            u_scr[:, c * IN_PROJ_COLS:(c + 1) * IN_PROJ_COLS] = z
        else:
            v_scr[:, (c - n_ucol) * IN_PROJ_COLS:(c - n_ucol + 1) * IN_PROJ_COLS] = z
            vsum = vsum + jnp.sum(z, axis=-1, keepdims=True)

    if not sample:
        for cp in batch(step, 0):
            cp.wait()
        _reduce_page_batch(pbuf, km_ref, 0, pages_per_block)

        @pl.when(step + 1 < pl.num_programs(0))
        def _():
            for cp in batch(step + 1, 0):
                cp.start()

    mean = vsum / GATE_WIDTH
    vss = jnp.zeros((tm, 1), F32)
    for c in range(n_ucol):
        xc = v_scr[:, c * IN_PROJ_COLS:(c + 1) * IN_PROJ_COLS] - mean
        vss = vss + jnp.sum(xc * xc, axis=-1, keepdims=True)
    rstd = lax.rsqrt(vss / GATE_WIDTH + EPS)
    for c in range(n_ucol):
        cs = slice(c * IN_PROJ_COLS, (c + 1) * IN_PROJ_COLS)
        v_scr[:, cs] = (v_scr[:, cs] - mean) * rstd * lng_ref[:, cs] + lnb_ref[:, cs]

    if sample:
        vrows_ref[...] = v_scr[...]
        gated_scr[...] = u_scr[...] * (v_scr[...] * ws_ref[...] + bs_ref[...])
    else:
        vrows_ref[0] = v_scr[tm - CHUNK:tm, :]
        t_out = lax.broadcasted_iota(jnp.int32, (CHUNK, CHUNK), 0)
        s_in = lax.broadcasted_iota(jnp.int32, (CHUNK, CHUNK), 1)
        for g in range(N_GATE_GROUPS):
            gs = slice(g * GROUP_WIDTH, (g + 1) * GROUP_WIDTH)
            wsg = jnp.where(s_in <= t_out, ws_ref[g], 0.0).astype(BF16)
            bias = bs_ref[:, g:g + 1]
            for c in range(tm // CHUNK):
                rs = slice(c * CHUNK, (c + 1) * CHUNK)
                mixed = jnp.dot(wsg, v_scr[rs, gs].astype(BF16), preferred_element_type=F32) + bias
                gated_scr[rs, gs] = (u_scr[rs, gs] * mixed).astype(BF16)

    x1 = x + _mm(gated_scr[...], wout_ref[...], sample)
    x1_ref[...] = x1
    if not sample:
        for cp in batch(step, 1):
            cp.wait()
        _reduce_page_batch(pbuf, km_ref, 1, pages_per_block)
    _moe_prenorm_and_route(x1, mg_ref, wr_ref, br_ref, hm_ref, route_ref, sample)


def _gmlp(x, w, sample, seq, cache_k=None, page_table=None):
    n = x.shape[0]
    mw = w["f32"] if sample else w["bf16"]
    tm = n if sample else ROW_TILE
    n_steps = n // tm
    row = lambda i, *_: (i, 0)
    in_specs = [pl.BlockSpec((tm, D_MODEL), row), _full((1, D_MODEL)), _full((D_MODEL, 2 * GATE_WIDTH)),
                _full((1, GATE_WIDTH)), _full((1, GATE_WIDTH)), None, None, _full((GATE_WIDTH, D_MODEL)),
                _full((1, D_MODEL)), _full((D_MODEL, LANES)), _full((1, LANES))]
    out_specs = [pl.BlockSpec((tm, D_MODEL), row), None, pl.BlockSpec((tm * ROW_PIECES, LANES), row),
                 pl.BlockSpec((tm, LANES), row)]
    out_shape = [jax.ShapeDtypeStruct((n, D_MODEL), F32), None,
                 jax.ShapeDtypeStruct((n * ROW_PIECES, LANES), F32), jax.ShapeDtypeStruct((n, LANES), F32)]
    scratch = [pltpu.VMEM((tm, GATE_WIDTH), F32), pltpu.VMEM((tm, GATE_WIDTH), F32),
               pltpu.VMEM((tm, GATE_WIDTH), F32 if sample else BF16)]
    args = [x, w["a_norm_g"], mw["a_w_in"], w["a_ln_g"], w["a_ln_b"],
            w["a_ws0"] if sample else w["a_w_s"], w["a_bs0"] if sample else w["a_b_s_t"], mw["a_w_out"],
            w["moe_norm_g"][0], mw["moe_w_route"][0], w["moe_b_route"][0]]
    if sample:
        in_specs[5], in_specs[6] = _full((1, GATE_WIDTH)), _full((1, GATE_WIDTH))
        out_shape[1] = jax.ShapeDtypeStruct((n, GATE_WIDTH), F32)
        out_specs[1] = pl.BlockSpec((tm, GATE_WIDTH), row)
        n_prefetch = 0
    else:
        in_specs[5], in_specs[6] = _full((N_GATE_GROUPS, CHUNK, CHUNK)), _full((CHUNK, N_GATE_GROUPS))
        tiles_per_seq = seq // tm
        out_shape[1] = jax.ShapeDtypeStruct((n // seq, CHUNK, GATE_WIDTH), F32)
        out_specs[1] = pl.BlockSpec((1, CHUNK, GATE_WIDTH), lambda i, *_: (i // tiles_per_seq, 0, 0))
        n_dec, n_pages = page_table.shape
        page = cache_k.shape[1]
        pages_per_step = N_PAGE_BATCHES * PAGE_BATCH
        assert n_dec * n_pages == n_steps * pages_per_step and n_pages % pages_per_step == 0
        blocks_per_step = pages_per_step * page // MOBA_BLOCK
        steps_per_seq = n_pages // pages_per_step
        in_specs.append(pl.BlockSpec(memory_space=pl.ANY))
        args = [page_table] + args + [cache_k]
        out_specs.append(pl.BlockSpec((1, blocks_per_step, N_HEADS, HEAD_DIM),
                                      lambda i, *_: (i // steps_per_seq, i % steps_per_seq, 0, 0)))
        out_shape.append(jax.ShapeDtypeStruct((n_dec, n_pages * page // MOBA_BLOCK, N_HEADS, HEAD_DIM), F32))
        scratch += [pltpu.VMEM((N_PAGE_BATCHES, PAGE_BATCH, page, N_HEADS, HEAD_DIM), F32),
                    pltpu.SemaphoreType.DMA((N_PAGE_BATCHES,))]
        n_prefetch = 1
    grid_spec = pltpu.PrefetchScalarGridSpec(num_scalar_prefetch=n_prefetch, grid=(n_steps,), in_specs=in_specs,
                                             out_specs=out_specs, scratch_shapes=scratch)
    return pl.pallas_call(
        functools.partial(_gmlp_kernel, sample, tm),
        grid_spec=grid_spec,
        out_shape=out_shape,
        compiler_params=_params(1),
        name="gmlp_sample" if sample else "gmlp_prompt",
    )(*args)


def _expert_kernel(tmg, n_pairs, n_tiles, exact, te_ref, nv_ref, ts_ref, od_ref, h_hbm, wg_ref, wu_ref, wd_ref, out_hbm,
                   xbuf0, xbuf1, ybuf0, ybuf1, sem_in, sem_out, *bf16_weights):
    i = pl.program_id(0)
    xbufs, ybufs = (xbuf0, xbuf1), (ybuf0, ybuf1)
    rows8 = SUBLANES
    weights = (wg_ref, wu_ref, wd_ref) if exact else bf16_weights

    def valid(t):
        return (t >= 0) & (t < n_tiles) & (nv_ref[jnp.clip(t, 0, n_tiles - 1)] > 0)

    def gather_row(r, p, tok=0):
        return pltpu.make_async_copy(h_hbm.at[pl.ds(pl.multiple_of(tok * rows8, rows8), rows8)],
                                     xbufs[p].at[pl.ds(r * rows8, rows8)], sem_in.at[p])

    def scatter_row(r, p, pair=0):
        row8 = r * rows8 if isinstance(r, int) else pl.multiple_of(r * rows8, rows8)
        return pltpu.make_async_copy(ybufs[p].at[pl.ds(row8, rows8)],
                                     out_hbm.at[pl.ds(pl.multiple_of(pair * rows8, rows8), rows8)], sem_out.at[p])

    def start_gather(t, p):
        base = ts_ref[t]
        for r in range(tmg):
            gather_row(r, p, od_ref[base + r] >> 1).start()

    def for_scatter_rows(t, p, start, all_rows):
        base = ts_ref[t]

        def one(r):
            if start:
                scatter_row(r, p, od_ref[base + r]).start()
            else:
                scatter_row(r, p).wait()

        if all_rows:
            for r in range(tmg):
                one(r)
        else:
            def body(r, carry):
                one(r)
                return carry
            lax.fori_loop(0, jnp.minimum(tmg, n_pairs - base), body, 0)

    def refresh_weights(t):
        if not exact:
            @pl.when((t == 0) | (te_ref[t] != te_ref[jnp.maximum(t - 1, 0)]))
            def _():
                for w_bf16, w_ref in zip(bf16_weights, (wg_ref, wu_ref, wd_ref)):
                    w_bf16[...] = w_ref[...].astype(BF16)

    def compute(p):
        wg, wu, wd = weights
        x = _load_tile_rows(xbufs[p], tmg)
        a = jax.nn.silu(_mm(x, wg[...], exact)) * _mm(x, wu[...], exact)
        _store_tile_rows(ybufs[p], _mm(a, wd[...], exact))

    def step(p, guarded):
        when = pl.when if guarded else (lambda cond: (lambda body: body()))

        @when(valid(i - 1))
        def _():
            for r in range(tmg):
                gather_row(r, 1 - p).wait()
            refresh_weights(i - 1)

        @when(valid(i - 3))
        def _():
            for_scatter_rows(i - 3, 1 - p, False, not guarded)

        @when(valid(i - 2))
        def _():
            for_scatter_rows(i - 2, p, True, not guarded)

        @when(valid(i))
        def _():
            start_gather(i, p)

        @when(valid(i - 1))
        def _():
            compute(1 - p)

    steady = (i >= 3) & valid(i) & (ts_ref[jnp.clip(i - 2, 0, n_tiles - 1)] + tmg <= n_pairs)
    for p in range(2):
        @pl.when(steady & (i % 2 == p))
        def _():
            step(p, False)

        @pl.when(jnp.logical_not(steady) & (i % 2 == p))
        def _():
            step(p, True)


def _dispatch(route, tmg):
    n_pairs = EXPERT_TOPK * route.shape[0]
    e_flat = route[:, 0:EXPERT_TOPK].astype(jnp.int32).reshape(n_pairs)
    n_tiles = -(-(n_pairs + N_EXPERTS * (tmg - 1)) // tmg)
    order = jnp.argsort(e_flat, stable=True).astype(jnp.int32)
    counts = jnp.sum((e_flat[:, None] == jnp.arange(N_EXPERTS, dtype=jnp.int32)[None, :]).astype(jnp.int32), axis=0)
    tiles_per = -(-counts // tmg)
    t_end = jnp.cumsum(tiles_per)
    t_start = t_end - tiles_per
    c_start = jnp.cumsum(counts) - counts
    tile = jnp.arange(n_tiles, dtype=jnp.int32)
    tile_expert = jnp.minimum(jnp.sum((t_end[None, :] <= tile[:, None]).astype(jnp.int32), axis=1), N_EXPERTS - 1)
    first_row = (tile - t_start[tile_expert]) * tmg
    tile_valid = jnp.clip(counts[tile_expert] - first_row, 0, tmg).astype(jnp.int32)
    tile_src = jnp.where(tile_valid > 0, c_start[tile_expert] + first_row, 0).astype(jnp.int32)
    order = jnp.concatenate([order, jnp.zeros((tmg,), jnp.int32)])
    return tile_expert.astype(jnp.int32), tile_valid, tile_src, order


def _experts(hm, route, w, layer, tmg, exact):
    n = route.shape[0]
    w_gate, w_up, w_down = w["moe_w_gate"], w["moe_w_up"], w["moe_w_down"]
    tile_expert, tile_valid, tile_src, order = _dispatch(route, tmg)
    n_tiles = tile_expert.shape[0]
    by_expert = lambda i, te, nv, ts, od: (layer, te[jnp.clip(i - 1, 0, n_tiles - 1)], 0, 0)
    scratch = [pltpu.VMEM((tmg * ROW_PIECES, LANES), F32)] * 4 \
        + [pltpu.SemaphoreType.DMA((2,)), pltpu.SemaphoreType.DMA((2,))]
    if not exact:
        scratch += [pltpu.VMEM((D_MODEL, EXPERT_FF), BF16), pltpu.VMEM((D_MODEL, EXPERT_FF), BF16),
                    pltpu.VMEM((EXPERT_FF, D_MODEL), BF16)]
    grid_spec = pltpu.PrefetchScalarGridSpec(
        num_scalar_prefetch=4,
        grid=(n_tiles + 3,),
        in_specs=[pl.BlockSpec(memory_space=pl.ANY),
                  pl.BlockSpec((None, None, D_MODEL, EXPERT_FF), by_expert),
                  pl.BlockSpec((None, None, D_MODEL, EXPERT_FF), by_expert),
                  pl.BlockSpec((None, None, EXPERT_FF, D_MODEL), by_expert)],
        out_specs=pl.BlockSpec(memory_space=pl.ANY),
        scratch_shapes=scratch,
    )
    return pl.pallas_call(
        functools.partial(_expert_kernel, tmg, EXPERT_TOPK * n, n_tiles, exact),
        grid_spec=grid_spec,
        out_shape=jax.ShapeDtypeStruct((EXPERT_TOPK * n * ROW_PIECES, LANES), F32),
        compiler_params=_params(1),
        name="experts",
    )(tile_expert, tile_valid, tile_src, order, hm, w_gate, w_up, w_down)


def _rope(xh, cos, sin, lane):
    swapped = jnp.where(lane < ROT_HALF, pltpu.roll(xh, LANES - ROT_HALF, 1), pltpu.roll(xh, ROT_HALF, 1))
    return xh * cos + swapped * sin


def _kvq_kernel(prompt, exact, x1_ref, yp_ref, route_ref, kvg_ref, wkv_ref, kg_ref, cos_ref, sin_ref,
                bng_ref, wq_ref, qg_ref, x2_ref, k_ref, v_ref, q_ref, *prompt_refs):
    x2 = _moe_residual(x1_ref[...], yp_ref, route_ref)
    x2_ref[...] = x2
    cos = cos_ref[...]
    sin = sin_ref[...]
    lane = lax.broadcasted_iota(jnp.int32, cos.shape, 1)
    kv = _mm(_rms(x2, kvg_ref[...]), wkv_ref[...], exact)
    for h in range(N_HEADS):
        hs = slice(h * HEAD_DIM, (h + 1) * HEAD_DIM)
        kh = _rope(_rms(kv[:, hs], kg_ref[...]), cos, sin, lane)
        k_ref[:, hs] = kh
        if prompt:
            kb_ref, _, km_ref = prompt_refs
            kb_ref[:, hs] = kh.astype(BF16)
            km_ref[0, :, hs] = jnp.mean(kh, axis=0, keepdims=True)
    v = kv[:, D_MODEL:]
    v_ref[...] = v
    if prompt:
        prompt_refs[1][0] = v.T.astype(BF16)
    q = _mm(_rms(x2, bng_ref[...]), wq_ref[...], exact)
    for h in range(N_HEADS):
        hs = slice(h * HEAD_DIM, (h + 1) * HEAD_DIM)
        q_ref[:, hs] = _rope(_rms(q[:, hs], qg_ref[...]), cos, sin, lane).astype(q_ref.dtype)


def _kvq(x1, ypairs, route, cos_tab, sin_tab, w, prompt, exact):
    n = x1.shape[0]
    mw = w["f32"] if exact else w["bf16"]
    tm = ROW_TILE if prompt else n
    row = lambda i: (i, 0)
    tab_tiles = cos_tab.shape[0] // tm
    tab = lambda i: (i % tab_tiles, 0)
    out_specs = [pl.BlockSpec((tm, D_MODEL), row)] * 4
    out_shape = [jax.ShapeDtypeStruct((n, D_MODEL), F32)] * 3 + [jax.ShapeDtypeStruct((n, D_MODEL), BF16 if prompt else F32)]
    if prompt:
        seq = cos_tab.shape[0]
        out_specs += [pl.BlockSpec((tm, D_MODEL), row),
                      pl.BlockSpec((1, D_MODEL, tm), lambda i: (i // tab_tiles, 0, i % tab_tiles)),
                      pl.BlockSpec((1, 1, D_MODEL), lambda i: (i, 0, 0))]
        out_shape += [jax.ShapeDtypeStruct((n, D_MODEL), BF16), jax.ShapeDtypeStruct((n // seq, D_MODEL, seq), BF16),
                      jax.ShapeDtypeStruct((n // tm, 1, D_MODEL), F32)]
    return pl.pallas_call(
        functools.partial(_kvq_kernel, prompt, exact),
        grid=(n // tm,),
        in_specs=[pl.BlockSpec((tm, D_MODEL), row), pl.BlockSpec((EXPERT_TOPK * tm * ROW_PIECES, LANES), row),
                  pl.BlockSpec((tm, LANES), row),
                  _full((1, D_MODEL)), _full((D_MODEL, 2 * D_MODEL)), _full((1, HEAD_DIM)),
                  pl.BlockSpec((tm, HEAD_DIM), tab), pl.BlockSpec((tm, HEAD_DIM), tab),
                  _full((1, D_MODEL)), _full((D_MODEL, D_MODEL)), _full((1, HEAD_DIM))],
        out_specs=out_specs,
        out_shape=out_shape,
        compiler_params=_params(1),
        name="kvq",
    )(x1, ypairs, route, w["kv_norm_g"], mw["w_kv"], w["k_norm_g"], cos_tab, sin_tab,
      w["b_norm_g"], mw["b_w_q"], w["q_norm_g"])


def _attn_kernel(q_ref, kb_ref, vt_ref, km_ref, x2_ref, wo_ref, mg_ref, wr_ref, br_ref,
                 x3_ref, hm_ref, route_ref, sel_scr, o_scr, *acc_scrs):
    j = pl.program_id(1)
    tq = q_ref.shape[0]
    n_blk = km_ref.shape[1]
    scale_log2e = HEAD_DIM ** -0.5 * 1.4426950408889634
    blk = lax.broadcasted_iota(jnp.int32, (n_blk, tq), 0).astype(F32)
    cur = j.astype(F32)
    contract_last = (((1,), (1,)), ((), ()))
    own = pl.multiple_of(j * MOBA_BLOCK, MOBA_BLOCK)
    chains = [(h, g) for h in range(N_HEADS) for g in range(tq // LANES)]

    def scores(h, g, off):
        hs = slice(h * HEAD_DIM, (h + 1) * HEAD_DIM)
        return lax.dot_general(kb_ref[pl.ds(off, MOBA_BLOCK), hs], q_ref[g * LANES:(g + 1) * LANES, hs],
                               contract_last, preferred_element_type=F32) * scale_log2e

    def weighted_values(h, off, p):
        hs = slice(h * HEAD_DIM, (h + 1) * HEAD_DIM)
        return jnp.dot(vt_ref[0, hs, pl.ds(off, MOBA_BLOCK)], p.astype(BF16), preferred_element_type=F32)

    for h in range(N_HEADS):
        hs = slice(h * HEAD_DIM, (h + 1) * HEAD_DIM)
        gate = lax.dot_general(km_ref[0, :, hs].astype(BF16), q_ref[:, hs], contract_last,
                               preferred_element_type=F32)
        gate = jnp.where(blk < cur, gate, NEG_INF)
        sel = jnp.zeros((n_blk, tq), F32)
        for _ in range(MOBA_TOPK):
            m = jnp.max(gate, axis=0, keepdims=True)
            first = jnp.min(jnp.where(gate == m, blk, float(n_blk)), axis=0, keepdims=True)
            pick = (blk == first) & (m > NEG_INF)
            sel = jnp.where(pick, 1.0, sel)
            gate = jnp.where(pick, NEG_INF, gate)
        sel_scr[h] = sel

    k_pos = lax.broadcasted_iota(jnp.int32, (MOBA_BLOCK, LANES), 0)
    q_pos = lax.broadcasted_iota(jnp.int32, (MOBA_BLOCK, LANES), 1)
    stats = []
    for h, g in chains:
        s = jnp.where(k_pos <= q_pos + g * LANES, scores(h, g, own), NEG_INF)
        m0 = jnp.max(s, axis=0, keepdims=True)
        p = jnp.exp2(s - m0)
        stats += [m0, jnp.sum(p, axis=0, keepdims=True)]
        acc_scrs[h][:, g * LANES:(g + 1) * LANES] = weighted_values(h, own, p)

    def past_block(jj, stats):
        off = pl.multiple_of(jj * MOBA_BLOCK, MOBA_BLOCK)
        new_stats = []
        visible = [sel_scr[h, pl.ds(jj, 1), :] for h in range(N_HEADS)]
        for c, (h, g) in enumerate(chains):
            qs = slice(g * LANES, (g + 1) * LANES)
            m_old, l_old = stats[2 * c], stats[2 * c + 1]
            s = jnp.where(visible[h][:, qs] > 0.0, scores(h, g, off), NEG_INF)
            m_new = jnp.maximum(m_old, jnp.max(s, axis=0, keepdims=True))
            alpha = jnp.exp2(m_old - m_new)
            p = jnp.exp2(s - m_new)
            new_stats += [m_new, alpha * l_old + jnp.sum(p, axis=0, keepdims=True)]
            acc_scrs[h][:, qs] = alpha * acc_scrs[h][:, qs] + weighted_values(h, off, p)
        return tuple(new_stats)

    stats = lax.fori_loop(0, j, past_block, tuple(stats))

    for c, (h, g) in enumerate(chains):
        qs = slice(g * LANES, (g + 1) * LANES)
        o_scr[qs, h * HEAD_DIM:(h + 1) * HEAD_DIM] = (acc_scrs[h][:, qs] / stats[2 * c + 1]).T.astype(BF16)

    x3 = x2_ref[...] + jnp.dot(o_scr[...], wo_ref[...], preferred_element_type=F32)
    x3_ref[...] = x3
    _moe_prenorm_and_route(x3, mg_ref, wr_ref, br_ref, hm_ref, route_ref, False)


def _attention_prompt(q, kb, vt, kmean, x2, w, seq):
    n = q.shape[0]
    bsz = n // seq
    n_qb = seq // MOBA_BLOCK
    row = lambda b, j: (b * n_qb + j, 0)
    return pl.pallas_call(
        _attn_kernel,
        grid=(bsz, n_qb),
        in_specs=[pl.BlockSpec((MOBA_BLOCK, D_MODEL), row),
                  pl.BlockSpec((seq, D_MODEL), lambda b, j: (b, 0)),
                  pl.BlockSpec((1, D_MODEL, seq), lambda b, j: (b, 0, 0)),
                  pl.BlockSpec((1, n_qb, D_MODEL), lambda b, j: (b, 0, 0)),
                  pl.BlockSpec((MOBA_BLOCK, D_MODEL), row),
                  _full((D_MODEL, D_MODEL)), _full((1, D_MODEL)), _full((D_MODEL, LANES)), _full((1, LANES))],
        out_specs=[pl.BlockSpec((MOBA_BLOCK, D_MODEL), row), pl.BlockSpec((MOBA_BLOCK * ROW_PIECES, LANES), row),
                   pl.BlockSpec((MOBA_BLOCK, LANES), row)],
        out_shape=[jax.ShapeDtypeStruct((n, D_MODEL), F32), jax.ShapeDtypeStruct((n * ROW_PIECES, LANES), F32),
                   jax.ShapeDtypeStruct((n, LANES), F32)],
        scratch_shapes=[pltpu.VMEM((N_HEADS, n_qb, MOBA_BLOCK), F32),
                        pltpu.VMEM((MOBA_BLOCK, D_MODEL), BF16)]
                       + [pltpu.VMEM((HEAD_DIM, MOBA_BLOCK), F32)] * N_HEADS,
        compiler_params=_params(2),
        name="moba_prompt",
    )(q, kb, vt, kmean.reshape(bsz, n_qb, D_MODEL), x2, w["bf16"]["b_w_o"],
      w["moe_norm_g"][1], w["bf16"]["moe_w_route"][1], w["moe_b_route"][1])


def _block_select_kernel(q_ref, km_ref, sel_ref):
    gate = jnp.sum(km_ref[0] * q_ref[...], axis=-1, keepdims=True)
    n_blk = gate.shape[0]
    blk = lax.broadcasted_iota(jnp.int32, gate.shape, 0).astype(F32)
    rank = lax.broadcasted_iota(jnp.int32, sel_ref.shape[1:], 1)
    out = jnp.zeros(sel_ref.shape[1:], F32)
    for r in range(MOBA_TOPK):
        m = jnp.max(gate, axis=0, keepdims=True)
        first = jnp.min(jnp.where(gate == m, blk, float(n_blk)), axis=0, keepdims=True)
        gate = jnp.where(blk == first, NEG_INF, gate)
        out = jnp.where(rank == r, first[0], out)
    sel_ref[0] = out.astype(jnp.int32)


def _block_select(q4, kmean_s):
    n_seq, n_blk = kmean_s.shape[:2]
    return pl.pallas_call(
        _block_select_kernel,
        grid=(n_seq,),
        in_specs=[pl.BlockSpec((1, N_HEADS, HEAD_DIM), lambda s: (s, 0, 0)),
                  pl.BlockSpec((1, n_blk, N_HEADS, HEAD_DIM), lambda s: (s, 0, 0, 0))],
        out_specs=pl.BlockSpec((1, N_HEADS, LANES), lambda s: (s, 0, 0)),
        out_shape=jax.ShapeDtypeStruct((n_seq, N_HEADS, LANES), jnp.int32),
        compiler_params=_params(1),
        name="block_select",
    )(q4, kmean_s)


def _attn_sample_kernel(n_sel, pg_ref, q_ref, k_own_ref, v_own_ref, ck_hbm, cv_hbm, o_ref, kbuf, vbuf, sem):
    s = pl.program_id(0)

    def page_copies(seq, slot):
        cps = []
        for h in range(N_HEADS):
            for i in range(n_sel):
                pg = pg_ref[(seq * N_HEADS + h) * n_sel + i]
                cps.append(pltpu.make_async_copy(ck_hbm.at[pg, :, h, :], kbuf.at[slot, h, i], sem.at[slot]))
                cps.append(pltpu.make_async_copy(cv_hbm.at[pg, :, h, :], vbuf.at[slot, h, i], sem.at[slot]))
        return cps

    @pl.when(s == 0)
    def _():
        for cp in page_copies(0, 0):
            cp.start()

    @pl.when(s + 1 < pl.num_programs(0))
    def _():
        for cp in page_copies(s + 1, (s + 1) % 2):
            cp.start()

    slot = s % 2
    for cp in page_copies(s, slot):
        cp.wait()

    scale = HEAD_DIM ** -0.5
    for h in range(N_HEADS):
        hs = slice(h * HEAD_DIM, (h + 1) * HEAD_DIM)
        q = q_ref[0, :, hs]
        s_own = jnp.sum(k_own_ref[0, :, hs] * q, axis=-1, keepdims=True) * scale
        scores = [jnp.sum(kbuf[slot, h, i] * q, axis=-1, keepdims=True) * scale for i in range(n_sel)]
        m = s_own
        for sc in scores:
            m = jnp.maximum(m, jnp.max(sc, axis=0, keepdims=True))
        p_own = jnp.exp(s_own - m)
        l = p_own
        acc = p_own * v_own_ref[0, :, hs]
        for i, sc in enumerate(scores):
            p = jnp.exp(sc - m)
            l = l + jnp.sum(p, axis=0, keepdims=True)
            acc = acc + jnp.sum(p * vbuf[slot, h, i], axis=0, keepdims=True)
        o_ref[0, :, hs] = acc / l


def _attention_sample(q3, k3, v3, cache_k, cache_v, sel_pages):
    n_seq = q3.shape[0]
    page = cache_k.shape[1]
    n_sel = sel_pages.shape[-1]
    seq_row = lambda s, pg: (s, 0, 0)
    grid_spec = pltpu.PrefetchScalarGridSpec(
        num_scalar_prefetch=1,
        grid=(n_seq,),
        in_specs=[pl.BlockSpec((1, 1, D_MODEL), seq_row)] * 3
                 + [pl.BlockSpec(memory_space=pl.ANY), pl.BlockSpec(memory_space=pl.ANY)],
        out_specs=pl.BlockSpec((1, 1, D_MODEL), seq_row),
        scratch_shapes=[pltpu.VMEM((2, N_HEADS, n_sel, page, HEAD_DIM), F32),
                        pltpu.VMEM((2, N_HEADS, n_sel, page, HEAD_DIM), F32),
                        pltpu.SemaphoreType.DMA((2,))],
    )
    return pl.pallas_call(
        functools.partial(_attn_sample_kernel, n_sel),
        grid_spec=grid_spec,
        out_shape=jax.ShapeDtypeStruct((n_seq, 1, D_MODEL), F32),
        compiler_params=_params(1),
        name="moba_sample",
    )(sel_pages.reshape(-1), q3, k3, v3, cache_k, cache_v)


def _oproj_kernel(o_ref, x2_ref, wo_ref, mg_ref, wr_ref, br_ref, x3_ref, hm_ref, route_ref):
    x3 = x2_ref[...] + _mm(o_ref[...], wo_ref[...], True)
    x3_ref[...] = x3
    _moe_prenorm_and_route(x3, mg_ref, wr_ref, br_ref, hm_ref, route_ref, True)


def _oproj(o, x2, w):
    n = o.shape[0]
    return pl.pallas_call(
        _oproj_kernel,
        grid=(1,),
        in_specs=[_full((n, D_MODEL)), _full((n, D_MODEL)), _full((D_MODEL, D_MODEL)), _full((1, D_MODEL)),
                  _full((D_MODEL, LANES)), _full((1, LANES))],
        out_specs=[_full((n, D_MODEL)), _full((n * ROW_PIECES, LANES)), _full((n, LANES))],
        out_shape=[jax.ShapeDtypeStruct((n, D_MODEL), F32), jax.ShapeDtypeStruct((n * ROW_PIECES, LANES), F32),
                   jax.ShapeDtypeStruct((n, LANES), F32)],
        compiler_params=_params(1),
        name="oproj_sample",
    )(o, x2, w["f32"]["b_w_o"], w["moe_norm_g"][1], w["f32"]["moe_w_route"][1], w["moe_b_route"][1])


def _combine_kernel(x_ref, yp_ref, route_ref, o_ref):
    o_ref[...] = _moe_residual(x_ref[...], yp_ref, route_ref)


def _combine(x, ypairs, route):
    n = x.shape[0]
    tm = min(n, 4 * ROW_TILE)
    row = lambda i: (i, 0)
    return pl.pallas_call(
        _combine_kernel,
        grid=(n // tm,),
        in_specs=[pl.BlockSpec((tm, D_MODEL), row), pl.BlockSpec((EXPERT_TOPK * tm * ROW_PIECES, LANES), row),
                  pl.BlockSpec((tm, LANES), row)],
        out_specs=pl.BlockSpec((tm, D_MODEL), row),
        out_shape=jax.ShapeDtypeStruct((n, D_MODEL), F32),
        compiler_params=_params(1),
        name="combine",
    )(x, ypairs, route)


def _rope_tables(pos):
    inv_freq = ROPE_THETA ** (-jnp.arange(ROT_HALF, dtype=F32) / ROT_HALF)
    ang = pos.astype(F32)[:, None] * inv_freq[None, :]
    cos, sin = jnp.cos(ang), jnp.sin(ang)
    rest = HEAD_DIM - ROT_DIM
    cos_tab = jnp.concatenate([cos, cos, jnp.ones((pos.shape[0], rest), F32)], axis=-1)
    sin_tab = jnp.concatenate([-sin, sin, jnp.zeros((pos.shape[0], rest), F32)], axis=-1)
    return cos_tab, sin_tab


def kernel(x_prompt, x_sample, cache_k, cache_v, page_table, a_norm_g, a_w_in, a_ln_g, a_ln_b, a_w_s, a_b_s, a_w_out, kv_norm_g, w_kv, k_norm_g, b_norm_g, b_w_q, q_norm_g, b_w_o, moe_norm_g, moe_w_rg, moe_b_rg, moe_w_re, moe_b_re, moe_w_gate, moe_w_up, moe_w_down):
    bsz, seq, _ = x_prompt.shape
    n_dec = x_sample.shape[0]
    assert x_sample.shape[1] == 1 and a_w_in.shape[0] == 1 and b_w_q.shape[0] == 1
    assert seq % ROW_TILE == 0 and ROW_TILE == MOBA_BLOCK
    page = cache_k.shape[1]
    past = page_table.shape[1] * page
    assert past % MOBA_BLOCK == 0 and past // MOBA_BLOCK >= MOBA_TOPK and MOBA_BLOCK % page == 0

    depth = moe_norm_g.shape[0]
    pad = LANES - N_EXPERT_GROUPS - N_EXPERTS
    mats = {
        "a_w_in": a_w_in[0], "a_w_out": a_w_out[0], "w_kv": w_kv, "b_w_q": b_w_q[0], "b_w_o": b_w_o[0],
        "moe_w_route": jnp.concatenate([moe_w_rg, moe_w_re, jnp.zeros((depth, D_MODEL, pad), F32)], axis=-1),
    }
    w = {
        "f32": mats, "bf16": {name: m.astype(BF16) for name, m in mats.items()},
        "moe_w_gate": moe_w_gate, "moe_w_up": moe_w_up, "moe_w_down": moe_w_down,
        "a_norm_g": a_norm_g[0][None], "a_ln_g": a_ln_g[0][None], "a_ln_b": a_ln_b[0][None],
        "a_w_s": a_w_s[0], "a_b_s_t": a_b_s[0].T,
        "a_ws0": jnp.repeat(a_w_s[0, :, 0, 0], GROUP_WIDTH)[None], "a_bs0": jnp.repeat(a_b_s[0, :, 0], GROUP_WIDTH)[None],
        "kv_norm_g": kv_norm_g[None], "k_norm_g": k_norm_g[None],
        "b_norm_g": b_norm_g[0][None], "q_norm_g": q_norm_g[0][None],
        "moe_norm_g": moe_norm_g[:, None, :],
        "moe_b_route": jnp.concatenate([moe_b_rg, moe_b_re, jnp.zeros((depth, pad), F32)], axis=-1)[:, None, :],
    }

    xp = x_prompt.reshape(bsz * seq, D_MODEL)
    xs = x_sample.reshape(n_dec, D_MODEL)
    cos_p, sin_p = _rope_tables(jnp.arange(seq, dtype=jnp.int32))
    cos_s, sin_s = _rope_tables(jnp.full((n_dec,), past, dtype=jnp.int32))

    x1p, vrows_p, hmp, route0p, kmean_s = _gmlp(xp, w, False, seq, cache_k, page_table)
    x1s, vrows_s, hms, route0s = _gmlp(xs, w, True, seq)
    yp = _experts(hmp, route0p, w, 0, EXPERT_TILE, False)
    ys = _experts(hms, route0s, w, 0, EXPERT_TILE_SMALL, True)

    x2p, k_p, v_p, q_p, kb_p, vt_p, kmean_p = _kvq(x1p, yp, route0p, cos_p, sin_p, w, True, False)
    x2s, k_s, v_s, q_s = _kvq(x1s, ys, route0s, cos_s, sin_s, w, False, True)

    x3p, hmp, route1p = _attention_prompt(q_p, kb_p, vt_p, kmean_p, x2p, w, seq)
    sel_blocks = _block_select(q_s.reshape(n_dec, N_HEADS, HEAD_DIM), kmean_s)[:, :, :MOBA_TOPK]
    pages_per_block = MOBA_BLOCK // page
    page_slots = sel_blocks[..., None] * pages_per_block + jnp.arange(pages_per_block, dtype=jnp.int32)
    page_slots = page_slots.reshape(n_dec, N_HEADS * MOBA_TOPK * pages_per_block)
    sel_pages = jnp.take_along_axis(page_table, page_slots, axis=1).reshape(n_dec, N_HEADS, MOBA_TOPK * pages_per_block)
    o_s = _attention_sample(q_s.reshape(n_dec, 1, D_MODEL), k_s.reshape(n_dec, 1, D_MODEL),
                            v_s.reshape(n_dec, 1, D_MODEL), cache_k, cache_v, sel_pages)
    x3s, hms, route1s = _oproj(o_s.reshape(n_dec, D_MODEL), x2s, w)
    yp = _experts(hmp, route1p, w, 1, EXPERT_TILE, False)
    ys = _experts(hms, route1s, w, 1, EXPERT_TILE_SMALL, True)
    y_prompt = _combine(x3p, yp, route1p)
    y_sample = _combine(x3s, ys, route1s)

    return (y_prompt.reshape(bsz, seq, D_MODEL), y_sample.reshape(n_dec, 1, D_MODEL),
            vrows_p.reshape(1, bsz, CHUNK, GATE_WIDTH), vrows_s.reshape(1, n_dec, 1, GATE_WIDTH),
            k_p.reshape(bsz, seq, N_HEADS, HEAD_DIM), v_p.reshape(bsz, seq, N_HEADS, HEAD_DIM),
            k_s.reshape(n_dec, 1, N_HEADS, HEAD_DIM), v_s.reshape(n_dec, 1, N_HEADS, HEAD_DIM))
```

```python
import functools

import jax
import jax.numpy as jnp
from jax import lax
from jax.experimental import pallas as pl
from jax.experimental.pallas import tpu as pltpu

D_MODEL = 1024
CHUNK = 128
GATE_WIDTH = 2 * D_MODEL
N_GATE_GROUPS = 8
GROUP_WIDTH = GATE_WIDTH // N_GATE_GROUPS
HEAD_DIM = 128
N_HEADS = D_MODEL // HEAD_DIM
ROT_DIM = HEAD_DIM // 4
ROT_HALF = ROT_DIM // 2
ROPE_THETA = 500000.0
MOBA_BLOCK = 256
MOBA_TOPK = 3
N_EXPERT_GROUPS = 4
EXPERTS_PER_GROUP = 4
N_EXPERTS = N_EXPERT_GROUPS * EXPERTS_PER_GROUP
EXPERT_TOPK = 2
EXPERT_FF = D_MODEL // 2
EPS = 1e-6

LANES = 128
SUBLANES = 8
ROW_PIECES = D_MODEL // LANES
ROW_TILE = 256
EXPERT_TILE = 256
EXPERT_TILE_SMALL = 16
IN_PROJ_COLS = 512
N_IN_PROJ_CHUNKS = 2 * GATE_WIDTH // IN_PROJ_COLS
PAGE_BATCH = 16
N_PAGE_BATCHES = 2
GATHER_DMA_PRIORITY = 1
VMEM_LIMIT = 60 * 1024 * 1024

F32 = jnp.float32
BF16 = jnp.bfloat16
NEG_INF = float("-inf")

assert ROW_PIECES == SUBLANES


def _params(n_axes):
    return pltpu.CompilerParams(dimension_semantics=("arbitrary",) * n_axes, vmem_limit_bytes=VMEM_LIMIT)


def _rms(x, g):
    return x * lax.rsqrt(jnp.mean(x * x, axis=-1, keepdims=True) + EPS) * g


def _gelu_tanh(z):
    c = 0.7978845608028654
    half_z = 0.5 * z
    return half_z + half_z * jnp.tanh(z * (c + (c * 0.044715) * (z * z)))


def _mm(a, b, exact):
    if exact:
        return jnp.dot(a, b, precision=lax.Precision.HIGHEST, preferred_element_type=F32)
    return jnp.dot(a.astype(BF16), b, preferred_element_type=F32)


def _full(shape):
    nd = len(shape)
    return pl.BlockSpec(shape, lambda *_: (0,) * nd, pipeline_mode=pl.Buffered(1))


def _store_tile_rows(ref, x):
    for s in range(ROW_PIECES):
        ref[pl.ds(s, x.shape[0], stride=ROW_PIECES), :] = x[:, s * LANES:(s + 1) * LANES]


def _load_tile_rows(ref, rows, first=0, step=ROW_PIECES):
    return jnp.concatenate([ref[pl.ds(first + s, rows, stride=step), :] for s in range(ROW_PIECES)], axis=-1)


def _route(hm, wr_ref, br_ref, exact):
    logits = _mm(hm, wr_ref[...], exact) + br_ref[...]
    lane = lax.broadcasted_iota(jnp.int32, logits.shape, 1).astype(F32)
    lg = jnp.where(lane < N_EXPERT_GROUPS, logits, NEG_INF)
    mg = jnp.max(lg, axis=-1, keepdims=True)
    g_sel = jnp.min(jnp.where(lg == mg, lane, float(LANES)), axis=-1, keepdims=True)
    p_grp = 1.0 / jnp.sum(jnp.exp(lg - mg), axis=-1, keepdims=True)
    lo = N_EXPERT_GROUPS + g_sel * EXPERTS_PER_GROUP
    le = jnp.where((lane >= lo) & (lane < lo + EXPERTS_PER_GROUP), logits, NEG_INF)
    v1 = jnp.max(le, axis=-1, keepdims=True)
    i1 = jnp.min(jnp.where(le == v1, lane, float(LANES)), axis=-1, keepdims=True)
    le2 = jnp.where(lane == i1, NEG_INF, le)
    v2 = jnp.max(le2, axis=-1, keepdims=True)
    i2 = jnp.min(jnp.where(le2 == v2, lane, float(LANES)), axis=-1, keepdims=True)
    t = jnp.exp(v2 - v1)
    w1 = p_grp / (1.0 + t)
    w2 = p_grp * t / (1.0 + t)
    out = jnp.where(lane == 0.0, i1 - N_EXPERT_GROUPS, 0.0)
    out = jnp.where(lane == 1.0, i2 - N_EXPERT_GROUPS, out)
    out = jnp.where(lane == 2.0, w1, out)
    out = jnp.where(lane == 3.0, w2, out)
    return out


def _moe_prenorm_and_route(x, mg_ref, wr_ref, br_ref, hm_ref, route_ref, exact):
    hm = _rms(x, mg_ref[...])
    _store_tile_rows(hm_ref, hm)
    route_ref[...] = _route(hm, wr_ref, br_ref, exact)


def _moe_residual(x, y_ref, route_ref):
    rows = x.shape[0]
    step = EXPERT_TOPK * ROW_PIECES
    y1 = _load_tile_rows(y_ref, rows, 0, step)
    y2 = _load_tile_rows(y_ref, rows, ROW_PIECES, step)
    return x + (route_ref[:, 2:3] * y1 + route_ref[:, 3:4] * y2)


def _page_batch_copies(pt_ref, ck_hbm, pbuf, psem, step, k):
    n_pages = pt_ref.shape[1]
    first = (step * N_PAGE_BATCHES + k) * PAGE_BATCH
    seq, page0 = first // n_pages, first % n_pages
    return [pltpu.make_async_copy(ck_hbm.at[pt_ref[seq, page0 + p]], pbuf.at[k, p], psem.at[k])
            for p in range(PAGE_BATCH)]


def _reduce_page_batch(pbuf, km_ref, k, pages_per_block):
    blocks = PAGE_BATCH // pages_per_block
    for b in range(blocks):
        tot = jnp.sum(pbuf[k, b * pages_per_block], axis=0)
        for p in range(1, pages_per_block):
            tot = tot + jnp.sum(pbuf[k, b * pages_per_block + p], axis=0)
        km_ref[0, k * blocks + b] = tot / MOBA_BLOCK


def _gmlp_kernel(sample, tm, *refs):
    if sample:
        (x_ref, ng_ref, win_ref, lng_ref, lnb_ref, ws_ref, bs_ref, wout_ref, mg_ref, wr_ref, br_ref,
         x1_ref, vrows_ref, hm_ref, route_ref, u_scr, v_scr, gated_scr) = refs
    else:
        (pt_ref, x_ref, ng_ref, win_ref, lng_ref, lnb_ref, ws_ref, bs_ref, wout_ref, mg_ref, wr_ref, br_ref, ck_hbm,
         x1_ref, vrows_ref, hm_ref, route_ref, km_ref, u_scr, v_scr, gated_scr, pbuf, psem) = refs
        step = pl.program_id(0)
        batch = functools.partial(_page_batch_copies, pt_ref, ck_hbm, pbuf, psem)
        pages_per_block = MOBA_BLOCK // ck_hbm.shape[1]

        @pl.when(step == 0)
        def _():
            for cp in batch(step, 0):
                cp.start()

        for cp in batch(step, 1):
            cp.start()

    x = x_ref[...]
    h = _rms(x, ng_ref[...])
    if not sample:
        h = h.astype(BF16)
    n_col = N_IN_PROJ_CHUNKS
    n_ucol = GATE_WIDTH // IN_PROJ_COLS
    vsum = jnp.zeros((tm, 1), F32)
    for c in range(n_col):
        z = _mm(h, win_ref[:, c * IN_PROJ_COLS:(c + 1) * IN_PROJ_COLS], sample)
        z = _gelu_tanh(z)
        if c < n_ucol:
            u_scr[:, c * IN_PROJ_COLS:(c + 1) * IN_PROJ_COLS] = z
        else:
            v_scr[:, (c - n_ucol) * IN_PROJ_COLS:(c - n_ucol + 1) * IN_PROJ_COLS] = z
            vsum = vsum + jnp.sum(z, axis=-1, keepdims=True)

    if not sample:
        for cp in batch(step, 0):
            cp.wait()
        _reduce_page_batch(pbuf, km_ref, 0, pages_per_block)

        @pl.when(step + 1 < pl.num_programs(0))
        def _():
            for cp in batch(step + 1, 0):
                cp.start()

    mean = vsum / GATE_WIDTH
    vss = jnp.zeros((tm, 1), F32)
    for c in range(n_ucol):
        xc = v_scr[:, c * IN_PROJ_COLS:(c + 1) * IN_PROJ_COLS] - mean
        vss = vss + jnp.sum(xc * xc, axis=-1, keepdims=True)
    rstd = lax.rsqrt(vss / GATE_WIDTH + EPS)
    for c in range(n_ucol):
        cs = slice(c * IN_PROJ_COLS, (c + 1) * IN_PROJ_COLS)
        v_scr[:, cs] = (v_scr[:, cs] - mean) * rstd * lng_ref[:, cs] + lnb_ref[:, cs]

    if sample:
        vrows_ref[...] = v_scr[...]
        gated_scr[...] = u_scr[...] * (v_scr[...] * ws_ref[...] + bs_ref[...])
    else:
        vrows_ref[0] = v_scr[tm - CHUNK:tm, :]
        t_out = lax.broadcasted_iota(jnp.int32, (CHUNK, CHUNK), 0)
        s_in = lax.broadcasted_iota(jnp.int32, (CHUNK, CHUNK), 1)
        for g in range(N_GATE_GROUPS):
            gs = slice(g * GROUP_WIDTH, (g + 1) * GROUP_WIDTH)
            wsg = jnp.where(s_in <= t_out, ws_ref[g], 0.0).astype(BF16)
            bias = bs_ref[:, g:g + 1]
            for c in range(tm // CHUNK):
                rs = slice(c * CHUNK, (c + 1) * CHUNK)
                mixed = jnp.dot(wsg, v_scr[rs, gs].astype(BF16), preferred_element_type=F32) + bias
                gated_scr[rs, gs] = (u_scr[rs, gs] * mixed).astype(BF16)

    x1 = x + _mm(gated_scr[...], wout_ref[...], sample)
    x1_ref[...] = x1
    if not sample:
        for cp in batch(step, 1):
            cp.wait()
        _reduce_page_batch(pbuf, km_ref, 1, pages_per_block)
    _moe_prenorm_and_route(x1, mg_ref, wr_ref, br_ref, hm_ref, route_ref, sample)


def _gmlp(x, w, sample, seq, cache_k=None, page_table=None):
    n = x.shape[0]
    mw = w["f32"] if sample else w["bf16"]
    tm = n if sample else ROW_TILE
    n_steps = n // tm
    row = lambda i, *_: (i, 0)
    in_specs = [pl.BlockSpec((tm, D_MODEL), row), _full((1, D_MODEL)), _full((D_MODEL, 2 * GATE_WIDTH)),
                _full((1, GATE_WIDTH)), _full((1, GATE_WIDTH)), None, None, _full((GATE_WIDTH, D_MODEL)),
                _full((1, D_MODEL)), _full((D_MODEL, LANES)), _full((1, LANES))]
    out_specs = [pl.BlockSpec((tm, D_MODEL), row), None, pl.BlockSpec((tm * ROW_PIECES, LANES), row),
                 pl.BlockSpec((tm, LANES), row)]
    out_shape = [jax.ShapeDtypeStruct((n, D_MODEL), F32), None,
                 jax.ShapeDtypeStruct((n * ROW_PIECES, LANES), F32), jax.ShapeDtypeStruct((n, LANES), F32)]
    scratch = [pltpu.VMEM((tm, GATE_WIDTH), F32), pltpu.VMEM((tm, GATE_WIDTH), F32),
               pltpu.VMEM((tm, GATE_WIDTH), F32 if sample else BF16)]
    args = [x, w["a_norm_g"], mw["a_w_in"], w["a_ln_g"], w["a_ln_b"],
            w["a_ws0"] if sample else w["a_w_s"], w["a_bs0"] if sample else w["a_b_s_t"], mw["a_w_out"],
            w["moe_norm_g"][0], mw["moe_w_route"][0], w["moe_b_route"][0]]
    if sample:
        in_specs[5], in_specs[6] = _full((1, GATE_WIDTH)), _full((1, GATE_WIDTH))
        out_shape[1] = jax.ShapeDtypeStruct((n, GATE_WIDTH), F32)
        out_specs[1] = pl.BlockSpec((tm, GATE_WIDTH), row)
        n_prefetch = 0
    else:
        in_specs[5], in_specs[6] = _full((N_GATE_GROUPS, CHUNK, CHUNK)), _full((CHUNK, N_GATE_GROUPS))
        tiles_per_seq = seq // tm
        out_shape[1] = jax.ShapeDtypeStruct((n // seq, CHUNK, GATE_WIDTH), F32)
        out_specs[1] = pl.BlockSpec((1, CHUNK, GATE_WIDTH), lambda i, *_: (i // tiles_per_seq, 0, 0))
        n_dec, n_pages = page_table.shape
        page = cache_k.shape[1]
        pages_per_step = N_PAGE_BATCHES * PAGE_BATCH
        assert n_dec * n_pages == n_steps * pages_per_step and n_pages % pages_per_step == 0
        blocks_per_step = pages_per_step * page // MOBA_BLOCK
        steps_per_seq = n_pages // pages_per_step
        in_specs.append(pl.BlockSpec(memory_space=pl.ANY))
        args = [page_table] + args + [cache_k]
        out_specs.append(pl.BlockSpec((1, blocks_per_step, N_HEADS, HEAD_DIM),
                                      lambda i, *_: (i // steps_per_seq, i % steps_per_seq, 0, 0)))
        out_shape.append(jax.ShapeDtypeStruct((n_dec, n_pages * page // MOBA_BLOCK, N_HEADS, HEAD_DIM), F32))
        scratch += [pltpu.VMEM((N_PAGE_BATCHES, PAGE_BATCH, page, N_HEADS, HEAD_DIM), F32),
                    pltpu.SemaphoreType.DMA((N_PAGE_BATCHES,))]
        n_prefetch = 1
    grid_spec = pltpu.PrefetchScalarGridSpec(num_scalar_prefetch=n_prefetch, grid=(n_steps,), in_specs=in_specs,
                                             out_specs=out_specs, scratch_shapes=scratch)
    return pl.pallas_call(
        functools.partial(_gmlp_kernel, sample, tm),
        grid_spec=grid_spec,
        out_shape=out_shape,
        compiler_params=_params(1),
        name="gmlp_sample" if sample else "gmlp_prompt",
    )(*args)


def _expert_kernel(tmg, n_pairs, n_tiles, exact, te_ref, nv_ref, ts_ref, od_ref, h_hbm, wg_ref, wu_ref, wd_ref, out_hbm,
                   xbuf0, xbuf1, ybuf0, ybuf1, sem_in, sem_out, *bf16_weights):
    i = pl.program_id(0)
    xbufs, ybufs = (xbuf0, xbuf1), (ybuf0, ybuf1)
    rows8 = SUBLANES
    weights = (wg_ref, wu_ref, wd_ref) if exact else bf16_weights

    def valid(t):
        return (t >= 0) & (t < n_tiles) & (nv_ref[jnp.clip(t, 0, n_tiles - 1)] > 0)

    def gather_row(r, p, tok=0):
        return pltpu.make_async_copy(h_hbm.at[pl.ds(pl.multiple_of(tok * rows8, rows8), rows8)],
                                     xbufs[p].at[pl.ds(r * rows8, rows8)], sem_in.at[p])

    def scatter_row(r, p, pair=0):
        row8 = r * rows8 if isinstance(r, int) else pl.multiple_of(r * rows8, rows8)
        return pltpu.make_async_copy(ybufs[p].at[pl.ds(row8, rows8)],
                                     out_hbm.at[pl.ds(pl.multiple_of(pair * rows8, rows8), rows8)], sem_out.at[p])

    def start_gather(t, p):
        base = ts_ref[t]
        for r in range(tmg):
            gather_row(r, p, od_ref[base + r] >> 1).start(priority=GATHER_DMA_PRIORITY)

    def for_scatter_rows(t, p, start, all_rows):
        base = ts_ref[t]

        def one(r):
            if start:
                scatter_row(r, p, od_ref[base + r]).start()
            else:
                scatter_row(r, p).wait()

        if all_rows:
            for r in range(tmg):
                one(r)
        else:
            def body(r, carry):
                one(r)
                return carry
            lax.fori_loop(0, jnp.minimum(tmg, n_pairs - base), body, 0)

    def refresh_weights(t):
        if not exact:
            @pl.when((t == 0) | (te_ref[t] != te_ref[jnp.maximum(t - 1, 0)]))
            def _():
                for w_bf16, w_ref in zip(bf16_weights, (wg_ref, wu_ref, wd_ref)):
                    w_bf16[...] = w_ref[...].astype(BF16)

    def compute(p):
        wg, wu, wd = weights
        x = _load_tile_rows(xbufs[p], tmg)
        a = jax.nn.silu(_mm(x, wg[...], exact)) * _mm(x, wu[...], exact)
        _store_tile_rows(ybufs[p], _mm(a, wd[...], exact))

    def step(p, guarded):
        when = pl.when if guarded else (lambda cond: (lambda body: body()))

        @when(valid(i - 1))
        def _():
            for r in range(tmg):
                gather_row(r, 1 - p).wait()
            refresh_weights(i - 1)

        @when(valid(i - 3))
        def _():
            for_scatter_rows(i - 3, 1 - p, False, not guarded)

        @when(valid(i - 2))
        def _():
            for_scatter_rows(i - 2, p, True, not guarded)

        @when(valid(i))
        def _():
            start_gather(i, p)

        @when(valid(i - 1))
        def _():
            compute(1 - p)

    steady = (i >= 3) & valid(i) & (ts_ref[jnp.clip(i - 2, 0, n_tiles - 1)] + tmg <= n_pairs)
    for p in range(2):
        @pl.when(steady & (i % 2 == p))
        def _():
            step(p, False)

        @pl.when(jnp.logical_not(steady) & (i % 2 == p))
        def _():
            step(p, True)


def _dispatch(route, tmg):
    n_pairs = EXPERT_TOPK * route.shape[0]
    e_flat = route[:, 0:EXPERT_TOPK].astype(jnp.int32).reshape(n_pairs)
    n_tiles = -(-(n_pairs + N_EXPERTS * (tmg - 1)) // tmg)
    order = jnp.argsort(e_flat, stable=True).astype(jnp.int32)
    counts = jnp.sum((e_flat[:, None] == jnp.arange(N_EXPERTS, dtype=jnp.int32)[None, :]).astype(jnp.int32), axis=0)
    tiles_per = -(-counts // tmg)
    t_end = jnp.cumsum(tiles_per)
    t_start = t_end - tiles_per
    c_start = jnp.cumsum(counts) - counts
    tile = jnp.arange(n_tiles, dtype=jnp.int32)
    tile_expert = jnp.minimum(jnp.sum((t_end[None, :] <= tile[:, None]).astype(jnp.int32), axis=1), N_EXPERTS - 1)
    first_row = (tile - t_start[tile_expert]) * tmg
    tile_valid = jnp.clip(counts[tile_expert] - first_row, 0, tmg).astype(jnp.int32)
    tile_src = jnp.where(tile_valid > 0, c_start[tile_expert] + first_row, 0).astype(jnp.int32)
    order = jnp.concatenate([order, jnp.zeros((tmg,), jnp.int32)])
    return tile_expert.astype(jnp.int32), tile_valid, tile_src, order


def _experts(hm, route, w, layer, tmg, exact):
    n = route.shape[0]
    w_gate, w_up, w_down = w["moe_w_gate"], w["moe_w_up"], w["moe_w_down"]
    tile_expert, tile_valid, tile_src, order = _dispatch(route, tmg)
    n_tiles = tile_expert.shape[0]
    by_expert = lambda i, te, nv, ts, od: (layer, te[jnp.clip(i - 1, 0, n_tiles - 1)], 0, 0)
    scratch = [pltpu.VMEM((tmg * ROW_PIECES, LANES), F32)] * 4 \
        + [pltpu.SemaphoreType.DMA((2,)), pltpu.SemaphoreType.DMA((2,))]
    if not exact:
        scratch += [pltpu.VMEM((D_MODEL, EXPERT_FF), BF16), pltpu.VMEM((D_MODEL, EXPERT_FF), BF16),
                    pltpu.VMEM((EXPERT_FF, D_MODEL), BF16)]
    grid_spec = pltpu.PrefetchScalarGridSpec(
        num_scalar_prefetch=4,
        grid=(n_tiles + 3,),
        in_specs=[pl.BlockSpec(memory_space=pl.ANY),
                  pl.BlockSpec((None, None, D_MODEL, EXPERT_FF), by_expert),
                  pl.BlockSpec((None, None, D_MODEL, EXPERT_FF), by_expert),
                  pl.BlockSpec((None, None, EXPERT_FF, D_MODEL), by_expert)],
        out_specs=pl.BlockSpec(memory_space=pl.ANY),
        scratch_shapes=scratch,
    )
    return pl.pallas_call(
        functools.partial(_expert_kernel, tmg, EXPERT_TOPK * n, n_tiles, exact),
        grid_spec=grid_spec,
        out_shape=jax.ShapeDtypeStruct((EXPERT_TOPK * n * ROW_PIECES, LANES), F32),
        compiler_params=_params(1),
        name="experts",
    )(tile_expert, tile_valid, tile_src, order, hm, w_gate, w_up, w_down)


def _rope(xh, cos, sin, lane):
    swapped = jnp.where(lane < ROT_HALF, pltpu.roll(xh, LANES - ROT_HALF, 1), pltpu.roll(xh, ROT_HALF, 1))
    return xh * cos + swapped * sin


def _kvq_kernel(prompt, exact, x1_ref, yp_ref, route_ref, kvg_ref, wkv_ref, kg_ref, cos_ref, sin_ref,
                bng_ref, wq_ref, qg_ref, x2_ref, k_ref, v_ref, q_ref, *prompt_refs):
    x2 = _moe_residual(x1_ref[...], yp_ref, route_ref)
    x2_ref[...] = x2
    cos = cos_ref[...]
    sin = sin_ref[...]
    lane = lax.broadcasted_iota(jnp.int32, cos.shape, 1)
    kv = _mm(_rms(x2, kvg_ref[...]), wkv_ref[...], exact)
    for h in range(N_HEADS):
        hs = slice(h * HEAD_DIM, (h + 1) * HEAD_DIM)
        kh = _rope(_rms(kv[:, hs], kg_ref[...]), cos, sin, lane)
        k_ref[:, hs] = kh
        if prompt:
            kb_ref, _, km_ref = prompt_refs
            kb_ref[:, hs] = kh.astype(BF16)
            km_ref[0, :, hs] = jnp.mean(kh, axis=0, keepdims=True)
    v = kv[:, D_MODEL:]
    v_ref[...] = v
    if prompt:
        prompt_refs[1][0] = v.T.astype(BF16)
    q = _mm(_rms(x2, bng_ref[...]), wq_ref[...], exact)
    for h in range(N_HEADS):
        hs = slice(h * HEAD_DIM, (h + 1) * HEAD_DIM)
        q_ref[:, hs] = _rope(_rms(q[:, hs], qg_ref[...]), cos, sin, lane).astype(q_ref.dtype)


def _kvq(x1, ypairs, route, cos_tab, sin_tab, w, prompt, exact):
    n = x1.shape[0]
    mw = w["f32"] if exact else w["bf16"]
    tm = ROW_TILE if prompt else n
    row = lambda i: (i, 0)
    tab_tiles = cos_tab.shape[0] // tm
    tab = lambda i: (i % tab_tiles, 0)
    out_specs = [pl.BlockSpec((tm, D_MODEL), row)] * 4
    out_shape = [jax.ShapeDtypeStruct((n, D_MODEL), F32)] * 3 + [jax.ShapeDtypeStruct((n, D_MODEL), BF16 if prompt else F32)]
    if prompt:
        seq = cos_tab.shape[0]
        out_specs += [pl.BlockSpec((tm, D_MODEL), row),
                      pl.BlockSpec((1, D_MODEL, tm), lambda i: (i // tab_tiles, 0, i % tab_tiles)),
                      pl.BlockSpec((1, 1, D_MODEL), lambda i: (i, 0, 0))]
        out_shape += [jax.ShapeDtypeStruct((n, D_MODEL), BF16), jax.ShapeDtypeStruct((n // seq, D_MODEL, seq), BF16),
                      jax.ShapeDtypeStruct((n // tm, 1, D_MODEL), F32)]
    return pl.pallas_call(
        functools.partial(_kvq_kernel, prompt, exact),
        grid=(n // tm,),
        in_specs=[pl.BlockSpec((tm, D_MODEL), row), pl.BlockSpec((EXPERT_TOPK * tm * ROW_PIECES, LANES), row),
                  pl.BlockSpec((tm, LANES), row),
                  _full((1, D_MODEL)), _full((D_MODEL, 2 * D_MODEL)), _full((1, HEAD_DIM)),
                  pl.BlockSpec((tm, HEAD_DIM), tab), pl.BlockSpec((tm, HEAD_DIM), tab),
                  _full((1, D_MODEL)), _full((D_MODEL, D_MODEL)), _full((1, HEAD_DIM))],
        out_specs=out_specs,
        out_shape=out_shape,
        compiler_params=_params(1),
        name="kvq",
    )(x1, ypairs, route, w["kv_norm_g"], mw["w_kv"], w["k_norm_g"], cos_tab, sin_tab,
      w["b_norm_g"], mw["b_w_q"], w["q_norm_g"])


def _attn_kernel(q_ref, kb_ref, vt_ref, km_ref, x2_ref, wo_ref, mg_ref, wr_ref, br_ref,
                 x3_ref, hm_ref, route_ref, sel_scr, o_scr, *acc_scrs):
    j = pl.program_id(1)
    tq = q_ref.shape[0]
    n_blk = km_ref.shape[1]
    scale_log2e = HEAD_DIM ** -0.5 * 1.4426950408889634
    blk = lax.broadcasted_iota(jnp.int32, (n_blk, tq), 0).astype(F32)
    cur = j.astype(F32)
    contract_last = (((1,), (1,)), ((), ()))
    own = pl.multiple_of(j * MOBA_BLOCK, MOBA_BLOCK)
    chains = [(h, g) for h in range(N_HEADS) for g in range(tq // LANES)]

    def scores(h, g, off):
        hs = slice(h * HEAD_DIM, (h + 1) * HEAD_DIM)
        return lax.dot_general(kb_ref[pl.ds(off, MOBA_BLOCK), hs], q_ref[g * LANES:(g + 1) * LANES, hs],
                               contract_last, preferred_element_type=F32) * scale_log2e

    def weighted_values(h, off, p):
        hs = slice(h * HEAD_DIM, (h + 1) * HEAD_DIM)
        return jnp.dot(vt_ref[0, hs, pl.ds(off, MOBA_BLOCK)], p.astype(BF16), preferred_element_type=F32)

    for h in range(N_HEADS):
        hs = slice(h * HEAD_DIM, (h + 1) * HEAD_DIM)
        gate = lax.dot_general(km_ref[0, :, hs].astype(BF16), q_ref[:, hs], contract_last,
                               preferred_element_type=F32)
        gate = jnp.where(blk < cur, gate, NEG_INF)
        sel = jnp.zeros((n_blk, tq), F32)
        for _ in range(MOBA_TOPK):
            m = jnp.max(gate, axis=0, keepdims=True)
            first = jnp.min(jnp.where(gate == m, blk, float(n_blk)), axis=0, keepdims=True)
            pick = (blk == first) & (m > NEG_INF)
            sel = jnp.where(pick, 1.0, sel)
            gate = jnp.where(pick, NEG_INF, gate)
        sel_scr[h] = sel

    k_pos = lax.broadcasted_iota(jnp.int32, (MOBA_BLOCK, LANES), 0)
    q_pos = lax.broadcasted_iota(jnp.int32, (MOBA_BLOCK, LANES), 1)
    stats = []
    for h, g in chains:
        s = jnp.where(k_pos <= q_pos + g * LANES, scores(h, g, own), NEG_INF)
        m0 = jnp.max(s, axis=0, keepdims=True)
        p = jnp.exp2(s - m0)
        stats += [m0, jnp.sum(p, axis=0, keepdims=True)]
        acc_scrs[h][:, g * LANES:(g + 1) * LANES] = weighted_values(h, own, p)

    def past_block(jj, stats):
        off = pl.multiple_of(jj * MOBA_BLOCK, MOBA_BLOCK)
        new_stats = []
        visible = [sel_scr[h, pl.ds(jj, 1), :] for h in range(N_HEADS)]
        for c, (h, g) in enumerate(chains):
            qs = slice(g * LANES, (g + 1) * LANES)
            m_old, l_old = stats[2 * c], stats[2 * c + 1]
            s = jnp.where(visible[h][:, qs] > 0.0, scores(h, g, off), NEG_INF)
            m_new = jnp.maximum(m_old, jnp.max(s, axis=0, keepdims=True))
            alpha = jnp.exp2(m_old - m_new)
            p = jnp.exp2(s - m_new)
            new_stats += [m_new, alpha * l_old + jnp.sum(p, axis=0, keepdims=True)]
            acc_scrs[h][:, qs] = alpha * acc_scrs[h][:, qs] + weighted_values(h, off, p)
        return tuple(new_stats)

    stats = lax.fori_loop(0, j, past_block, tuple(stats))

    for c, (h, g) in enumerate(chains):
        qs = slice(g * LANES, (g + 1) * LANES)
        o_scr[qs, h * HEAD_DIM:(h + 1) * HEAD_DIM] = (acc_scrs[h][:, qs] / stats[2 * c + 1]).T.astype(BF16)

    x3 = x2_ref[...] + jnp.dot(o_scr[...], wo_ref[...], preferred_element_type=F32)
    x3_ref[...] = x3
    _moe_prenorm_and_route(x3, mg_ref, wr_ref, br_ref, hm_ref, route_ref, False)


def _attention_prompt(q, kb, vt, kmean, x2, w, seq):
    n = q.shape[0]
    bsz = n // seq
    n_qb = seq // MOBA_BLOCK
    row = lambda b, j: (b * n_qb + j, 0)
    return pl.pallas_call(
        _attn_kernel,
        grid=(bsz, n_qb),
        in_specs=[pl.BlockSpec((MOBA_BLOCK, D_MODEL), row),
                  pl.BlockSpec((seq, D_MODEL), lambda b, j: (b, 0)),
                  pl.BlockSpec((1, D_MODEL, seq), lambda b, j: (b, 0, 0)),
                  pl.BlockSpec((1, n_qb, D_MODEL), lambda b, j: (b, 0, 0)),
                  pl.BlockSpec((MOBA_BLOCK, D_MODEL), row),
                  _full((D_MODEL, D_MODEL)), _full((1, D_MODEL)), _full((D_MODEL, LANES)), _full((1, LANES))],
        out_specs=[pl.BlockSpec((MOBA_BLOCK, D_MODEL), row), pl.BlockSpec((MOBA_BLOCK * ROW_PIECES, LANES), row),
                   pl.BlockSpec((MOBA_BLOCK, LANES), row)],
        out_shape=[jax.ShapeDtypeStruct((n, D_MODEL), F32), jax.ShapeDtypeStruct((n * ROW_PIECES, LANES), F32),
                   jax.ShapeDtypeStruct((n, LANES), F32)],
        scratch_shapes=[pltpu.VMEM((N_HEADS, n_qb, MOBA_BLOCK), F32),
                        pltpu.VMEM((MOBA_BLOCK, D_MODEL), BF16)]
                       + [pltpu.VMEM((HEAD_DIM, MOBA_BLOCK), F32)] * N_HEADS,
        compiler_params=_params(2),
        name="moba_prompt",
    )(q, kb, vt, kmean.reshape(bsz, n_qb, D_MODEL), x2, w["bf16"]["b_w_o"],
      w["moe_norm_g"][1], w["bf16"]["moe_w_route"][1], w["moe_b_route"][1])


def _block_select_kernel(q_ref, km_ref, sel_ref):
    gate = jnp.sum(km_ref[0] * q_ref[...], axis=-1, keepdims=True)
    n_blk = gate.shape[0]
    blk = lax.broadcasted_iota(jnp.int32, gate.shape, 0).astype(F32)
    rank = lax.broadcasted_iota(jnp.int32, sel_ref.shape[1:], 1)
    out = jnp.zeros(sel_ref.shape[1:], F32)
    for r in range(MOBA_TOPK):
        m = jnp.max(gate, axis=0, keepdims=True)
        first = jnp.min(jnp.where(gate == m, blk, float(n_blk)), axis=0, keepdims=True)
        gate = jnp.where(blk == first, NEG_INF, gate)
        out = jnp.where(rank == r, first[0], out)
    sel_ref[0] = out.astype(jnp.int32)


def _block_select(q4, kmean_s):
    n_seq, n_blk = kmean_s.shape[:2]
    return pl.pallas_call(
        _block_select_kernel,
        grid=(n_seq,),
        in_specs=[pl.BlockSpec((1, N_HEADS, HEAD_DIM), lambda s: (s, 0, 0)),
                  pl.BlockSpec((1, n_blk, N_HEADS, HEAD_DIM), lambda s: (s, 0, 0, 0))],
        out_specs=pl.BlockSpec((1, N_HEADS, LANES), lambda s: (s, 0, 0)),
        out_shape=jax.ShapeDtypeStruct((n_seq, N_HEADS, LANES), jnp.int32),
        compiler_params=_params(1),
        name="block_select",
    )(q4, kmean_s)


def _attn_sample_kernel(n_sel, pg_ref, q_ref, k_own_ref, v_own_ref, ck_hbm, cv_hbm, o_ref, kbuf, vbuf, sem):
    s = pl.program_id(0)

    def page_copies(seq, slot):
        cps = []
        for h in range(N_HEADS):
            for i in range(n_sel):
                pg = pg_ref[(seq * N_HEADS + h) * n_sel + i]
                cps.append(pltpu.make_async_copy(ck_hbm.at[pg, :, h, :], kbuf.at[slot, h, i], sem.at[slot]))
                cps.append(pltpu.make_async_copy(cv_hbm.at[pg, :, h, :], vbuf.at[slot, h, i], sem.at[slot]))
        return cps

    @pl.when(s == 0)
    def _():
        for cp in page_copies(0, 0):
            cp.start()

    @pl.when(s + 1 < pl.num_programs(0))
    def _():
        for cp in page_copies(s + 1, (s + 1) % 2):
            cp.start()

    slot = s % 2
    for cp in page_copies(s, slot):
        cp.wait()

    scale = HEAD_DIM ** -0.5
    for h in range(N_HEADS):
        hs = slice(h * HEAD_DIM, (h + 1) * HEAD_DIM)
        q = q_ref[0, :, hs]
        s_own = jnp.sum(k_own_ref[0, :, hs] * q, axis=-1, keepdims=True) * scale
        scores = [jnp.sum(kbuf[slot, h, i] * q, axis=-1, keepdims=True) * scale for i in range(n_sel)]
        m = s_own
        for sc in scores:
            m = jnp.maximum(m, jnp.max(sc, axis=0, keepdims=True))
        p_own = jnp.exp(s_own - m)
        l = p_own
        acc = p_own * v_own_ref[0, :, hs]
        for i, sc in enumerate(scores):
            p = jnp.exp(sc - m)
            l = l + jnp.sum(p, axis=0, keepdims=True)
            acc = acc + jnp.sum(p * vbuf[slot, h, i], axis=0, keepdims=True)
        o_ref[0, :, hs] = acc / l


def _attention_sample(q3, k3, v3, cache_k, cache_v, sel_pages):
    n_seq = q3.shape[0]
    page = cache_k.shape[1]
    n_sel = sel_pages.shape[-1]
    seq_row = lambda s, pg: (s, 0, 0)
    grid_spec = pltpu.PrefetchScalarGridSpec(
        num_scalar_prefetch=1,
        grid=(n_seq,),
        in_specs=[pl.BlockSpec((1, 1, D_MODEL), seq_row)] * 3
                 + [pl.BlockSpec(memory_space=pl.ANY), pl.BlockSpec(memory_space=pl.ANY)],
        out_specs=pl.BlockSpec((1, 1, D_MODEL), seq_row),
        scratch_shapes=[pltpu.VMEM((2, N_HEADS, n_sel, page, HEAD_DIM), F32),
                        pltpu.VMEM((2, N_HEADS, n_sel, page, HEAD_DIM), F32),
                        pltpu.SemaphoreType.DMA((2,))],
    )
    return pl.pallas_call(
        functools.partial(_attn_sample_kernel, n_sel),
        grid_spec=grid_spec,
        out_shape=jax.ShapeDtypeStruct((n_seq, 1, D_MODEL), F32),
        compiler_params=_params(1),
        name="moba_sample",
    )(sel_pages.reshape(-1), q3, k3, v3, cache_k, cache_v)


def _oproj_kernel(o_ref, x2_ref, wo_ref, mg_ref, wr_ref, br_ref, x3_ref, hm_ref, route_ref):
    x3 = x2_ref[...] + _mm(o_ref[...], wo_ref[...], True)
    x3_ref[...] = x3
    _moe_prenorm_and_route(x3, mg_ref, wr_ref, br_ref, hm_ref, route_ref, True)


def _oproj(o, x2, w):
    n = o.shape[0]
    return pl.pallas_call(
        _oproj_kernel,
        grid=(1,),
        in_specs=[_full((n, D_MODEL)), _full((n, D_MODEL)), _full((D_MODEL, D_MODEL)), _full((1, D_MODEL)),
                  _full((D_MODEL, LANES)), _full((1, LANES))],
        out_specs=[_full((n, D_MODEL)), _full((n * ROW_PIECES, LANES)), _full((n, LANES))],
        out_shape=[jax.ShapeDtypeStruct((n, D_MODEL), F32), jax.ShapeDtypeStruct((n * ROW_PIECES, LANES), F32),
                   jax.ShapeDtypeStruct((n, LANES), F32)],
        compiler_params=_params(1),
        name="oproj_sample",
    )(o, x2, w["f32"]["b_w_o"], w["moe_norm_g"][1], w["f32"]["moe_w_route"][1], w["moe_b_route"][1])


def _combine_kernel(x_ref, yp_ref, route_ref, o_ref):
    o_ref[...] = _moe_residual(x_ref[...], yp_ref, route_ref)


def _combine(x, ypairs, route):
    n = x.shape[0]
    tm = min(n, 4 * ROW_TILE)
    row = lambda i: (i, 0)
    return pl.pallas_call(
        _combine_kernel,
        grid=(n // tm,),
        in_specs=[pl.BlockSpec((tm, D_MODEL), row), pl.BlockSpec((EXPERT_TOPK * tm * ROW_PIECES, LANES), row),
                  pl.BlockSpec((tm, LANES), row)],
        out_specs=pl.BlockSpec((tm, D_MODEL), row),
        out_shape=jax.ShapeDtypeStruct((n, D_MODEL), F32),
        compiler_params=_params(1),
        name="combine",
    )(x, ypairs, route)


def _rope_tables(pos):
    inv_freq = ROPE_THETA ** (-jnp.arange(ROT_HALF, dtype=F32) / ROT_HALF)
    ang = pos.astype(F32)[:, None] * inv_freq[None, :]
    cos, sin = jnp.cos(ang), jnp.sin(ang)
    rest = HEAD_DIM - ROT_DIM
    cos_tab = jnp.concatenate([cos, cos, jnp.ones((pos.shape[0], rest), F32)], axis=-1)
    sin_tab = jnp.concatenate([-sin, sin, jnp.zeros((pos.shape[0], rest), F32)], axis=-1)
    return cos_tab, sin_tab


def kernel(x_prompt, x_sample, cache_k, cache_v, page_table, a_norm_g, a_w_in, a_ln_g, a_ln_b, a_w_s, a_b_s, a_w_out, kv_norm_g, w_kv, k_norm_g, b_norm_g, b_w_q, q_norm_g, b_w_o, moe_norm_g, moe_w_rg, moe_b_rg, moe_w_re, moe_b_re, moe_w_gate, moe_w_up, moe_w_down):
    bsz, seq, _ = x_prompt.shape
    n_dec = x_sample.shape[0]
    assert x_sample.shape[1] == 1 and a_w_in.shape[0] == 1 and b_w_q.shape[0] == 1
    assert seq % ROW_TILE == 0 and ROW_TILE == MOBA_BLOCK
    page = cache_k.shape[1]
    past = page_table.shape[1] * page
    assert past % MOBA_BLOCK == 0 and past // MOBA_BLOCK >= MOBA_TOPK and MOBA_BLOCK % page == 0

    depth = moe_norm_g.shape[0]
    pad = LANES - N_EXPERT_GROUPS - N_EXPERTS
    mats = {
        "a_w_in": a_w_in[0], "a_w_out": a_w_out[0], "w_kv": w_kv, "b_w_q": b_w_q[0], "b_w_o": b_w_o[0],
        "moe_w_route": jnp.concatenate([moe_w_rg, moe_w_re, jnp.zeros((depth, D_MODEL, pad), F32)], axis=-1),
    }
    w = {
        "f32": mats, "bf16": {name: m.astype(BF16) for name, m in mats.items()},
        "moe_w_gate": moe_w_gate, "moe_w_up": moe_w_up, "moe_w_down": moe_w_down,
        "a_norm_g": a_norm_g[0][None], "a_ln_g": a_ln_g[0][None], "a_ln_b": a_ln_b[0][None],
        "a_w_s": a_w_s[0], "a_b_s_t": a_b_s[0].T,
        "a_ws0": jnp.repeat(a_w_s[0, :, 0, 0], GROUP_WIDTH)[None], "a_bs0": jnp.repeat(a_b_s[0, :, 0], GROUP_WIDTH)[None],
        "kv_norm_g": kv_norm_g[None], "k_norm_g": k_norm_g[None],
        "b_norm_g": b_norm_g[0][None], "q_norm_g": q_norm_g[0][None],
        "moe_norm_g": moe_norm_g[:, None, :],
        "moe_b_route": jnp.concatenate([moe_b_rg, moe_b_re, jnp.zeros((depth, pad), F32)], axis=-1)[:, None, :],
    }

    xp = x_prompt.reshape(bsz * seq, D_MODEL)
    xs = x_sample.reshape(n_dec, D_MODEL)
    cos_p, sin_p = _rope_tables(jnp.arange(seq, dtype=jnp.int32))
    cos_s, sin_s = _rope_tables(jnp.full((n_dec,), past, dtype=jnp.int32))

    x1p, vrows_p, hmp, route0p, kmean_s = _gmlp(xp, w, False, seq, cache_k, page_table)
    x1s, vrows_s, hms, route0s = _gmlp(xs, w, True, seq)
    yp = _experts(hmp, route0p, w, 0, EXPERT_TILE, False)
    ys = _experts(hms, route0s, w, 0, EXPERT_TILE_SMALL, True)

    x2p, k_p, v_p, q_p, kb_p, vt_p, kmean_p = _kvq(x1p, yp, route0p, cos_p, sin_p, w, True, False)
    x2s, k_s, v_s, q_s = _kvq(x1s, ys, route0s, cos_s, sin_s, w, False, True)

    x3p, hmp, route1p = _attention_prompt(q_p, kb_p, vt_p, kmean_p, x2p, w, seq)
    sel_blocks = _block_select(q_s.reshape(n_dec, N_HEADS, HEAD_DIM), kmean_s)[:, :, :MOBA_TOPK]
    pages_per_block = MOBA_BLOCK // page
    page_slots = sel_blocks[..., None] * pages_per_block + jnp.arange(pages_per_block, dtype=jnp.int32)
    page_slots = page_slots.reshape(n_dec, N_HEADS * MOBA_TOPK * pages_per_block)
    sel_pages = jnp.take_along_axis(page_table, page_slots, axis=1).reshape(n_dec, N_HEADS, MOBA_TOPK * pages_per_block)
    o_s = _attention_sample(q_s.reshape(n_dec, 1, D_MODEL), k_s.reshape(n_dec, 1, D_MODEL),
                            v_s.reshape(n_dec, 1, D_MODEL), cache_k, cache_v, sel_pages)
    x3s, hms, route1s = _oproj(o_s.reshape(n_dec, D_MODEL), x2s, w)
    yp = _experts(hmp, route1p, w, 1, EXPERT_TILE, False)
    ys = _experts(hms, route1s, w, 1, EXPERT_TILE_SMALL, True)
    y_prompt = _combine(x3p, yp, route1p)
    y_sample = _combine(x3s, ys, route1s)

    return (y_prompt.reshape(bsz, seq, D_MODEL), y_sample.reshape(n_dec, 1, D_MODEL),
            vrows_p.reshape(1, bsz, CHUNK, GATE_WIDTH), vrows_s.reshape(1, n_dec, 1, GATE_WIDTH),
            k_p.reshape(bsz, seq, N_HEADS, HEAD_DIM), v_p.reshape(bsz, seq, N_HEADS, HEAD_DIM),
            k_s.reshape(n_dec, 1, N_HEADS, HEAD_DIM), v_s.reshape(n_dec, 1, N_HEADS, HEAD_DIM))
```

```python
import functools

import jax
import jax.numpy as jnp
from jax import lax
from jax.experimental import pallas as pl
from jax.experimental.pallas import tpu as pltpu

D_MODEL = 1024
CHUNK = 128
GATE_WIDTH = 2 * D_MODEL
N_GATE_GROUPS = 8
GROUP_WIDTH = GATE_WIDTH // N_GATE_GROUPS
HEAD_DIM = 128
N_HEADS = D_MODEL // HEAD_DIM
ROT_DIM = HEAD_DIM // 4
ROT_HALF = ROT_DIM // 2
ROPE_THETA = 500000.0
MOBA_BLOCK = 256
MOBA_TOPK = 3
N_EXPERT_GROUPS = 4
EXPERTS_PER_GROUP = 4
N_EXPERTS = N_EXPERT_GROUPS * EXPERTS_PER_GROUP
EXPERT_TOPK = 2
EXPERT_FF = D_MODEL // 2
EPS = 1e-6

LANES = 128
SUBLANES = 8
ROW_PIECES = D_MODEL // LANES
ROW_TILE = 256
EXPERT_TILE = 256
EXPERT_TILE_SMALL = 16
IN_PROJ_COLS = 512
N_IN_PROJ_CHUNKS = 2 * GATE_WIDTH // IN_PROJ_COLS
PAGE_BATCH = 16
N_PAGE_BATCHES = 2
VMEM_LIMIT = 60 * 1024 * 1024

F32 = jnp.float32
BF16 = jnp.bfloat16
NEG_INF = float("-inf")

assert ROW_PIECES == SUBLANES


def _params(n_axes):
    return pltpu.CompilerParams(dimension_semantics=("arbitrary",) * n_axes, vmem_limit_bytes=VMEM_LIMIT)


def _rms(x, g):
    return x * lax.rsqrt(jnp.mean(x * x, axis=-1, keepdims=True) + EPS) * g


def _gelu_tanh(z):
    c = 0.7978845608028654
    half_z = 0.5 * z
    return half_z + half_z * jnp.tanh(z * (c + (c * 0.044715) * (z * z)))


def _mm(a, b, exact):
    if exact:
        return jnp.dot(a, b, precision=lax.Precision.HIGHEST, preferred_element_type=F32)
    return jnp.dot(a.astype(BF16), b, preferred_element_type=F32)


def _full(shape):
    nd = len(shape)
    return pl.BlockSpec(shape, lambda *_: (0,) * nd, pipeline_mode=pl.Buffered(1))


def _store_tile_rows(ref, x):
    for s in range(ROW_PIECES):
        ref[pl.ds(s, x.shape[0], stride=ROW_PIECES), :] = x[:, s * LANES:(s + 1) * LANES]


def _load_tile_rows(ref, rows, first=0, step=ROW_PIECES):
    return jnp.concatenate([ref[pl.ds(first + s, rows, stride=step), :] for s in range(ROW_PIECES)], axis=-1)


def _route(hm, wr_ref, br_ref, exact):
    logits = _mm(hm, wr_ref[...], exact) + br_ref[...]
    lane = lax.broadcasted_iota(jnp.int32, logits.shape, 1).astype(F32)
    lg = jnp.where(lane < N_EXPERT_GROUPS, logits, NEG_INF)
    mg = jnp.max(lg, axis=-1, keepdims=True)
    g_sel = jnp.min(jnp.where(lg == mg, lane, float(LANES)), axis=-1, keepdims=True)
    p_grp = 1.0 / jnp.sum(jnp.exp(lg - mg), axis=-1, keepdims=True)
    lo = N_EXPERT_GROUPS + g_sel * EXPERTS_PER_GROUP
    le = jnp.where((lane >= lo) & (lane < lo + EXPERTS_PER_GROUP), logits, NEG_INF)
    v1 = jnp.max(le, axis=-1, keepdims=True)
    i1 = jnp.min(jnp.where(le == v1, lane, float(LANES)), axis=-1, keepdims=True)
    le2 = jnp.where(lane == i1, NEG_INF, le)
    v2 = jnp.max(le2, axis=-1, keepdims=True)
    i2 = jnp.min(jnp.where(le2 == v2, lane, float(LANES)), axis=-1, keepdims=True)
    t = jnp.exp(v2 - v1)
    w1 = p_grp / (1.0 + t)
    w2 = p_grp * t / (1.0 + t)
    out = jnp.where(lane == 0.0, i1 - N_EXPERT_GROUPS, 0.0)
    out = jnp.where(lane == 1.0, i2 - N_EXPERT_GROUPS, out)
    out = jnp.where(lane == 2.0, w1, out)
    out = jnp.where(lane == 3.0, w2, out)
    return out


def _moe_prenorm_and_route(x, mg_ref, wr_ref, br_ref, hm_ref, route_ref, exact):
    hm = _rms(x, mg_ref[...])
    _store_tile_rows(hm_ref, hm)
    route_ref[...] = _route(hm, wr_ref, br_ref, exact)


def _moe_residual(x, y_ref, route_ref):
    rows = x.shape[0]
    step = EXPERT_TOPK * ROW_PIECES
    y1 = _load_tile_rows(y_ref, rows, 0, step)
    y2 = _load_tile_rows(y_ref, rows, ROW_PIECES, step)
    return x + (route_ref[:, 2:3] * y1 + route_ref[:, 3:4] * y2)


def _page_batch_copies(pt_ref, ck_hbm, pbuf, psem, step, k):
    n_pages = pt_ref.shape[1]
    first = (step * N_PAGE_BATCHES + k) * PAGE_BATCH
    seq, page0 = first // n_pages, first % n_pages
    return [pltpu.make_async_copy(ck_hbm.at[pt_ref[seq, page0 + p]], pbuf.at[k, p], psem.at[k])
            for p in range(PAGE_BATCH)]


def _reduce_page_batch(pbuf, km_ref, k, pages_per_block):
    blocks = PAGE_BATCH // pages_per_block
    for b in range(blocks):
        tot = jnp.sum(pbuf[k, b * pages_per_block], axis=0)
        for p in range(1, pages_per_block):
            tot = tot + jnp.sum(pbuf[k, b * pages_per_block + p], axis=0)
        km_ref[0, k * blocks + b] = tot / MOBA_BLOCK


def _gmlp_kernel(sample, tm, *refs):
    if sample:
        (x_ref, ng_ref, win_ref, lng_ref, lnb_ref, ws_ref, bs_ref, wout_ref, mg_ref, wr_ref, br_ref,
         x1_ref, vrows_ref, hm_ref, route_ref, u_scr, v_scr, gated_scr) = refs
    else:
        (pt_ref, x_ref, ng_ref, win_ref, lng_ref, lnb_ref, ws_ref, bs_ref, wout_ref, mg_ref, wr_ref, br_ref, ck_hbm,
         x1_ref, vrows_ref, hm_ref, route_ref, km_ref, u_scr, v_scr, gated_scr, pbuf, psem) = refs
        step = pl.program_id(0)
        batch = functools.partial(_page_batch_copies, pt_ref, ck_hbm, pbuf, psem)
        pages_per_block = MOBA_BLOCK // ck_hbm.shape[1]

        @pl.when(step == 0)
        def _():
            for cp in batch(step, 0):
                cp.start()

        for cp in batch(step, 1):
            cp.start()

    x = x_ref[...]
    h = _rms(x, ng_ref[...])
    if not sample:
        h = h.astype(BF16)
    n_col = N_IN_PROJ_CHUNKS
    n_ucol = GATE_WIDTH // IN_PROJ_COLS
    vsum = jnp.zeros((tm, 1), F32)
    for c in range(n_col):
        z = _mm(h, win_ref[:, c * IN_PROJ_COLS:(c + 1) * IN_PROJ_COLS], sample)
        z = _gelu_tanh(z)
        if c < n_ucol:
            u_scr[:, c * IN_PROJ_COLS:(c + 1) * IN_PROJ_COLS] = z
        else:
            v_scr[:, (c - n_ucol) * IN_PROJ_COLS:(c - n_ucol + 1) * IN_PROJ_COLS] = z
            vsum = vsum + jnp.sum(z, axis=-1, keepdims=True)

    if not sample:
        for cp in batch(step, 0):
            cp.wait()
        _reduce_page_batch(pbuf, km_ref, 0, pages_per_block)

        @pl.when(step + 1 < pl.num_programs(0))
        def _():
            for cp in batch(step + 1, 0):
                cp.start()

    mean = vsum / GATE_WIDTH
    vss = jnp.zeros((tm, 1), F32)
    for c in range(n_ucol):
        xc = v_scr[:, c * IN_PROJ_COLS:(c + 1) * IN_PROJ_COLS] - mean
        vss = vss + jnp.sum(xc * xc, axis=-1, keepdims=True)
    rstd = lax.rsqrt(vss / GATE_WIDTH + EPS)
    for c in range(n_ucol):
        cs = slice(c * IN_PROJ_COLS, (c + 1) * IN_PROJ_COLS)
        v_scr[:, cs] = (v_scr[:, cs] - mean) * rstd * lng_ref[:, cs] + lnb_ref[:, cs]

    if sample:
        vrows_ref[...] = v_scr[...]
        gated_scr[...] = u_scr[...] * (v_scr[...] * ws_ref[...] + bs_ref[...])
    else:
        vrows_ref[0] = v_scr[tm - CHUNK:tm, :]
        t_out = lax.broadcasted_iota(jnp.int32, (CHUNK, CHUNK), 0)
        s_in = lax.broadcasted_iota(jnp.int32, (CHUNK, CHUNK), 1)
        n_chunks = tm // CHUNK
        zero = jnp.zeros((CHUNK, CHUNK), BF16)
        for g in range(N_GATE_GROUPS):
            gs = slice(g * GROUP_WIDTH, (g + 1) * GROUP_WIDTH)
            wsg = jnp.where(s_in <= t_out, ws_ref[g], 0.0).astype(BF16)
            ws_tile = jnp.concatenate(
                [jnp.concatenate([wsg if c == r else zero for c in range(n_chunks)], axis=1)
                 for r in range(n_chunks)], axis=0)
            bias = jnp.concatenate([bs_ref[:, g:g + 1]] * n_chunks, axis=0)
            mixed = jnp.dot(ws_tile, v_scr[:, gs].astype(BF16), preferred_element_type=F32) + bias
            gated_scr[:, gs] = (u_scr[:, gs] * mixed).astype(BF16)

    x1 = x + _mm(gated_scr[...], wout_ref[...], sample)
    x1_ref[...] = x1
    if not sample:
        for cp in batch(step, 1):
            cp.wait()
        _reduce_page_batch(pbuf, km_ref, 1, pages_per_block)
    _moe_prenorm_and_route(x1, mg_ref, wr_ref, br_ref, hm_ref, route_ref, sample)


def _gmlp(x, w, sample, seq, cache_k=None, page_table=None):
    n = x.shape[0]
    mw = w["f32"] if sample else w["bf16"]
    tm = n if sample else ROW_TILE
    n_steps = n // tm
    row = lambda i, *_: (i, 0)
    in_specs = [pl.BlockSpec((tm, D_MODEL), row), _full((1, D_MODEL)), _full((D_MODEL, 2 * GATE_WIDTH)),
                _full((1, GATE_WIDTH)), _full((1, GATE_WIDTH)), None, None, _full((GATE_WIDTH, D_MODEL)),
                _full((1, D_MODEL)), _full((D_MODEL, LANES)), _full((1, LANES))]
    out_specs = [pl.BlockSpec((tm, D_MODEL), row), None, pl.BlockSpec((tm * ROW_PIECES, LANES), row),
                 pl.BlockSpec((tm, LANES), row)]
    out_shape = [jax.ShapeDtypeStruct((n, D_MODEL), F32), None,
                 jax.ShapeDtypeStruct((n * ROW_PIECES, LANES), F32), jax.ShapeDtypeStruct((n, LANES), F32)]
    scratch = [pltpu.VMEM((tm, GATE_WIDTH), F32), pltpu.VMEM((tm, GATE_WIDTH), F32),
               pltpu.VMEM((tm, GATE_WIDTH), F32 if sample else BF16)]
    args = [x, w["a_norm_g"], mw["a_w_in"], w["a_ln_g"], w["a_ln_b"],
            w["a_ws0"] if sample else w["a_w_s"], w["a_bs0"] if sample else w["a_b_s_t"], mw["a_w_out"],
            w["moe_norm_g"][0], mw["moe_w_route"][0], w["moe_b_route"][0]]
    if sample:
        in_specs[5], in_specs[6] = _full((1, GATE_WIDTH)), _full((1, GATE_WIDTH))
        out_shape[1] = jax.ShapeDtypeStruct((n, GATE_WIDTH), F32)
        out_specs[1] = pl.BlockSpec((tm, GATE_WIDTH), row)
        n_prefetch = 0
    else:
        in_specs[5], in_specs[6] = _full((N_GATE_GROUPS, CHUNK, CHUNK)), _full((CHUNK, N_GATE_GROUPS))
        tiles_per_seq = seq // tm
        out_shape[1] = jax.ShapeDtypeStruct((n // seq, CHUNK, GATE_WIDTH), F32)
        out_specs[1] = pl.BlockSpec((1, CHUNK, GATE_WIDTH), lambda i, *_: (i // tiles_per_seq, 0, 0))
        n_dec, n_pages = page_table.shape
        page = cache_k.shape[1]
        pages_per_step = N_PAGE_BATCHES * PAGE_BATCH
        assert n_dec * n_pages == n_steps * pages_per_step and n_pages % pages_per_step == 0
        blocks_per_step = pages_per_step * page // MOBA_BLOCK
        steps_per_seq = n_pages // pages_per_step
        in_specs.append(pl.BlockSpec(memory_space=pl.ANY))
        args = [page_table] + args + [cache_k]
        out_specs.append(pl.BlockSpec((1, blocks_per_step, N_HEADS, HEAD_DIM),
                                      lambda i, *_: (i // steps_per_seq, i % steps_per_seq, 0, 0)))
        out_shape.append(jax.ShapeDtypeStruct((n_dec, n_pages * page // MOBA_BLOCK, N_HEADS, HEAD_DIM), F32))
        scratch += [pltpu.VMEM((N_PAGE_BATCHES, PAGE_BATCH, page, N_HEADS, HEAD_DIM), F32),
                    pltpu.SemaphoreType.DMA((N_PAGE_BATCHES,))]
        n_prefetch = 1
    grid_spec = pltpu.PrefetchScalarGridSpec(num_scalar_prefetch=n_prefetch, grid=(n_steps,), in_specs=in_specs,
                                             out_specs=out_specs, scratch_shapes=scratch)
    return pl.pallas_call(
        functools.partial(_gmlp_kernel, sample, tm),
        grid_spec=grid_spec,
        out_shape=out_shape,
        compiler_params=_params(1),
        name="gmlp_sample" if sample else "gmlp_prompt",
    )(*args)


def _expert_kernel(tmg, n_pairs, n_tiles, exact, te_ref, nv_ref, ts_ref, od_ref, h_hbm, wg_ref, wu_ref, wd_ref, out_hbm,
                   xbuf0, xbuf1, ybuf0, ybuf1, sem_in, sem_out, *bf16_weights):
    i = pl.program_id(0)
    xbufs, ybufs = (xbuf0, xbuf1), (ybuf0, ybuf1)
    rows8 = SUBLANES
    weights = (wg_ref, wu_ref, wd_ref) if exact else bf16_weights

    def valid(t):
        return (t >= 0) & (t < n_tiles) & (nv_ref[jnp.clip(t, 0, n_tiles - 1)] > 0)

    def gather_row(r, p, tok=0):
        return pltpu.make_async_copy(h_hbm.at[pl.ds(pl.multiple_of(tok * rows8, rows8), rows8)],
                                     xbufs[p].at[pl.ds(r * rows8, rows8)], sem_in.at[p])

    def scatter_row(r, p, pair=0):
        row8 = r * rows8 if isinstance(r, int) else pl.multiple_of(r * rows8, rows8)
        return pltpu.make_async_copy(ybufs[p].at[pl.ds(row8, rows8)],
                                     out_hbm.at[pl.ds(pl.multiple_of(pair * rows8, rows8), rows8)], sem_out.at[p])

    def start_gather(t, p):
        base = ts_ref[t]
        for r in range(tmg):
            gather_row(r, p, od_ref[base + r] >> 1).start()

    def for_scatter_rows(t, p, start, all_rows):
        base = ts_ref[t]

        def one(r):
            if start:
                scatter_row(r, p, od_ref[base + r]).start()
            else:
                scatter_row(r, p).wait()

        if all_rows:
            for r in range(tmg):
                one(r)
        else:
            def body(r, carry):
                one(r)
                return carry
            lax.fori_loop(0, jnp.minimum(tmg, n_pairs - base), body, 0)

    def refresh_weights(t):
        if not exact:
            @pl.when((t == 0) | (te_ref[t] != te_ref[jnp.maximum(t - 1, 0)]))
            def _():
                for w_bf16, w_ref in zip(bf16_weights, (wg_ref, wu_ref, wd_ref)):
                    w_bf16[...] = w_ref[...].astype(BF16)

    def compute(p):
        wg, wu, wd = weights
        x = _load_tile_rows(xbufs[p], tmg)
        a = jax.nn.silu(_mm(x, wg[...], exact)) * _mm(x, wu[...], exact)
        _store_tile_rows(ybufs[p], _mm(a, wd[...], exact))

    def step(p, guarded):
        when = pl.when if guarded else (lambda cond: (lambda body: body()))

        @when(valid(i - 1))
        def _():
            for r in range(tmg):
                gather_row(r, 1 - p).wait()
            refresh_weights(i - 1)

        @when(valid(i - 3))
        def _():
            for_scatter_rows(i - 3, 1 - p, False, not guarded)

        @when(valid(i - 2))
        def _():
            for_scatter_rows(i - 2, p, True, not guarded)

        @when(valid(i))
        def _():
            start_gather(i, p)

        @when(valid(i - 1))
        def _():
            compute(1 - p)

    steady = (i >= 3) & valid(i) & (ts_ref[jnp.clip(i - 2, 0, n_tiles - 1)] + tmg <= n_pairs)
    for p in range(2):
        @pl.when(steady & (i % 2 == p))
        def _():
            step(p, False)

        @pl.when(jnp.logical_not(steady) & (i % 2 == p))
        def _():
            step(p, True)


def _dispatch(route, tmg):
    n_pairs = EXPERT_TOPK * route.shape[0]
    e_flat = route[:, 0:EXPERT_TOPK].astype(jnp.int32).reshape(n_pairs)
    n_tiles = -(-(n_pairs + N_EXPERTS * (tmg - 1)) // tmg)
    order = jnp.argsort(e_flat, stable=True).astype(jnp.int32)
    counts = jnp.sum((e_flat[:, None] == jnp.arange(N_EXPERTS, dtype=jnp.int32)[None, :]).astype(jnp.int32), axis=0)
    tiles_per = -(-counts // tmg)
    t_end = jnp.cumsum(tiles_per)
    t_start = t_end - tiles_per
    c_start = jnp.cumsum(counts) - counts
    tile = jnp.arange(n_tiles, dtype=jnp.int32)
    tile_expert = jnp.minimum(jnp.sum((t_end[None, :] <= tile[:, None]).astype(jnp.int32), axis=1), N_EXPERTS - 1)
    first_row = (tile - t_start[tile_expert]) * tmg
    tile_valid = jnp.clip(counts[tile_expert] - first_row, 0, tmg).astype(jnp.int32)
    tile_src = jnp.where(tile_valid > 0, c_start[tile_expert] + first_row, 0).astype(jnp.int32)
    order = jnp.concatenate([order, jnp.zeros((tmg,), jnp.int32)])
    return tile_expert.astype(jnp.int32), tile_valid, tile_src, order


def _experts(hm, route, w, layer, tmg, exact):
    n = route.shape[0]
    w_gate, w_up, w_down = w["moe_w_gate"], w["moe_w_up"], w["moe_w_down"]
    tile_expert, tile_valid, tile_src, order = _dispatch(route, tmg)
    n_tiles = tile_expert.shape[0]
    by_expert = lambda i, te, nv, ts, od: (layer, te[jnp.clip(i - 1, 0, n_tiles - 1)], 0, 0)
    scratch = [pltpu.VMEM((tmg * ROW_PIECES, LANES), F32)] * 4 \
        + [pltpu.SemaphoreType.DMA((2,)), pltpu.SemaphoreType.DMA((2,))]
    if not exact:
        scratch += [pltpu.VMEM((D_MODEL, EXPERT_FF), BF16), pltpu.VMEM((D_MODEL, EXPERT_FF), BF16),
                    pltpu.VMEM((EXPERT_FF, D_MODEL), BF16)]
    grid_spec = pltpu.PrefetchScalarGridSpec(
        num_scalar_prefetch=4,
        grid=(n_tiles + 3,),
        in_specs=[pl.BlockSpec(memory_space=pl.ANY),
                  pl.BlockSpec((None, None, D_MODEL, EXPERT_FF), by_expert),
                  pl.BlockSpec((None, None, D_MODEL, EXPERT_FF), by_expert),
                  pl.BlockSpec((None, None, EXPERT_FF, D_MODEL), by_expert)],
        out_specs=pl.BlockSpec(memory_space=pl.ANY),
        scratch_shapes=scratch,
    )
    return pl.pallas_call(
        functools.partial(_expert_kernel, tmg, EXPERT_TOPK * n, n_tiles, exact),
        grid_spec=grid_spec,
        out_shape=jax.ShapeDtypeStruct((EXPERT_TOPK * n * ROW_PIECES, LANES), F32),
        compiler_params=_params(1),
        name="experts",
    )(tile_expert, tile_valid, tile_src, order, hm, w_gate, w_up, w_down)


def _rope(xh, cos, sin, lane):
    swapped = jnp.where(lane < ROT_HALF, pltpu.roll(xh, LANES - ROT_HALF, 1), pltpu.roll(xh, ROT_HALF, 1))
    return xh * cos + swapped * sin


def _kvq_kernel(prompt, exact, x1_ref, yp_ref, route_ref, kvg_ref, wkv_ref, kg_ref, cos_ref, sin_ref,
                bng_ref, wq_ref, qg_ref, x2_ref, k_ref, v_ref, q_ref, *prompt_refs):
    x2 = _moe_residual(x1_ref[...], yp_ref, route_ref)
    x2_ref[...] = x2
    cos = cos_ref[...]
    sin = sin_ref[...]
    lane = lax.broadcasted_iota(jnp.int32, cos.shape, 1)
    kv = _mm(_rms(x2, kvg_ref[...]), wkv_ref[...], exact)
    for h in range(N_HEADS):
        hs = slice(h * HEAD_DIM, (h + 1) * HEAD_DIM)
        kh = _rope(_rms(kv[:, hs], kg_ref[...]), cos, sin, lane)
        k_ref[:, hs] = kh
        if prompt:
            kb_ref, _, km_ref = prompt_refs
            kb_ref[:, hs] = kh.astype(BF16)
            km_ref[0, :, hs] = jnp.mean(kh, axis=0, keepdims=True)
    v = kv[:, D_MODEL:]
    v_ref[...] = v
    if prompt:
        prompt_refs[1][0] = v.T.astype(BF16)
    q = _mm(_rms(x2, bng_ref[...]), wq_ref[...], exact)
    for h in range(N_HEADS):
        hs = slice(h * HEAD_DIM, (h + 1) * HEAD_DIM)
        q_ref[:, hs] = _rope(_rms(q[:, hs], qg_ref[...]), cos, sin, lane).astype(q_ref.dtype)


def _kvq(x1, ypairs, route, cos_tab, sin_tab, w, prompt, exact):
    n = x1.shape[0]
    mw = w["f32"] if exact else w["bf16"]
    tm = ROW_TILE if prompt else n
    row = lambda i: (i, 0)
    tab_tiles = cos_tab.shape[0] // tm
    tab = lambda i: (i % tab_tiles, 0)
    out_specs = [pl.BlockSpec((tm, D_MODEL), row)] * 4
    out_shape = [jax.ShapeDtypeStruct((n, D_MODEL), F32)] * 3 + [jax.ShapeDtypeStruct((n, D_MODEL), BF16 if prompt else F32)]
    if prompt:
        seq = cos_tab.shape[0]
        out_specs += [pl.BlockSpec((tm, D_MODEL), row),
                      pl.BlockSpec((1, D_MODEL, tm), lambda i: (i // tab_tiles, 0, i % tab_tiles)),
                      pl.BlockSpec((1, 1, D_MODEL), lambda i: (i, 0, 0))]
        out_shape += [jax.ShapeDtypeStruct((n, D_MODEL), BF16), jax.ShapeDtypeStruct((n // seq, D_MODEL, seq), BF16),
                      jax.ShapeDtypeStruct((n // tm, 1, D_MODEL), F32)]
    return pl.pallas_call(
        functools.partial(_kvq_kernel, prompt, exact),
        grid=(n // tm,),
        in_specs=[pl.BlockSpec((tm, D_MODEL), row), pl.BlockSpec((EXPERT_TOPK * tm * ROW_PIECES, LANES), row),
                  pl.BlockSpec((tm, LANES), row),
                  _full((1, D_MODEL)), _full((D_MODEL, 2 * D_MODEL)), _full((1, HEAD_DIM)),
                  pl.BlockSpec((tm, HEAD_DIM), tab), pl.BlockSpec((tm, HEAD_DIM), tab),
                  _full((1, D_MODEL)), _full((D_MODEL, D_MODEL)), _full((1, HEAD_DIM))],
        out_specs=out_specs,
        out_shape=out_shape,
        compiler_params=_params(1),
        name="kvq",
    )(x1, ypairs, route, w["kv_norm_g"], mw["w_kv"], w["k_norm_g"], cos_tab, sin_tab,
      w["b_norm_g"], mw["b_w_q"], w["q_norm_g"])


def _attn_kernel(q_ref, kb_ref, vt_ref, km_ref, x2_ref, wo_ref, mg_ref, wr_ref, br_ref,
                 x3_ref, hm_ref, route_ref, sel_scr, o_scr, *acc_scrs):
    j = pl.program_id(1)
    tq = q_ref.shape[0]
    n_blk = km_ref.shape[1]
    scale_log2e = HEAD_DIM ** -0.5 * 1.4426950408889634
    blk = lax.broadcasted_iota(jnp.int32, (n_blk, tq), 0).astype(F32)
    cur = j.astype(F32)
    contract_last = (((1,), (1,)), ((), ()))
    own = pl.multiple_of(j * MOBA_BLOCK, MOBA_BLOCK)
    chains = [(h, g) for h in range(N_HEADS) for g in range(tq // LANES)]

    def scores(h, g, off):
        hs = slice(h * HEAD_DIM, (h + 1) * HEAD_DIM)
        return lax.dot_general(kb_ref[pl.ds(off, MOBA_BLOCK), hs], q_ref[g * LANES:(g + 1) * LANES, hs],
                               contract_last, preferred_element_type=F32) * scale_log2e

    def weighted_values(h, off, p):
        hs = slice(h * HEAD_DIM, (h + 1) * HEAD_DIM)
        return jnp.dot(vt_ref[0, hs, pl.ds(off, MOBA_BLOCK)], p.astype(BF16), preferred_element_type=F32)

    for h in range(N_HEADS):
        hs = slice(h * HEAD_DIM, (h + 1) * HEAD_DIM)
        gate = lax.dot_general(km_ref[0, :, hs].astype(BF16), q_ref[:, hs], contract_last,
                               preferred_element_type=F32)
        gate = jnp.where(blk < cur, gate, NEG_INF)
        sel = jnp.zeros((n_blk, tq), F32)
        for _ in range(MOBA_TOPK):
            m = jnp.max(gate, axis=0, keepdims=True)
            first = jnp.min(jnp.where(gate == m, blk, float(n_blk)), axis=0, keepdims=True)
            pick = (blk == first) & (m > NEG_INF)
            sel = jnp.where(pick, 1.0, sel)
            gate = jnp.where(pick, NEG_INF, gate)
        sel_scr[h] = sel

    k_pos = lax.broadcasted_iota(jnp.int32, (MOBA_BLOCK, LANES), 0)
    q_pos = lax.broadcasted_iota(jnp.int32, (MOBA_BLOCK, LANES), 1)
    stats = []
    for h, g in chains:
        s = jnp.where(k_pos <= q_pos + g * LANES, scores(h, g, own), NEG_INF)
        m0 = jnp.max(s, axis=0, keepdims=True)
        p = jnp.exp2(s - m0)
        stats += [m0, jnp.sum(p, axis=0, keepdims=True)]
        acc_scrs[h][:, g * LANES:(g + 1) * LANES] = weighted_values(h, own, p)

    def past_block(jj, stats):
        off = pl.multiple_of(jj * MOBA_BLOCK, MOBA_BLOCK)
        new_stats = []
        visible = [sel_scr[h, pl.ds(jj, 1), :] for h in range(N_HEADS)]
        for c, (h, g) in enumerate(chains):
            qs = slice(g * LANES, (g + 1) * LANES)
            m_old, l_old = stats[2 * c], stats[2 * c + 1]
            s = jnp.where(visible[h][:, qs] > 0.0, scores(h, g, off), NEG_INF)
            m_new = jnp.maximum(m_old, jnp.max(s, axis=0, keepdims=True))
            alpha = jnp.exp2(m_old - m_new)
            p = jnp.exp2(s - m_new)
            new_stats += [m_new, alpha * l_old + jnp.sum(p, axis=0, keepdims=True)]
            acc_scrs[h][:, qs] = alpha * acc_scrs[h][:, qs] + weighted_values(h, off, p)
        return tuple(new_stats)

    stats = lax.fori_loop(0, j, past_block, tuple(stats))

    for c, (h, g) in enumerate(chains):
        qs = slice(g * LANES, (g + 1) * LANES)
        o_scr[qs, h * HEAD_DIM:(h + 1) * HEAD_DIM] = (acc_scrs[h][:, qs] / stats[2 * c + 1]).T.astype(BF16)

    x3 = x2_ref[...] + jnp.dot(o_scr[...], wo_ref[...], preferred_element_type=F32)
    x3_ref[...] = x3
    _moe_prenorm_and_route(x3, mg_ref, wr_ref, br_ref, hm_ref, route_ref, False)


def _attention_prompt(q, kb, vt, kmean, x2, w, seq):
    n = q.shape[0]
    bsz = n // seq
    n_qb = seq // MOBA_BLOCK
    row = lambda b, j: (b * n_qb + j, 0)
    return pl.pallas_call(
        _attn_kernel,
        grid=(bsz, n_qb),
        in_specs=[pl.BlockSpec((MOBA_BLOCK, D_MODEL), row),
                  pl.BlockSpec((seq, D_MODEL), lambda b, j: (b, 0)),
                  pl.BlockSpec((1, D_MODEL, seq), lambda b, j: (b, 0, 0)),
                  pl.BlockSpec((1, n_qb, D_MODEL), lambda b, j: (b, 0, 0)),
                  pl.BlockSpec((MOBA_BLOCK, D_MODEL), row),
                  _full((D_MODEL, D_MODEL)), _full((1, D_MODEL)), _full((D_MODEL, LANES)), _full((1, LANES))],
        out_specs=[pl.BlockSpec((MOBA_BLOCK, D_MODEL), row), pl.BlockSpec((MOBA_BLOCK * ROW_PIECES, LANES), row),
                   pl.BlockSpec((MOBA_BLOCK, LANES), row)],
        out_shape=[jax.ShapeDtypeStruct((n, D_MODEL), F32), jax.ShapeDtypeStruct((n * ROW_PIECES, LANES), F32),
                   jax.ShapeDtypeStruct((n, LANES), F32)],
        scratch_shapes=[pltpu.VMEM((N_HEADS, n_qb, MOBA_BLOCK), F32),
                        pltpu.VMEM((MOBA_BLOCK, D_MODEL), BF16)]
                       + [pltpu.VMEM((HEAD_DIM, MOBA_BLOCK), F32)] * N_HEADS,
        compiler_params=_params(2),
        name="moba_prompt",
    )(q, kb, vt, kmean.reshape(bsz, n_qb, D_MODEL), x2, w["bf16"]["b_w_o"],
      w["moe_norm_g"][1], w["bf16"]["moe_w_route"][1], w["moe_b_route"][1])


def _block_select_kernel(q_ref, km_ref, sel_ref):
    gate = jnp.sum(km_ref[0] * q_ref[...], axis=-1, keepdims=True)
    n_blk = gate.shape[0]
    blk = lax.broadcasted_iota(jnp.int32, gate.shape, 0).astype(F32)
    rank = lax.broadcasted_iota(jnp.int32, sel_ref.shape[1:], 1)
    out = jnp.zeros(sel_ref.shape[1:], F32)
    for r in range(MOBA_TOPK):
        m = jnp.max(gate, axis=0, keepdims=True)
        first = jnp.min(jnp.where(gate == m, blk, float(n_blk)), axis=0, keepdims=True)
        gate = jnp.where(blk == first, NEG_INF, gate)
        out = jnp.where(rank == r, first[0], out)
    sel_ref[0] = out.astype(jnp.int32)


def _block_select(q4, kmean_s):
    n_seq, n_blk = kmean_s.shape[:2]
    return pl.pallas_call(
        _block_select_kernel,
        grid=(n_seq,),
        in_specs=[pl.BlockSpec((1, N_HEADS, HEAD_DIM), lambda s: (s, 0, 0)),
                  pl.BlockSpec((1, n_blk, N_HEADS, HEAD_DIM), lambda s: (s, 0, 0, 0))],
        out_specs=pl.BlockSpec((1, N_HEADS, LANES), lambda s: (s, 0, 0)),
        out_shape=jax.ShapeDtypeStruct((n_seq, N_HEADS, LANES), jnp.int32),
        compiler_params=_params(1),
        name="block_select",
    )(q4, kmean_s)


def _attn_sample_kernel(n_sel, pg_ref, q_ref, k_own_ref, v_own_ref, ck_hbm, cv_hbm, o_ref, kbuf, vbuf, sem):
    s = pl.program_id(0)

    def page_copies(seq, slot):
        cps = []
        for h in range(N_HEADS):
            for i in range(n_sel):
                pg = pg_ref[(seq * N_HEADS + h) * n_sel + i]
                cps.append(pltpu.make_async_copy(ck_hbm.at[pg, :, h, :], kbuf.at[slot, h, i], sem.at[slot]))
                cps.append(pltpu.make_async_copy(cv_hbm.at[pg, :, h, :], vbuf.at[slot, h, i], sem.at[slot]))
        return cps

    @pl.when(s == 0)
    def _():
        for cp in page_copies(0, 0):
            cp.start()

    @pl.when(s + 1 < pl.num_programs(0))
    def _():
        for cp in page_copies(s + 1, (s + 1) % 2):
            cp.start()

    slot = s % 2
    for cp in page_copies(s, slot):
        cp.wait()

    scale = HEAD_DIM ** -0.5
    for h in range(N_HEADS):
        hs = slice(h * HEAD_DIM, (h + 1) * HEAD_DIM)
        q = q_ref[0, :, hs]
        s_own = jnp.sum(k_own_ref[0, :, hs] * q, axis=-1, keepdims=True) * scale
        scores = [jnp.sum(kbuf[slot, h, i] * q, axis=-1, keepdims=True) * scale for i in range(n_sel)]
        m = s_own
        for sc in scores:
            m = jnp.maximum(m, jnp.max(sc, axis=0, keepdims=True))
        p_own = jnp.exp(s_own - m)
        l = p_own
        acc = p_own * v_own_ref[0, :, hs]
        for i, sc in enumerate(scores):
            p = jnp.exp(sc - m)
            l = l + jnp.sum(p, axis=0, keepdims=True)
            acc = acc + jnp.sum(p * vbuf[slot, h, i], axis=0, keepdims=True)
        o_ref[0, :, hs] = acc / l


def _attention_sample(q3, k3, v3, cache_k, cache_v, sel_pages):
    n_seq = q3.shape[0]
    page = cache_k.shape[1]
    n_sel = sel_pages.shape[-1]
    seq_row = lambda s, pg: (s, 0, 0)
    grid_spec = pltpu.PrefetchScalarGridSpec(
        num_scalar_prefetch=1,
        grid=(n_seq,),
        in_specs=[pl.BlockSpec((1, 1, D_MODEL), seq_row)] * 3
                 + [pl.BlockSpec(memory_space=pl.ANY), pl.BlockSpec(memory_space=pl.ANY)],
        out_specs=pl.BlockSpec((1, 1, D_MODEL), seq_row),
        scratch_shapes=[pltpu.VMEM((2, N_HEADS, n_sel, page, HEAD_DIM), F32),
                        pltpu.VMEM((2, N_HEADS, n_sel, page, HEAD_DIM), F32),
                        pltpu.SemaphoreType.DMA((2,))],
    )
    return pl.pallas_call(
        functools.partial(_attn_sample_kernel, n_sel),
        grid_spec=grid_spec,
        out_shape=jax.ShapeDtypeStruct((n_seq, 1, D_MODEL), F32),
        compiler_params=_params(1),
        name="moba_sample",
    )(sel_pages.reshape(-1), q3, k3, v3, cache_k, cache_v)


def _oproj_kernel(o_ref, x2_ref, wo_ref, mg_ref, wr_ref, br_ref, x3_ref, hm_ref, route_ref):
    x3 = x2_ref[...] + _mm(o_ref[...], wo_ref[...], True)
    x3_ref[...] = x3
    _moe_prenorm_and_route(x3, mg_ref, wr_ref, br_ref, hm_ref, route_ref, True)


def _oproj(o, x2, w):
    n = o.shape[0]
    return pl.pallas_call(
        _oproj_kernel,
        grid=(1,),
        in_specs=[_full((n, D_MODEL)), _full((n, D_MODEL)), _full((D_MODEL, D_MODEL)), _full((1, D_MODEL)),
                  _full((D_MODEL, LANES)), _full((1, LANES))],
        out_specs=[_full((n, D_MODEL)), _full((n * ROW_PIECES, LANES)), _full((n, LANES))],
        out_shape=[jax.ShapeDtypeStruct((n, D_MODEL), F32), jax.ShapeDtypeStruct((n * ROW_PIECES, LANES), F32),
                   jax.ShapeDtypeStruct((n, LANES), F32)],
        compiler_params=_params(1),
        name="oproj_sample",
    )(o, x2, w["f32"]["b_w_o"], w["moe_norm_g"][1], w["f32"]["moe_w_route"][1], w["moe_b_route"][1])


def _combine_kernel(x_ref, yp_ref, route_ref, o_ref):
    o_ref[...] = _moe_residual(x_ref[...], yp_ref, route_ref)


def _combine(x, ypairs, route):
    n = x.shape[0]
    tm = min(n, 4 * ROW_TILE)
    row = lambda i: (i, 0)
    return pl.pallas_call(
        _combine_kernel,
        grid=(n // tm,),
        in_specs=[pl.BlockSpec((tm, D_MODEL), row), pl.BlockSpec((EXPERT_TOPK * tm * ROW_PIECES, LANES), row),
                  pl.BlockSpec((tm, LANES), row)],
        out_specs=pl.BlockSpec((tm, D_MODEL), row),
        out_shape=jax.ShapeDtypeStruct((n, D_MODEL), F32),
        compiler_params=_params(1),
        name="combine",
    )(x, ypairs, route)


def _rope_tables(pos):
    inv_freq = ROPE_THETA ** (-jnp.arange(ROT_HALF, dtype=F32) / ROT_HALF)
    ang = pos.astype(F32)[:, None] * inv_freq[None, :]
    cos, sin = jnp.cos(ang), jnp.sin(ang)
    rest = HEAD_DIM - ROT_DIM
    cos_tab = jnp.concatenate([cos, cos, jnp.ones((pos.shape[0], rest), F32)], axis=-1)
    sin_tab = jnp.concatenate([-sin, sin, jnp.zeros((pos.shape[0], rest), F32)], axis=-1)
    return cos_tab, sin_tab


def kernel(x_prompt, x_sample, cache_k, cache_v, page_table, a_norm_g, a_w_in, a_ln_g, a_ln_b, a_w_s, a_b_s, a_w_out, kv_norm_g, w_kv, k_norm_g, b_norm_g, b_w_q, q_norm_g, b_w_o, moe_norm_g, moe_w_rg, moe_b_rg, moe_w_re, moe_b_re, moe_w_gate, moe_w_up, moe_w_down):
    bsz, seq, _ = x_prompt.shape
    n_dec = x_sample.shape[0]
    assert x_sample.shape[1] == 1 and a_w_in.shape[0] == 1 and b_w_q.shape[0] == 1
    assert seq % ROW_TILE == 0 and ROW_TILE == MOBA_BLOCK
    page = cache_k.shape[1]
    past = page_table.shape[1] * page
    assert past % MOBA_BLOCK == 0 and past // MOBA_BLOCK >= MOBA_TOPK and MOBA_BLOCK % page == 0

    depth = moe_norm_g.shape[0]
    pad = LANES - N_EXPERT_GROUPS - N_EXPERTS
    mats = {
        "a_w_in": a_w_in[0], "a_w_out": a_w_out[0], "w_kv": w_kv, "b_w_q": b_w_q[0], "b_w_o": b_w_o[0],
        "moe_w_route": jnp.concatenate([moe_w_rg, moe_w_re, jnp.zeros((depth, D_MODEL, pad), F32)], axis=-1),
    }
    w = {
        "f32": mats, "bf16": {name: m.astype(BF16) for name, m in mats.items()},
        "moe_w_gate": moe_w_gate, "moe_w_up": moe_w_up, "moe_w_down": moe_w_down,
        "a_norm_g": a_norm_g[0][None], "a_ln_g": a_ln_g[0][None], "a_ln_b": a_ln_b[0][None],
        "a_w_s": a_w_s[0], "a_b_s_t": a_b_s[0].T,
        "a_ws0": jnp.repeat(a_w_s[0, :, 0, 0], GROUP_WIDTH)[None], "a_bs0": jnp.repeat(a_b_s[0, :, 0], GROUP_WIDTH)[None],
        "kv_norm_g": kv_norm_g[None], "k_norm_g": k_norm_g[None],
        "b_norm_g": b_norm_g[0][None], "q_norm_g": q_norm_g[0][None],
        "moe_norm_g": moe_norm_g[:, None, :],
        "moe_b_route": jnp.concatenate([moe_b_rg, moe_b_re, jnp.zeros((depth, pad), F32)], axis=-1)[:, None, :],
    }

    xp = x_prompt.reshape(bsz * seq, D_MODEL)
    xs = x_sample.reshape(n_dec, D_MODEL)
    cos_p, sin_p = _rope_tables(jnp.arange(seq, dtype=jnp.int32))
    cos_s, sin_s = _rope_tables(jnp.full((n_dec,), past, dtype=jnp.int32))

    x1p, vrows_p, hmp, route0p, kmean_s = _gmlp(xp, w, False, seq, cache_k, page_table)
    x1s, vrows_s, hms, route0s = _gmlp(xs, w, True, seq)
    yp = _experts(hmp, route0p, w, 0, EXPERT_TILE, False)
    ys = _experts(hms, route0s, w, 0, EXPERT_TILE_SMALL, True)

    x2p, k_p, v_p, q_p, kb_p, vt_p, kmean_p = _kvq(x1p, yp, route0p, cos_p, sin_p, w, True, False)
    x2s, k_s, v_s, q_s = _kvq(x1s, ys, route0s, cos_s, sin_s, w, False, True)

    x3p, hmp, route1p = _attention_prompt(q_p, kb_p, vt_p, kmean_p, x2p, w, seq)
    sel_blocks = _block_select(q_s.reshape(n_dec, N_HEADS, HEAD_DIM), kmean_s)[:, :, :MOBA_TOPK]
    pages_per_block = MOBA_BLOCK // page
    page_slots = sel_blocks[..., None] * pages_per_block + jnp.arange(pages_per_block, dtype=jnp.int32)
    page_slots = page_slots.reshape(n_dec, N_HEADS * MOBA_TOPK * pages_per_block)
    sel_pages = jnp.take_along_axis(page_table, page_slots, axis=1).reshape(n_dec, N_HEADS, MOBA_TOPK * pages_per_block)
    o_s = _attention_sample(q_s.reshape(n_dec, 1, D_MODEL), k_s.reshape(n_dec, 1, D_MODEL),
                            v_s.reshape(n_dec, 1, D_MODEL), cache_k, cache_v, sel_pages)
    x3s, hms, route1s = _oproj(o_s.reshape(n_dec, D_MODEL), x2s, w)
    yp = _experts(hmp, route1p, w, 1, EXPERT_TILE, False)
    ys = _experts(hms, route1s, w, 1, EXPERT_TILE_SMALL, True)
    y_prompt = _combine(x3p, yp, route1p)
    y_sample = _combine(x3s, ys, route1s)

    return (y_prompt.reshape(bsz, seq, D_MODEL), y_sample.reshape(n_dec, 1, D_MODEL),
            vrows_p.reshape(1, bsz, CHUNK, GATE_WIDTH), vrows_s.reshape(1, n_dec, 1, GATE_WIDTH),
            k_p.reshape(bsz, seq, N_HEADS, HEAD_DIM), v_p.reshape(bsz, seq, N_HEADS, HEAD_DIM),
            k_s.reshape(n_dec, 1, N_HEADS, HEAD_DIM), v_s.reshape(n_dec, 1, N_HEADS, HEAD_DIM))
```
